```python
import jax
import jax.numpy as jnp
from jax import lax
import numpy as np

D_MODEL = 2048
BATCH = 16
SEQ = 256
DEPTH = 2
DEC_BATCH = 4
DEC_SEQ = 2048
PAST_LEN = 256

GRID_W = 64
HEAD_DIM = 128
GLA_H = 8
GLA_DK = 64
GLA_DV = 128
GLA_RANK = 16
GLA_TAU = 16.0
GQA_H = 8
GQA_KV = 2
NA_H = 8
NA_WIN_R = 8
NA_WIN_C = 16
DN_H = 8
DN_DK = 128
DN_DV = 128
CONV_W = 3
CHUNK = 64
Q_BLOCK = 128
D_FF = 4 * D_MODEL
ROPE_THETA = 10000.0
ROPE_FREQ = HEAD_DIM // 4
EPS = 1e-6
N_EVEN = (DEPTH + 1) // 2
N_ODD = DEPTH // 2
EVEN_SPLIT = (GLA_H * GLA_DK, GLA_H * GLA_DK, GLA_H * GLA_DV, GLA_H * GLA_DV, 2 * GLA_RANK,
              GQA_H * HEAD_DIM, GQA_KV * HEAD_DIM, GQA_KV * HEAD_DIM)
EVEN_COLS = sum(EVEN_SPLIT)
EVEN_MIX = GLA_H * GLA_DV + GQA_H * HEAD_DIM
DN_QKV = DN_H * (2 * DN_DK + DN_DV)
ODD_SPLIT = (NA_H * HEAD_DIM, NA_H * HEAD_DIM, NA_H * HEAD_DIM, DN_QKV, DN_H * DN_DV, 2 * DN_H, 2 * DN_H)
ODD_COLS = sum(ODD_SPLIT)
ODD_MIX = NA_H * HEAD_DIM + DN_H * DN_DV

kernel_name = 'bidir_hybrid_dit_step'


def rmsnorm(x, g):
    xf = x.astype(jnp.float32)
    y = xf * lax.rsqrt(jnp.mean(xf * xf, axis=-1, keepdims=True) + EPS)
    return (y * g.astype(jnp.float32)).astype(x.dtype)


def l2norm(x):
    xf = x.astype(jnp.float32)
    return (xf * lax.rsqrt(jnp.sum(xf * xf, axis=-1, keepdims=True) + EPS)).astype(x.dtype)


def split_cols(x, sizes):
    return jnp.split(x, np.cumsum(sizes)[:-1].tolist(), axis=-1)


def rev(x):
    return jnp.flip(x, axis=1)


def adaln(cond, w, b):
    m = jax.nn.silu(cond) @ w + b
    return jnp.split(m[:, None, :], 6, axis=-1)


def axial_rope(n_tok):
    t = jnp.arange(n_tok)
    inv = 1.0 / (ROPE_THETA ** (jnp.arange(ROPE_FREQ, dtype=jnp.float32) / ROPE_FREQ))
    pos = jnp.stack([t // GRID_W, t % GRID_W], axis=1).astype(jnp.float32)
    ang = pos[:, :, None] * inv
    return jnp.cos(ang), jnp.sin(ang)


def apply_rope(x, cos, sin):
    b, t, h, d = x.shape
    xr = x.astype(jnp.float32).reshape(b, t, h, 2, 2, ROPE_FREQ)
    x1, x2 = xr[..., 0, :], xr[..., 1, :]
    c, s = cos[None, :, None], sin[None, :, None]
    out = jnp.stack([x1 * c - x2 * s, x2 * c + x1 * s], axis=-2)
    return out.reshape(b, t, h, d).astype(x.dtype)


def block_attention(q, k, v):
    b, t, h, d = q.shape
    kv = k.shape[2]
    qb = q.reshape(b, t // Q_BLOCK, Q_BLOCK, kv, h // kv, d).transpose(1, 0, 2, 3, 4, 5)

    def one(qi):
        s = jnp.einsum('bqhgd,bkhd->bhgqk', qi, k).astype(jnp.float32) * (d ** -0.5)
        p = jax.nn.softmax(s, axis=-1).astype(v.dtype)
        return jnp.einsum('bhgqk,bkhd->bqhgd', p, v)

    o = lax.map(one, qb)
    return o.transpose(1, 0, 2, 3, 4, 5).reshape(b, t, h, d)


def neighbourhood_attention(q, k, v, k_ctx, v_ctx, rpb):
    b, t, h, d = q.shape
    rows = t // GRID_W
    wr = min(NA_WIN_R, rows)
    nwin = wr * NA_WIN_C
    qg = q.reshape(b, rows, GRID_W, h, d)
    kg = k.reshape(b, rows, GRID_W, h, d)
    vg = v.reshape(b, rows, GRID_W, h, d)
    cols = jnp.arange(GRID_W)
    col_idx = jnp.clip(cols - NA_WIN_C // 2, 0, GRID_W - NA_WIN_C)[:, None] + jnp.arange(NA_WIN_C)
    col_bias = rpb[:, :, col_idx - cols[:, None] + NA_WIN_C - 1]

    def one_row(r):
        rs = jnp.clip(r - wr // 2, 0, rows - wr)
        q_r = lax.dynamic_index_in_dim(qg, r, axis=1, keepdims=False)
        k_win = lax.dynamic_slice_in_dim(kg, rs, wr, axis=1)[:, :, col_idx]
        v_win = lax.dynamic_slice_in_dim(vg, rs, wr, axis=1)[:, :, col_idx]
        bias = col_bias[:, rs + jnp.arange(wr) - r + NA_WIN_R - 1]
        s_win = jnp.einsum('bqhd,brqchd->bhqrc', q_r, k_win).astype(jnp.float32) * (d ** -0.5)
        s_win = s_win + jnp.transpose(bias, (0, 2, 1, 3)).astype(jnp.float32)
        s_ctx = jnp.einsum('bqhd,bkhd->bhqk', q_r, k_ctx).astype(jnp.float32) * (d ** -0.5)
        s = jnp.concatenate([s_win.reshape(b, h, GRID_W, nwin), s_ctx], axis=-1)
        p = jax.nn.softmax(s, axis=-1).astype(v.dtype)
        p_win = p[..., :nwin].reshape(b, h, GRID_W, wr, NA_WIN_C)
        return (jnp.einsum('bhqrc,brqchd->bqhd', p_win, v_win)
                + jnp.einsum('bhqk,bkhd->bqhd', p[..., nwin:], v_ctx))

    o = lax.map(one_row, jnp.arange(rows))
    return o.transpose(1, 0, 2, 3, 4).reshape(b, t, h, d)


def to_chunks(x):
    b, t, h, d = x.shape
    return x.astype(jnp.float32).reshape(b, t // CHUNK, CHUNK, h, d).transpose(1, 0, 3, 2, 4)


def from_chunks(o):
    n, b, h, c, d = o.shape
    return o.transpose(1, 0, 3, 2, 4).reshape(b, n * c, h, d)


def gla_chunked(q, k, v, log_a, s0):
    qc, kc, vc, ac = to_chunks(q), to_chunks(k), to_chunks(v), to_chunks(log_a)
    bcum = jnp.cumsum(ac, axis=3)
    b_last = bcum[:, :, :, -1:, :]
    qe = qc * jnp.exp(bcum)
    ke = kc * jnp.exp(-bcum)
    kd = kc * jnp.exp(b_last - bcum)
    tril = jnp.tril(jnp.ones((CHUNK, CHUNK), dtype=bool))
    scores = jnp.where(tril, jnp.einsum('nbhcd,nbhsd->nbhcs', qe, ke), 0.0)
    o_intra = jnp.einsum('nbhcs,nbhsv->nbhcv', scores, vc)
    dec = jnp.exp(b_last[:, :, :, 0, :])[..., None]

    def step(S, inp):
        qe_i, kd_i, v_i, dec_i = inp
        o_i = jnp.einsum('bhcd,bhdv->bhcv', qe_i, S)
        S = S * dec_i + jnp.einsum('bhcd,bhcv->bhdv', kd_i, v_i)
        return S, o_i

    S, o_inter = lax.scan(step, s0.astype(jnp.float32), (qe, kd, vc, dec))
    return from_chunks(o_inter + o_intra).astype(v.dtype), S


def delta_chunked(q, k, v, beta, g, s0):
    qc, kc, vc = to_chunks(q), to_chunks(k), to_chunks(v)
    bc = to_chunks(beta[..., None])[..., 0]
    gcum = jnp.cumsum(to_chunks(g[..., None])[..., 0], axis=-1)
    tril = jnp.tril(jnp.ones((CHUNK, CHUNK), dtype=bool))
    strict = jnp.tril(jnp.ones((CHUNK, CHUNK), dtype=bool), -1)
    diff = gcum[..., :, None] - gcum[..., None, :]
    L = jnp.where(tril, jnp.exp(jnp.where(tril, diff, 0.0)), 0.0)
    kb = kc * bc[..., None]
    M = jnp.where(strict, jnp.einsum('nbhcd,nbhsd->nbhcs', kb, kc) * L, 0.0)
    A = M + jnp.eye(CHUNK, dtype=jnp.float32)
    rhs = jnp.concatenate([vc * bc[..., None], kb * jnp.exp(gcum)[..., None]], axis=-1)
    sol = lax.linalg.triangular_solve(A, rhs, left_side=True, lower=True, unit_diagonal=True)
    dv = vc.shape[-1]
    u0, kcum = sol[..., :dv], sol[..., dv:]
    attn = jnp.einsum('nbhcd,nbhsd->nbhcs', qc, kc) * L
    qe = qc * jnp.exp(gcum)[..., None]
    g_last = gcum[..., -1:]
    kd = kc * jnp.exp(g_last - gcum)[..., None]
    dec = jnp.exp(g_last)[..., None]

    def step(S, inp):
        u0_i, kcum_i, qe_i, attn_i, kd_i, dec_i = inp
        u = u0_i - jnp.einsum('bhcd,bhdv->bhcv', kcum_i, S)
        o_i = jnp.einsum('bhcd,bhdv->bhcv', qe_i, S) + jnp.einsum('bhcs,bhsv->bhcv', attn_i, u)
        S = S * dec_i + jnp.einsum('bhcd,bhcv->bhdv', kd_i, u)
        return S, o_i

    S, o = lax.scan(step, s0.astype(jnp.float32), (u0, kcum, qe, attn, kd, dec))
    return from_chunks(o).astype(v.dtype), S


def centred_conv(x, w):
    pad = CONV_W // 2
    t = x.shape[1]
    xp = jnp.pad(x, ((0, 0), (pad, pad), (0, 0)))
    out = xp[:, 0:t] * w[0]
    for j in range(1, CONV_W):
        out = out + xp[:, j:j + t] * w[j]
    return out


def even_mixer(h, w_in, w_a2, b_a2, gla_norm, q_norm, k_norm, w_out, ctx, rope):
    b, t, _ = h.shape
    gq, gk, gv, gg, glo, aq, ak, av = split_cols(h @ w_in, EVEN_SPLIT)
    gq = gq.reshape(b, t, GLA_H, GLA_DK) * (GLA_DK ** -0.5)
    gk = gk.reshape(b, t, GLA_H, GLA_DK)
    gv = gv.reshape(b, t, GLA_H, GLA_DV)
    la = jnp.einsum('btdr,drk->btdk', glo.reshape(b, t, 2, GLA_RANK), w_a2) + b_a2
    la = (jax.nn.log_sigmoid(la.astype(jnp.float32)) / GLA_TAU).reshape(b, t, 2, GLA_H, GLA_DK)
    if ctx is None:
        s0 = jnp.zeros((b, 2, GLA_H, GLA_DK, GLA_DV), jnp.float32)
    else:
        s0 = ctx[0]
    o_f, s_f = gla_chunked(gq, gk, gv, la[:, :, 0], s0[:, 0])
    o_b, s_b = gla_chunked(rev(gq), rev(gk), rev(gv), rev(la[:, :, 1]), s0[:, 1])
    o_gla = rmsnorm(o_f + rev(o_b), gla_norm) * jax.nn.silu(gg.reshape(b, t, GLA_H, GLA_DV))

    aq = rmsnorm(aq.reshape(b, t, GQA_H, HEAD_DIM), q_norm)
    ak = rmsnorm(ak.reshape(b, t, GQA_KV, HEAD_DIM), k_norm)
    av = av.reshape(b, t, GQA_KV, HEAD_DIM)
    if ctx is None:
        o_att = block_attention(aq, ak, av)
        new_ctx = (jnp.stack([s_f, s_b], axis=1), ak, av)
    else:
        qr = apply_rope(aq, rope[0], rope[1])
        kr = apply_rope(ak, rope[0], rope[1])
        o_att = block_attention(qr, jnp.concatenate([ctx[1], kr], axis=1), jnp.concatenate([ctx[2], av], axis=1))
        new_ctx = None
    mix = jnp.concatenate([o_gla.reshape(b, t, -1), o_att.reshape(b, t, -1)], axis=-1)
    return mix @ w_out, new_ctx


def odd_mixer(h, w_in, conv_w, a_log, dt_bias, dn_norm, rpb, w_out, ctx):
    b, t, _ = h.shape
    nq, nk, nv, dqkv, dz, da, db = split_cols(h @ w_in, ODD_SPLIT)
    nq = nq.reshape(b, t, NA_H, HEAD_DIM)
    nk = nk.reshape(b, t, NA_H, HEAD_DIM)
    nv = nv.reshape(b, t, NA_H, HEAD_DIM)
    if ctx is None:
        o_na = block_attention(nq, nk, nv)
        s0 = jnp.zeros((b, 2, DN_H, DN_DK, DN_DV), jnp.float32)
    else:
        o_na = neighbourhood_attention(nq, nk, nv, ctx[1], ctx[2], rpb)
        s0 = ctx[0]
    dqkv = jax.nn.silu(centred_conv(dqkv, conv_w))
    dq, dk, dv = split_cols(dqkv, (DN_H * DN_DK, DN_H * DN_DK, DN_H * DN_DV))
    dq = l2norm(dq.reshape(b, t, DN_H, DN_DK)) * (DN_DK ** -0.5)
    dk = l2norm(dk.reshape(b, t, DN_H, DN_DK))
    dv = dv.reshape(b, t, DN_H, DN_DV)
    beta = jax.nn.sigmoid(db.reshape(b, t, 2, DN_H).astype(jnp.float32))
    g = -jnp.exp(a_log.astype(jnp.float32)) * jax.nn.softplus(
        da.reshape(b, t, 2, DN_H).astype(jnp.float32) + dt_bias.astype(jnp.float32))
    o_f, s_f = delta_chunked(dq, dk, dv, beta[:, :, 0], g[:, :, 0], s0[:, 0])
    o_b, s_b = delta_chunked(rev(dq), rev(dk), rev(dv), rev(beta[:, :, 1]), rev(g[:, :, 1]), s0[:, 1])
    o_dn = rmsnorm(o_f + rev(o_b), dn_norm) * jax.nn.silu(dz.reshape(b, t, DN_H, DN_DV))
    new_ctx = (jnp.stack([s_f, s_b], axis=1), nk, nv) if ctx is None else None
    mix = jnp.concatenate([o_na.reshape(b, t, -1), o_dn.reshape(b, t, -1)], axis=-1)
    return mix @ w_out, new_ctx


def setup_inputs(seed: int = 0) -> dict:
    key = jax.random.key(seed)
    ks = iter(jax.random.split(key, 40))

    def nrm(shape, scale):
        return jax.random.normal(next(ks), shape, jnp.float32) * scale

    D = D_MODEL
    a_log = jnp.log(jax.random.uniform(next(ks), (N_ODD, 2, DN_H), jnp.float32, 1.0, 16.0))
    dt = jnp.exp(jax.random.uniform(next(ks), (N_ODD, 2, DN_H), jnp.float32,
                                    float(np.log(1e-3)), float(np.log(1e-1))))
    dt_bias = dt + jnp.log(-jnp.expm1(-dt))
    return {
        'x_prompt': nrm((BATCH, SEQ, D), 1.0),
        'x_sample': nrm((DEC_BATCH, DEC_SEQ, D), 1.0),
        'state_gla': nrm((DEC_BATCH, N_EVEN, 2, GLA_H, GLA_DK, GLA_DV), 0.5),
        'cache_gqa_k': nrm((DEC_BATCH, N_EVEN, PAST_LEN, GQA_KV, HEAD_DIM), 1.0),
        'cache_gqa_v': nrm((DEC_BATCH, N_EVEN, PAST_LEN, GQA_KV, HEAD_DIM), 1.0),
        'cache_na_k': nrm((DEC_BATCH, N_ODD, PAST_LEN, NA_H, HEAD_DIM), 1.0),
        'cache_na_v': nrm((DEC_BATCH, N_ODD, PAST_LEN, NA_H, HEAD_DIM), 1.0),
        'state_delta': nrm((DEC_BATCH, N_ODD, 2, DN_H, DN_DK, DN_DV), 0.5),
        'c': nrm((DEC_BATCH, D), 1.0),
        'c_ctx': nrm((D,), 1.0),
        'norm1': 1.0 + nrm((DEPTH, D), 0.1),
        'norm2': 1.0 + nrm((DEPTH, D), 0.1),
        'w_ada': nrm((DEPTH, D, 6 * D), 0.5 * D ** -0.5),
        'b_ada': nrm((DEPTH, 6 * D), 0.01),
        'w_mlp1': nrm((DEPTH, D, D_FF), D ** -0.5),
        'w_mlp2': nrm((DEPTH, D_FF, D), D_FF ** -0.5),
        'ev_w_in': nrm((N_EVEN, D, EVEN_COLS), D ** -0.5),
        'ev_w_a2': nrm((N_EVEN, 2, GLA_RANK, GLA_H * GLA_DK), GLA_RANK ** -0.5),
        'ev_b_a2': nrm((N_EVEN, 2, GLA_H * GLA_DK), 0.1),
        'ev_gla_norm': 1.0 + nrm((N_EVEN, GLA_DV), 0.1),
        'ev_q_norm': 1.0 + nrm((N_EVEN, HEAD_DIM), 0.1),
        'ev_k_norm': 1.0 + nrm((N_EVEN, HEAD_DIM), 0.1),
        'ev_w_out': nrm((N_EVEN, EVEN_MIX, D), EVEN_MIX ** -0.5),
        'od_w_in': nrm((N_ODD, D, ODD_COLS), D ** -0.5),
        'od_conv': nrm((N_ODD, CONV_W, DN_QKV), CONV_W ** -0.5),
        'od_a_log': a_log,
        'od_dt_bias': dt_bias,
        'od_dn_norm': 1.0 + nrm((N_ODD, DN_DV), 0.1),
        'od_rpb': nrm((N_ODD, NA_H, 2 * NA_WIN_R - 1, 2 * NA_WIN_C - 1), 0.1),
        'od_w_out': nrm((N_ODD, ODD_MIX, D), ODD_MIX ** -0.5),
        'norm_f': 1.0 + nrm((D,), 0.1),
    }


def reference(x_prompt, x_sample, state_gla, cache_gqa_k, cache_gqa_v, cache_na_k, cache_na_v, state_delta,
              c, c_ctx, norm1, norm2, w_ada, b_ada, w_mlp1, w_mlp2,
              ev_w_in, ev_w_a2, ev_b_a2, ev_gla_norm, ev_q_norm, ev_k_norm, ev_w_out,
              od_w_in, od_conv, od_a_log, od_dt_bias, od_dn_norm, od_rpb, od_w_out, norm_f):

    def trunk(x, cond, caches, rope):
        new = []
        for i in range(DEPTH):
            j = i // 2
            sh1, sc1, gt1, sh2, sc2, gt2 = adaln(cond, w_ada[i], b_ada[i])
            h = rmsnorm(x, norm1[i]) * (1.0 + sc1) + sh1
            ctx = None if caches is None else caches[i]
            if i % 2 == 0:
                y, nc = even_mixer(h, ev_w_in[j], ev_w_a2[j], ev_b_a2[j], ev_gla_norm[j], ev_q_norm[j],
                                   ev_k_norm[j], ev_w_out[j], ctx, rope)
            else:
                y, nc = odd_mixer(h, od_w_in[j], od_conv[j], od_a_log[j], od_dt_bias[j], od_dn_norm[j],
                                  od_rpb[j], od_w_out[j], ctx)
            new.append(nc)
            x = x + gt1 * y
            h = rmsnorm(x, norm2[i]) * (1.0 + sc2) + sh2
            x = x + gt2 * (jnp.square(jax.nn.relu(h @ w_mlp1[i])) @ w_mlp2[i])
        return rmsnorm(x, norm_f), new

    y_prompt, new_p = trunk(x_prompt, c_ctx[None, :], None, None)
    st_gla = jnp.stack([new_p[i][0] for i in range(0, DEPTH, 2)], axis=1)
    ck_gqa = jnp.stack([new_p[i][1] for i in range(0, DEPTH, 2)], axis=1)
    cv_gqa = jnp.stack([new_p[i][2] for i in range(0, DEPTH, 2)], axis=1)
    ck_na = jnp.stack([new_p[i][1] for i in range(1, DEPTH, 2)], axis=1)
    cv_na = jnp.stack([new_p[i][2] for i in range(1, DEPTH, 2)], axis=1)
    st_dn = jnp.stack([new_p[i][0] for i in range(1, DEPTH, 2)], axis=1)

    caches = [(state_gla[:, i // 2], cache_gqa_k[:, i // 2], cache_gqa_v[:, i // 2]) if i % 2 == 0
              else (state_delta[:, i // 2], cache_na_k[:, i // 2], cache_na_v[:, i // 2])
              for i in range(DEPTH)]
    y_sample, _ = trunk(x_sample, c, caches, axial_rope(x_sample.shape[1]))

    return (y_prompt, y_sample, st_gla, ck_gqa, cv_gqa, ck_na, cv_na, st_dn)
```

```python
import functools

import jax
import jax.numpy as jnp
import numpy as np
from jax import lax
from jax.experimental import pallas as pl
from jax.experimental.pallas import tpu as pltpu

F32 = jnp.float32
BF16 = jnp.bfloat16
HIGHEST = lax.Precision.HIGHEST

D_MODEL = 2048
D_FF = 4 * D_MODEL
HEAD_DIM = 128
LANES = 128
GRID_W = 64
GLA_H = 8
GLA_DK = 64
GLA_DV = 128
GLA_RANK = 16
GLA_TAU = 16.0
GQA_H = 8
GQA_KV = 2
NA_H = 8
NA_WIN_R = 8
NA_WIN_C = 16
DN_H = 8
CHUNK = 64
ROPE_THETA = 10000.0
ROPE_FREQ = HEAD_DIM // 4
EPS = 1e-6
MASK_VALUE = -1e30

EV_Q, EV_K, EV_V, EV_G = 0, 512, 1024, 2048
EV_AQ, EV_AK, EV_AV, EV_LO = 3072, 4096, 4352, 4608
EV_COLS_PAD = 5120
OD_NQ, OD_NK, OD_NV = 0, 1024, 2048
OD_DQ, OD_DK, OD_DV, OD_DZ, OD_AB = 3072, 4096, 5120, 6144, 7168
OD_COLS_PAD = 7680

VMEM_LIMIT = 56 * 1024 * 1024


def _params(*sem):
    return pltpu.CompilerParams(dimension_semantics=sem, vmem_limit_bytes=VMEM_LIMIT)


def _dot(a, b):
    return jnp.dot(a, b, preferred_element_type=F32)


def _dot_nt(a, b):
    return lax.dot_general(a, b, (((1,), (1,)), ((), ())), preferred_element_type=F32)


def _dot_tn(a, b):
    return lax.dot_general(a, b, (((0,), (0,)), ((), ())), preferred_element_type=F32)


def _dot_hi(a, b):
    return jnp.dot(a, b, precision=HIGHEST, preferred_element_type=F32)


def _sigmoid(x):
    return 1.0 / (1.0 + jnp.exp(-x))


def _silu(x):
    return x * _sigmoid(x)


def _softplus(x):
    return jnp.maximum(x, 0.0) + jnp.log(1.0 + jnp.exp(-jnp.abs(x)))


def _rms(x, g):
    return x * lax.rsqrt(jnp.mean(x * x, axis=-1, keepdims=True) + EPS) * g


def _rms_mod(x, g, sc, sh):
    return _rms(x, g) * (1.0 + sc) + sh


def _adaln_kernel(c_ref, w_ref, b_ref, o_ref):
    a = _silu(c_ref[...]).astype(BF16)
    o_ref[...] = _dot(a, w_ref[...].astype(BF16)) + b_ref[...]


def _adaln(cond8, w_ada, b_ada):
    depth, d, n = w_ada.shape
    tn = 512
    return pl.pallas_call(
        _adaln_kernel,
        grid=(depth, n // tn),
        in_specs=[
            pl.BlockSpec((8, d), lambda l, j: (0, 0)),
            pl.BlockSpec((None, d, tn), lambda l, j: (l, 0, j)),
            pl.BlockSpec((None, 1, tn), lambda l, j: (l, 0, j)),
        ],
        out_specs=pl.BlockSpec((None, 8, tn), lambda l, j: (l, 0, j)),
        out_shape=jax.ShapeDtypeStruct((depth, 8, n), F32),
        compiler_params=_params("parallel", "parallel"),
        name="adaln",
    )(cond8, w_ada, b_ada.reshape(depth, 1, n))


def _mod_spec(layer, k, row_fn):
    return pl.BlockSpec((None, None, None, 1, D_MODEL), lambda i, j: (layer, row_fn(i), k, 0, 0))


def _inproj_kernel(x_ref, g_ref, sc_ref, sh_ref, w_ref, o_ref, h_ref):
    @pl.when(pl.program_id(1) == 0)
    def _():
        h_ref[...] = _rms_mod(x_ref[...], g_ref[...], sc_ref[...], sh_ref[...]).astype(BF16)

    o_ref[...] = _dot(h_ref[...], w_ref[...])


def _inproj(x, g, mod, layer, row_fn, w, tm):
    m, d = x.shape
    n = w.shape[1]
    tn = 512
    return pl.pallas_call(
        _inproj_kernel,
        grid=(m // tm, n // tn),
        in_specs=[
            pl.BlockSpec((tm, d), lambda i, j: (i, 0)),
            pl.BlockSpec((1, d), lambda i, j: (0, 0)),
            _mod_spec(layer, 1, row_fn),
            _mod_spec(layer, 0, row_fn),
            pl.BlockSpec((d, tn), lambda i, j: (0, j)),
        ],
        out_specs=pl.BlockSpec((tm, tn), lambda i, j: (i, j)),
        out_shape=jax.ShapeDtypeStruct((m, n), F32),
        scratch_shapes=[pltpu.VMEM((tm, d), BF16)],
        compiler_params=_params("parallel", "arbitrary"),
        name="inproj",
    )(x, g, mod, mod, w)


def _outproj_kernel(a_ref, b_ref, wa_ref, wb_ref, x_ref, gt_ref, o_ref):
    acc = _dot(a_ref[...], wa_ref[...]) + _dot(b_ref[...], wb_ref[...])
    o_ref[...] = x_ref[...] + gt_ref[...] * acc


def _outproj(mix_a, mix_b, w, x, mod, layer, row_fn, tm):
    m, d = x.shape
    ka, kb = mix_a.shape[1], mix_b.shape[1]
    tn = 1024
    gt_spec = pl.BlockSpec((None, None, None, 1, tn), lambda i, j: (layer, row_fn(i), 2, 0, j))
    return pl.pallas_call(
        _outproj_kernel,
        grid=(m // tm, d // tn),
        in_specs=[
            pl.BlockSpec((tm, ka), lambda i, j: (i, 0)),
            pl.BlockSpec((tm, kb), lambda i, j: (i, 0)),
            pl.BlockSpec((ka, tn), lambda i, j: (0, j)),
            pl.BlockSpec((kb, tn), lambda i, j: (1, j)),
            pl.BlockSpec((tm, tn), lambda i, j: (i, j)),
            gt_spec,
        ],
        out_specs=pl.BlockSpec((tm, tn), lambda i, j: (i, j)),
        out_shape=jax.ShapeDtypeStruct((m, d), F32),
        compiler_params=_params("parallel", "parallel"),
        name="outproj",
    )(mix_a, mix_b, w, w, x, mod)


def _mlp_kernel(x_ref, g_ref, sc_ref, sh_ref, gt_ref, w1_ref, w2_ref, gf_ref, o_ref, h_ref, acc_ref, *, final_norm):
    j = pl.program_id(1)

    @pl.when(j == 0)
    def _():
        h_ref[...] = _rms_mod(x_ref[...], g_ref[...], sc_ref[...], sh_ref[...]).astype(BF16)
        acc_ref[...] = jnp.zeros_like(acc_ref)

    hid = jnp.maximum(_dot(h_ref[...], w1_ref[...]), 0.0)
    acc_ref[...] += _dot((hid * hid).astype(BF16), w2_ref[...])

    @pl.when(j == pl.num_programs(1) - 1)
    def _():
        y = x_ref[...] + gt_ref[...] * acc_ref[...]
        if final_norm:
            y = _rms(y, gf_ref[...])
        o_ref[...] = y


def _mlp(x, g, mod, layer, row_fn, w1, w2, gf, final_norm, tm):
    m, d = x.shape
    ff = w1.shape[1]
    tf = 512
    return pl.pallas_call(
        functools.partial(_mlp_kernel, final_norm=final_norm),
        grid=(m // tm, ff // tf),
        in_specs=[
            pl.BlockSpec((tm, d), lambda i, j: (i, 0)),
            pl.BlockSpec((1, d), lambda i, j: (0, 0)),
            _mod_spec(layer, 4, row_fn),
            _mod_spec(layer, 3, row_fn),
            _mod_spec(layer, 5, row_fn),
            pl.BlockSpec((d, tf), lambda i, j: (0, j)),
            pl.BlockSpec((tf, d), lambda i, j: (j, 0)),
            pl.BlockSpec((1, d), lambda i, j: (0, 0)),
        ],
        out_specs=pl.BlockSpec((tm, d), lambda i, j: (i, 0)),
        out_shape=jax.ShapeDtypeStruct((m, d), F32),
        scratch_shapes=[pltpu.VMEM((tm, d), BF16), pltpu.VMEM((tm, d), F32)],
        compiler_params=_params("parallel", "arbitrary"),
        name="mlp",
    )(x, g, mod, mod, mod, w1, w2, gf)


def _rope(x, cos, sin):
    lane = lax.broadcasted_iota(jnp.int32, x.shape, 1)
    first_half = (lane % (2 * ROPE_FREQ)) < ROPE_FREQ
    partner = jnp.where(first_half, pltpu.roll(x, LANES - ROPE_FREQ, 1), pltpu.roll(x, ROPE_FREQ, 1))
    return x * cos + partner * sin


def _attn_kernel(*refs, group, use_norm, use_rope, has_ctx, write_k):
    it = iter(refs)
    q_ref, k_ref, v_ref = next(it), next(it), next(it)
    qn_ref = kn_ref = cq_ref = sq_ref = ck_ref = sk_ref = kc_ref = vc_ref = knew_ref = None
    if use_norm:
        qn_ref, kn_ref = next(it), next(it)
    if use_rope:
        cq_ref, sq_ref, ck_ref, sk_ref = next(it), next(it), next(it), next(it)
    if has_ctx:
        kc_ref, vc_ref = next(it), next(it)
    o_ref = next(it)
    if write_k:
        knew_ref = next(it)
    kbuf, vbuf = next(it), next(it)
    scale = HEAD_DIM ** -0.5

    @pl.when(pl.program_id(2) == 0)
    def _():
        k = k_ref[...]
        if use_norm:
            k = _rms(k, kn_ref[...])
        if write_k:
            knew_ref[...] = k
        if use_rope:
            k = _rope(k, ck_ref[...], sk_ref[...])
        kbuf[...] = k.astype(BF16)
        vbuf[...] = v_ref[...].astype(BF16)

    for g in range(group):
        cols = slice(g * HEAD_DIM, (g + 1) * HEAD_DIM)
        q = q_ref[:, cols]
        if use_norm:
            q = _rms(q, qn_ref[...])
        if use_rope:
            q = _rope(q, cq_ref[...], sq_ref[...])
        qb = q.astype(BF16)
        s = _dot_nt(qb, kbuf[...]) * scale
        m = jnp.max(s, axis=-1, keepdims=True)
        if has_ctx:
            s_c = _dot_nt(qb, kc_ref[...].astype(BF16)) * scale
            m = jnp.maximum(m, jnp.max(s_c, axis=-1, keepdims=True))
        p = jnp.exp(s - m)
        l = jnp.sum(p, axis=-1, keepdims=True)
        o = _dot(p.astype(BF16), vbuf[...])
        if has_ctx:
            p_c = jnp.exp(s_c - m)
            l = l + jnp.sum(p_c, axis=-1, keepdims=True)
            o = o + _dot(p_c.astype(BF16), vc_ref[...].astype(BF16))
        o_ref[:, cols] = (o / l).astype(o_ref.dtype)


def _attention(proj, n_batch, seq, q_col, k_col, v_col, n_kv, group, norms=None, rope=None, ctx=None,
               write_k=False):
    tq = min(seq, 256)
    nq = seq // tq
    gw = group * HEAD_DIM
    in_specs = [
        pl.BlockSpec((tq, gw), lambda b, h, i: (b * nq + i, q_col // gw + h)),
        pl.BlockSpec((seq, HEAD_DIM), lambda b, h, i: (b, k_col // HEAD_DIM + h)),
        pl.BlockSpec((seq, HEAD_DIM), lambda b, h, i: (b, v_col // HEAD_DIM + h)),
    ]
    args = [proj, proj, proj]
    vec = pl.BlockSpec((1, HEAD_DIM), lambda b, h, i: (0, 0))
    if norms is not None:
        in_specs += [vec, vec]
        args += [norms[0], norms[1]]
    if rope is not None:
        tab_q = pl.BlockSpec((tq, HEAD_DIM), lambda b, h, i: (i, 0))
        tab_k = pl.BlockSpec((seq, HEAD_DIM), lambda b, h, i: (0, 0))
        in_specs += [tab_q, tab_q, tab_k, tab_k]
        args += [rope[0], rope[1], rope[0], rope[1]]
    if ctx is not None:
        past = ctx[0].shape[1]
        c_spec = pl.BlockSpec((None, past, HEAD_DIM), lambda b, h, i: (b, 0, h))
        in_specs += [c_spec, c_spec]
        args += [ctx[0], ctx[1]]
    o_spec = pl.BlockSpec((tq, gw), lambda b, h, i: (b * nq + i, h))
    o_shape = jax.ShapeDtypeStruct((n_batch * seq, n_kv * gw), BF16)
    if write_k:
        out_specs = [o_spec, pl.BlockSpec((seq, HEAD_DIM), lambda b, h, i: (b, h))]
        out_shape = [o_shape, jax.ShapeDtypeStruct((n_batch * seq, n_kv * HEAD_DIM), F32)]
    else:
        out_specs, out_shape = o_spec, o_shape
    return pl.pallas_call(
        functools.partial(_attn_kernel, group=group, use_norm=norms is not None, use_rope=rope is not None,
                          has_ctx=ctx is not None, write_k=write_k),
        grid=(n_batch, n_kv, nq),
        in_specs=in_specs,
        out_specs=out_specs,
        out_shape=out_shape,
        scratch_shapes=[pltpu.VMEM((seq, HEAD_DIM), BF16), pltpu.VMEM((seq, HEAD_DIM), BF16)],
        compiler_params=_params("parallel", "parallel", "arbitrary"),
        name="attention",
    )(*args)


def _na_row_start(r, rows):
    return jnp.clip(r - NA_WIN_R // 2, 0, rows - NA_WIN_R)


def _na_kernel(q_ref, k_ref, v_ref, kc_ref, vc_ref, bias_ref, o_ref, *, rows):
    r = pl.program_id(2)
    rs = _na_row_start(r, rows)
    delta = r - rs
    off = pl.multiple_of(rs * GRID_W, GRID_W)
    win = NA_WIN_R * GRID_W
    scale = HEAD_DIM ** -0.5
    qb = q_ref[...].astype(BF16)
    kw = k_ref[pl.ds(off, win), :].astype(BF16)
    vw = v_ref[pl.ds(off, win), :].astype(BF16)
    s_w = _dot_nt(qb, kw) * scale
    bias = jnp.concatenate([bias_ref[2 * jj - delta + NA_WIN_R - 1] for jj in range(NA_WIN_R // 2)], axis=-1)
    s_w = s_w + bias
    s_c = _dot_nt(qb, kc_ref[...].astype(BF16)) * scale
    m = jnp.maximum(jnp.max(s_w, axis=-1, keepdims=True), jnp.max(s_c, axis=-1, keepdims=True))
    p_w = jnp.exp(s_w - m)
    p_c = jnp.exp(s_c - m)
    l = jnp.sum(p_w, axis=-1, keepdims=True) + jnp.sum(p_c, axis=-1, keepdims=True)
    o = _dot(p_w.astype(BF16), vw) + _dot(p_c.astype(BF16), vc_ref[...].astype(BF16))
    o_ref[...] = (o / l).astype(o_ref.dtype)


def _na_bias_table(rpb):
    cols = jnp.arange(GRID_W)
    start = jnp.clip(cols - NA_WIN_C // 2, 0, GRID_W - NA_WIN_C)
    inside = (cols[None, :] >= start[:, None]) & (cols[None, :] < start[:, None] + NA_WIN_C)
    rel = jnp.clip(cols[None, :] - cols[:, None] + NA_WIN_C - 1, 0, 2 * NA_WIN_C - 2)
    tab = jnp.where(inside[None, None], rpb[:, :, rel], MASK_VALUE)
    return jnp.concatenate([tab[:, :-1], tab[:, 1:]], axis=-1)


def _neighbourhood(proj, n_batch, seq, k_ctx, v_ctx, bias):
    rows = seq // GRID_W
    past = k_ctx.shape[1]
    c_spec = pl.BlockSpec((None, past, HEAD_DIM), lambda b, h, r: (b, 0, h))
    return pl.pallas_call(
        functools.partial(_na_kernel, rows=rows),
        grid=(n_batch, NA_H, rows),
        in_specs=[
            pl.BlockSpec((GRID_W, HEAD_DIM), lambda b, h, r: (b * rows + r, OD_NQ // HEAD_DIM + h)),
            pl.BlockSpec((seq, HEAD_DIM), lambda b, h, r: (b, OD_NK // HEAD_DIM + h)),
            pl.BlockSpec((seq, HEAD_DIM), lambda b, h, r: (b, OD_NV // HEAD_DIM + h)),
            c_spec,
            c_spec,
            pl.BlockSpec((None, 2 * NA_WIN_R - 2, GRID_W, 2 * GRID_W), lambda b, h, r: (h, 0, 0, 0)),
        ],
        out_specs=pl.BlockSpec((GRID_W, HEAD_DIM), lambda b, h, r: (b * rows + r, h)),
        out_shape=jax.ShapeDtypeStruct((n_batch * seq, NA_H * HEAD_DIM), BF16),
        compiler_params=_params("parallel", "parallel", "arbitrary"),
        name="neighbourhood",
    )(proj, proj, proj, k_ctx, v_ctx, bias)


def _order_masks():
    row = lax.broadcasted_iota(jnp.int32, (CHUNK, CHUNK), 0)
    col = lax.broadcasted_iota(jnp.int32, (CHUNK, CHUNK), 1)
    return row, col


def _gla_kernel(*refs, seq, has_s0, write_state):
    it = iter(refs)
    q_ref, k_ref, v_ref, gg_ref, lo_ref, w2_ref, b2_ref, gn_ref = (next(it) for _ in range(8))
    s0_ref = next(it) if has_s0 else None
    mix_ref = next(it)
    st_ref = next(it) if write_state else None
    oacc, la_s = next(it), next(it)
    n_chunks = seq // CHUNK
    lane = lax.broadcasted_iota(jnp.int32, (1, LANES), 1)
    head_mask = [(lane < GLA_DK).astype(F32), (lane >= GLA_DK).astype(F32)]
    row, col = _order_masks()

    lo = lo_ref[...].astype(BF16)
    for d in range(2):
        z = _dot(lo, w2_ref[d].astype(BF16)) + b2_ref[d]
        la_s[d] = -_softplus(-z) * (1.0 / GLA_TAU)

    for d in range(2):
        before = (col <= row) if d == 0 else (col >= row)
        before_f = before.astype(F32)

        def body(i, carry, d=d, before=before, before_f=before_f):
            n = i if d == 0 else n_chunks - 1 - i
            off = pl.multiple_of(n * CHUNK, CHUNK)
            a = la_s[d, pl.ds(off, CHUNK), :]
            bc = _dot_hi(before_f, a)
            bl = bc[CHUNK - 1:CHUNK, :] if d == 0 else bc[0:1, :]
            q = q_ref[pl.ds(off, CHUNK), :] * (GLA_DK ** -0.5)
            k = k_ref[pl.ds(off, CHUNK), :]
            qe = q * jnp.exp(bc)
            ke = (k * jnp.exp(-bc)).astype(BF16)
            kd = k * jnp.exp(bl - bc)
            dec = jnp.exp(bl)
            new = []
            for p in range(2):
                cols = slice(p * GLA_DV, (p + 1) * GLA_DV)
                qep = (qe * head_mask[p]).astype(BF16)
                kdp = (kd * head_mask[p]).astype(BF16)
                vp = v_ref[pl.ds(off, CHUNK), cols].astype(BF16)
                sc = jnp.where(before, _dot_nt(qep, ke), 0.0)
                st = carry[p]
                o = _dot(sc.astype(BF16), vp) + _dot_nt(qep, st.astype(BF16))
                new.append(st * dec + _dot_tn(vp, kdp))
                if d == 0:
                    oacc[pl.ds(off, CHUNK), cols] = o
                else:
                    oacc[pl.ds(off, CHUNK), cols] += o
            return tuple(new)

        if has_s0:
            s0t = s0_ref[d].T
            init = (s0t * head_mask[0], s0t * head_mask[1])
        else:
            init = (jnp.zeros((GLA_DV, LANES), F32), jnp.zeros((GLA_DV, LANES), F32))
        fin = lax.fori_loop(0, n_chunks, body, init)
        if write_state:
            st_ref[d] = (fin[0] + fin[1]).T

    for p in range(2):
        cols = slice(p * GLA_DV, (p + 1) * GLA_DV)
        y = _rms(oacc[:, cols], gn_ref[...]) * _silu(gg_ref[:, cols])
        mix_ref[:, cols] = y.astype(mix_ref.dtype)


def _gla(proj, n_batch, seq, w2pad, b2, gn, s0, write_state):
    pairs = GLA_H // 2
    pw = 2 * GLA_DV
    st_spec = pl.BlockSpec((None, 2, None, LANES, GLA_DV), lambda b, p: (b, 0, p, 0, 0))
    in_specs = [
        pl.BlockSpec((seq, LANES), lambda b, p: (b, EV_Q // LANES + p)),
        pl.BlockSpec((seq, LANES), lambda b, p: (b, EV_K // LANES + p)),
        pl.BlockSpec((seq, pw), lambda b, p: (b, EV_V // pw + p)),
        pl.BlockSpec((seq, pw), lambda b, p: (b, EV_G // pw + p)),
        pl.BlockSpec((seq, LANES), lambda b, p: (b, EV_LO // LANES)),
        pl.BlockSpec((2, LANES, LANES), lambda b, p: (0, 0, p)),
        pl.BlockSpec((2, 1, LANES), lambda b, p: (0, 0, p)),
        pl.BlockSpec((1, GLA_DV), lambda b, p: (0, 0)),
    ]
    args = [proj, proj, proj, proj, proj, w2pad, b2, gn]
    if s0 is not None:
        in_specs.append(st_spec)
        args.append(s0)
    mix_spec = pl.BlockSpec((seq, pw), lambda b, p: (b, p))
    mix_shape = jax.ShapeDtypeStruct((n_batch * seq, GLA_H * GLA_DV), BF16)
    if write_state:
        out_specs = [mix_spec, st_spec]
        out_shape = [mix_shape, jax.ShapeDtypeStruct((n_batch, 2, pairs, LANES, GLA_DV), F32)]
    else:
        out_specs, out_shape = mix_spec, mix_shape
    return pl.pallas_call(
        functools.partial(_gla_kernel, seq=seq, has_s0=s0 is not None, write_state=write_state),
        grid=(n_batch, pairs),
        in_specs=in_specs,
        out_specs=out_specs,
        out_shape=out_shape,
        scratch_shapes=[pltpu.VMEM((seq, pw), F32), pltpu.VMEM((2, seq, LANES), F32)],
        compiler_params=_params("parallel", "parallel"),
        name="gla",
    )(*args)


def _conv_silu(x, w):
    seq = x.shape[0]
    t = lax.broadcasted_iota(jnp.int32, x.shape, 0)
    prev = jnp.where(t == 0, 0.0, pltpu.roll(x, 1, 0))
    nxt = jnp.where(t == seq - 1, 0.0, pltpu.roll(x, seq - 1, 0))
    return _silu(prev * w[0:1, :] + x * w[1:2, :] + nxt * w[2:3, :])


def _l2norm(x):
    return x * lax.rsqrt(jnp.sum(x * x, axis=-1, keepdims=True) + EPS)


def _delta_kernel(*refs, seq, has_s0, write_state):
    it = iter(refs)
    (q_ref, k_ref, v_ref, z_ref, ab_ref, wq_ref, wk_ref, wv_ref, alog_ref, dtb_ref,
     gn_ref) = (next(it) for _ in range(11))
    s0_ref = next(it) if has_s0 else None
    mix_ref = next(it)
    st_ref = next(it) if write_state else None
    qs, ks, vs, oacc, g_s, b_s = (next(it) for _ in range(6))
    n_chunks = seq // CHUNK
    h = pl.program_id(1)
    lane = lax.broadcasted_iota(jnp.int32, (1, LANES), 1)
    row, col = _order_masks()
    eye = (row == col).astype(F32)

    qs[...] = _l2norm(_conv_silu(q_ref[...], wq_ref[...])) * (HEAD_DIM ** -0.5)
    ks[...] = _l2norm(_conv_silu(k_ref[...], wk_ref[...]))
    vs[...] = _conv_silu(v_ref[...], wv_ref[...])
    ab = ab_ref[...]
    g_all = -jnp.exp(alog_ref[...]) * _softplus(ab + dtb_ref[...])
    beta_all = _sigmoid(ab)
    for d in range(2):
        sel_g = lane == d * DN_H + h
        sel_b = lane == 2 * DN_H + d * DN_H + h
        g_s[d] = jnp.broadcast_to(jnp.sum(jnp.where(sel_g, g_all, 0.0), axis=-1, keepdims=True), (seq, LANES))
        b_s[d] = jnp.broadcast_to(jnp.sum(jnp.where(sel_b, beta_all, 0.0), axis=-1, keepdims=True), (seq, LANES))

    for d in range(2):
        before = (col <= row) if d == 0 else (col >= row)
        before_f = before.astype(F32)
        after_f = ((col >= row) if d == 0 else (col <= row)).astype(F32)
        strict = (col < row) if d == 0 else (col > row)
        ones = jnp.ones((CHUNK, CHUNK), F32)

        def body(i, s, d=d, before=before, before_f=before_f, after_f=after_f, strict=strict, ones=ones):
            n = i if d == 0 else n_chunks - 1 - i
            off = pl.multiple_of(n * CHUNK, CHUNK)
            gb = g_s[d, pl.ds(off, CHUNK), :]
            bb = b_s[d, pl.ds(off, CHUNK), :]
            q = qs[pl.ds(off, CHUNK), :]
            k = ks[pl.ds(off, CHUNK), :]
            v = vs[pl.ds(off, CHUNK), :]
            gc = _dot_hi(before_f, gb)
            gr = _dot_hi(ones, gb[:, :CHUNK] * after_f)
            decay = jnp.where(before, jnp.exp(jnp.where(before, gc[:, :CHUNK] - gr, 0.0)), 0.0)
            kb16 = k.astype(BF16)
            kbeta = k * bb
            mm = jnp.where(strict, _dot_nt(kbeta.astype(BF16), kb16) * decay, 0.0)
            pw = -mm
            inv = eye + pw
            for _ in range(int(np.log2(CHUNK)) - 1):
                pw = _dot_hi(pw, pw)
                inv = inv + _dot_hi(pw, inv)
            egc = jnp.exp(gc)
            u0 = _dot_hi(inv, v * bb)
            kcum = _dot_hi(inv, kbeta * egc)
            attn = _dot_nt(q.astype(BF16), kb16) * decay
            g_last = gc[CHUNK - 1:CHUNK, :] if d == 0 else gc[0:1, :]
            qe = (q * egc).astype(BF16)
            kd = (k * jnp.exp(g_last - gc)).astype(BF16)
            s16 = s.astype(BF16)
            u = u0 - _dot(kcum.astype(BF16), s16)
            u16 = u.astype(BF16)
            o = _dot(qe, s16) + _dot(attn.astype(BF16), u16)
            if d == 0:
                oacc[pl.ds(off, CHUNK), :] = o
            else:
                oacc[pl.ds(off, CHUNK), :] += o
            return s * jnp.exp(g_last) + _dot_tn(kd, u16)

        init = s0_ref[d] if has_s0 else jnp.zeros((HEAD_DIM, HEAD_DIM), F32)
        fin = lax.fori_loop(0, n_chunks, body, init)
        if write_state:
            st_ref[d] = fin

    mix_ref[...] = (_rms(oacc[...], gn_ref[...]) * _silu(z_ref[...])).astype(mix_ref.dtype)


def _delta(proj, n_batch, seq, conv_w, alog_row, dtb_row, gn, s0, write_state):
    def col_spec(col0):
        return pl.BlockSpec((seq, HEAD_DIM), lambda b, h: (b, col0 // HEAD_DIM + h))

    def conv_spec(part):
        return pl.BlockSpec((3, HEAD_DIM), lambda b, h: (0, part * DN_H + h))

    vec = pl.BlockSpec((1, LANES), lambda b, h: (0, 0))
    st_spec = pl.BlockSpec((None, 2, None, HEAD_DIM, HEAD_DIM), lambda b, h: (b, 0, h, 0, 0))
    in_specs = [col_spec(OD_DQ), col_spec(OD_DK), col_spec(OD_DV), col_spec(OD_DZ),
                pl.BlockSpec((seq, LANES), lambda b, h: (b, OD_AB // LANES)),
                conv_spec(0), conv_spec(1), conv_spec(2), vec, vec, vec]
    args = [proj, proj, proj, proj, proj, conv_w, conv_w, conv_w, alog_row, dtb_row, gn]
    if s0 is not None:
        in_specs.append(st_spec)
        args.append(s0)
    mix_spec = pl.BlockSpec((seq, HEAD_DIM), lambda b, h: (b, h))
    mix_shape = jax.ShapeDtypeStruct((n_batch * seq, DN_H * HEAD_DIM), BF16)
    if write_state:
        out_specs = [mix_spec, st_spec]
        out_shape = [mix_shape, jax.ShapeDtypeStruct((n_batch, 2, DN_H, HEAD_DIM, HEAD_DIM), F32)]
    else:
        out_specs, out_shape = mix_spec, mix_shape
    tok = pltpu.VMEM((seq, HEAD_DIM), F32)
    return pl.pallas_call(
        functools.partial(_delta_kernel, seq=seq, has_s0=s0 is not None, write_state=write_state),
        grid=(n_batch, DN_H),
        in_specs=in_specs,
        out_specs=out_specs,
        out_shape=out_shape,
        scratch_shapes=[tok, tok, tok, tok, pltpu.VMEM((2, seq, LANES), F32), pltpu.VMEM((2, seq, LANES), F32)],
        compiler_params=_params("parallel", "arbitrary"),
        name="deltanet",
    )(*args)


def _rope_tables(n_tok):
    t = jnp.arange(n_tok)
    inv = 1.0 / (ROPE_THETA ** (jnp.arange(ROPE_FREQ, dtype=F32) / ROPE_FREQ))
    pos = jnp.stack([t // GRID_W, t % GRID_W], axis=1).astype(F32)
    ang = pos[:, :, None] * inv
    cos = jnp.concatenate([jnp.cos(ang), jnp.cos(ang)], axis=-1).reshape(n_tok, HEAD_DIM)
    sin = jnp.concatenate([-jnp.sin(ang), jnp.sin(ang)], axis=-1).reshape(n_tok, HEAD_DIM)
    return cos, sin


def _pad_lanes(v):
    return jnp.pad(v.reshape(1, -1), ((0, 0), (0, LANES - v.size)))


def kernel(x_prompt, x_sample, state_gla, cache_gqa_k, cache_gqa_v, cache_na_k, cache_na_v, state_delta, c, c_ctx, norm1, norm2, w_ada, b_ada, w_mlp1, w_mlp2, ev_w_in, ev_w_a2, ev_b_a2, ev_gla_norm, ev_q_norm, ev_k_norm, ev_w_out, od_w_in, od_conv, od_a_log, od_dt_bias, od_dn_norm, od_rpb, od_w_out, norm_f):
    n_ctx, seq_ctx, d = x_prompt.shape
    n_lat, seq_lat, _ = x_sample.shape
    depth = w_ada.shape[0]

    w_ev = ev_w_in[0]
    w_ev = jnp.concatenate(
        [w_ev[:, :EV_AQ], w_ev[:, EV_AQ + 2 * GLA_RANK:], w_ev[:, EV_AQ:EV_AQ + 2 * GLA_RANK],
         jnp.zeros((d, EV_COLS_PAD - w_ev.shape[1]), F32)], axis=1).astype(BF16)
    w_od = jnp.pad(od_w_in[0], ((0, 0), (0, OD_COLS_PAD - od_w_in.shape[2]))).astype(BF16)
    w_in = [w_ev, w_od]
    w_out = [ev_w_out[0].astype(BF16), od_w_out[0].astype(BF16)]
    w1 = w_mlp1.astype(BF16)
    w2 = w_mlp2.astype(BF16)
    w2pad = jnp.zeros((2, LANES, GLA_H * GLA_DK), F32)
    for dd in range(2):
        w2pad = w2pad.at[dd, dd * GLA_RANK:(dd + 1) * GLA_RANK].set(ev_w_a2[0, dd])
    b2 = ev_b_a2[0].reshape(2, 1, GLA_H * GLA_DK)
    alog_row = _pad_lanes(od_a_log[0])
    dtb_row = _pad_lanes(od_dt_bias[0])
    na_bias = _na_bias_table(od_rpb[0])
    rope = _rope_tables(seq_lat)

    cond8 = jnp.concatenate([c_ctx[None, :], c, jnp.zeros((8 - 1 - n_lat, d), F32)], axis=0)
    mod = _adaln(cond8, w_ada, b_ada).reshape(depth, 8, 6, 1, d)

    def trunk(x, n_batch, seq, row_fn, caches):
        latent = caches is not None
        tm = 512
        outs = {}
        proj = _inproj(x, norm1[0:1], mod, 0, row_fn, w_in[0], tm)
        if latent:
            s0 = caches["gla"].reshape(n_batch, 2, GLA_H // 2, LANES, GLA_DV)
            mix_a = _gla(proj, n_batch, seq, w2pad, b2, ev_gla_norm, s0, False)
            mix_b = _attention(proj, n_batch, seq, EV_AQ, EV_AK, EV_AV, GQA_KV, GQA_H // GQA_KV,
                               norms=(ev_q_norm, ev_k_norm), rope=rope, ctx=caches["gqa"])
        else:
            mix_a, st = _gla(proj, n_batch, seq, w2pad, b2, ev_gla_norm, None, True)
            mix_b, k_new = _attention(proj, n_batch, seq, EV_AQ, EV_AK, EV_AV, GQA_KV, GQA_H // GQA_KV,
                                      norms=(ev_q_norm, ev_k_norm), write_k=True)
            outs["st_gla"] = st.reshape(n_batch, 1, 2, GLA_H, GLA_DK, GLA_DV)
            outs["ck_gqa"] = k_new.reshape(n_batch, 1, seq, GQA_KV, HEAD_DIM)
            outs["cv_gqa"] = proj[:, EV_AV:EV_AV + GQA_KV * HEAD_DIM].reshape(n_batch, 1, seq, GQA_KV, HEAD_DIM)
        x = _outproj(mix_a, mix_b, w_out[0], x, mod, 0, row_fn, tm)
        x = _mlp(x, norm2[0:1], mod, 0, row_fn, w1[0], w2[0], norm_f[None, :], False, tm)
        proj = _inproj(x, norm1[1:2], mod, 1, row_fn, w_in[1], tm)
        if latent:
            mix_a = _neighbourhood(proj, n_batch, seq, caches["na"][0], caches["na"][1], na_bias)
            s0 = caches["delta"].reshape(n_batch, 2, DN_H, HEAD_DIM, HEAD_DIM)
            mix_b = _delta(proj, n_batch, seq, od_conv[0], alog_row, dtb_row, od_dn_norm, s0, False)
        else:
            mix_a = _attention(proj, n_batch, seq, OD_NQ, OD_NK, OD_NV, NA_H, 1)
            mix_b, st = _delta(proj, n_batch, seq, od_conv[0], alog_row, dtb_row, od_dn_norm, None, True)
            outs["ck_na"] = proj[:, OD_NK:OD_NK + NA_H * HEAD_DIM].reshape(n_batch, 1, seq, NA_H, HEAD_DIM)
            outs["cv_na"] = proj[:, OD_NV:OD_NV + NA_H * HEAD_DIM].reshape(n_batch, 1, seq, NA_H, HEAD_DIM)
            outs["st_dn"] = st.reshape(n_batch, 1, 2, DN_H, HEAD_DIM, HEAD_DIM)
        x = _outproj(mix_a, mix_b, w_out[1], x, mod, 1, row_fn, tm)
        y = _mlp(x, norm2[1:2], mod, 1, row_fn, w1[1], w2[1], norm_f[None, :], True, tm)
        return y.reshape(n_batch, seq, d), outs

    y_prompt, new = trunk(x_prompt.reshape(n_ctx * seq_ctx, d), n_ctx, seq_ctx, lambda i: 0, None)
    past = cache_gqa_k.shape[2]
    caches = {
        "gla": state_gla[:, 0],
        "gqa": (cache_gqa_k[:, 0].reshape(n_lat, past, GQA_KV * HEAD_DIM),
                cache_gqa_v[:, 0].reshape(n_lat, past, GQA_KV * HEAD_DIM)),
        "na": (cache_na_k[:, 0].reshape(n_lat, past, NA_H * HEAD_DIM),
               cache_na_v[:, 0].reshape(n_lat, past, NA_H * HEAD_DIM)),
        "delta": state_delta[:, 0],
    }
    y_sample, _ = trunk(x_sample.reshape(n_lat * seq_lat, d), n_lat, seq_lat,
                        lambda i: 1 + (i * 512) // seq_lat, caches)
    return (y_prompt, y_sample, new["st_gla"], new["ck_gqa"], new["cv_gqa"], new["ck_na"], new["cv_na"],
            new["st_dn"])
```

```python
import functools

import jax
import jax.numpy as jnp
import numpy as np
from jax import lax
from jax.experimental import pallas as pl
from jax.experimental.pallas import tpu as pltpu

F32 = jnp.float32
BF16 = jnp.bfloat16
HIGHEST = lax.Precision.HIGHEST

D_MODEL = 2048
D_FF = 4 * D_MODEL
HEAD_DIM = 128
LANES = 128
GRID_W = 64
GLA_H = 8
GLA_DK = 64
GLA_DV = 128
GLA_RANK = 16
GLA_TAU = 16.0
GQA_H = 8
GQA_KV = 2
NA_H = 8
NA_WIN_R = 8
NA_WIN_C = 16
DN_H = 8
CHUNK = 64
DN_CHUNK = 128
DN_REFINE_STEPS = 2
NA_ROWS_PER_STEP = 4
ROPE_THETA = 10000.0
ROPE_FREQ = HEAD_DIM // 4
EPS = 1e-6
MASK_VALUE = -1e30

EV_Q, EV_K, EV_V, EV_G = 0, 512, 1024, 2048
EV_AQ, EV_AK, EV_AV, EV_LO = 3072, 4096, 4352, 4608
EV_COLS_PAD = 5120
OD_NQ, OD_NK, OD_NV = 0, 1024, 2048
OD_DQ, OD_DK, OD_DV, OD_DZ, OD_AB = 3072, 4096, 5120, 6144, 7168
OD_COLS_PAD = 7680

VMEM_LIMIT = 56 * 1024 * 1024


def _params(*sem):
    return pltpu.CompilerParams(dimension_semantics=sem, vmem_limit_bytes=VMEM_LIMIT)


def _dot(a, b):
    return jnp.dot(a, b, preferred_element_type=F32)


def _dot_nt(a, b):
    return lax.dot_general(a, b, (((1,), (1,)), ((), ())), preferred_element_type=F32)


def _dot_tn(a, b):
    return lax.dot_general(a, b, (((0,), (0,)), ((), ())), preferred_element_type=F32)


def _dot_hi(a, b):
    return jnp.dot(a, b, precision=HIGHEST, preferred_element_type=F32)


def _sigmoid(x):
    return 1.0 / (1.0 + jnp.exp(-x))


def _silu(x):
    return x * _sigmoid(x)


def _softplus(x):
    return jnp.maximum(x, 0.0) + jnp.log(1.0 + jnp.exp(-jnp.abs(x)))


def _rms(x, g):
    return x * lax.rsqrt(jnp.mean(x * x, axis=-1, keepdims=True) + EPS) * g


def _rms_mod(x, g, sc, sh):
    return _rms(x, g) * (1.0 + sc) + sh


def _adaln_kernel(c_ref, w_ref, b_ref, o_ref):
    a = _silu(c_ref[...]).astype(BF16)
    o_ref[...] = _dot(a, w_ref[...].astype(BF16)) + b_ref[...]


def _adaln(cond8, w_ada, b_ada):
    depth, d, n = w_ada.shape
    tn = 512
    return pl.pallas_call(
        _adaln_kernel,
        grid=(depth, n // tn),
        in_specs=[
            pl.BlockSpec((8, d), lambda l, j: (0, 0)),
            pl.BlockSpec((None, d, tn), lambda l, j: (l, 0, j)),
            pl.BlockSpec((None, 1, tn), lambda l, j: (l, 0, j)),
        ],
        out_specs=pl.BlockSpec((None, 8, tn), lambda l, j: (l, 0, j)),
        out_shape=jax.ShapeDtypeStruct((depth, 8, n), F32),
        compiler_params=_params("parallel", "parallel"),
        name="adaln",
    )(cond8, w_ada, b_ada.reshape(depth, 1, n))


def _mod_spec(layer, k, row_fn):
    return pl.BlockSpec((None, None, None, 1, D_MODEL), lambda i, j: (layer, row_fn(i), k, 0, 0))


def _inproj_kernel(x_ref, g_ref, sc_ref, sh_ref, w_ref, o_ref, h_ref):
    @pl.when(pl.program_id(1) == 0)
    def _():
        h_ref[...] = _rms_mod(x_ref[...], g_ref[...], sc_ref[...], sh_ref[...]).astype(BF16)

    o_ref[...] = _dot(h_ref[...], w_ref[...])


def _inproj(x, g, mod, layer, row_fn, w, tm):
    m, d = x.shape
    n = w.shape[1]
    tn = 512
    return pl.pallas_call(
        _inproj_kernel,
        grid=(m // tm, n // tn),
        in_specs=[
            pl.BlockSpec((tm, d), lambda i, j: (i, 0)),
            pl.BlockSpec((1, d), lambda i, j: (0, 0)),
            _mod_spec(layer, 1, row_fn),
            _mod_spec(layer, 0, row_fn),
            pl.BlockSpec((d, tn), lambda i, j: (0, j)),
        ],
        out_specs=pl.BlockSpec((tm, tn), lambda i, j: (i, j)),
        out_shape=jax.ShapeDtypeStruct((m, n), F32),
        scratch_shapes=[pltpu.VMEM((tm, d), BF16)],
        compiler_params=_params("parallel", "arbitrary"),
        name="inproj",
    )(x, g, mod, mod, w)


def _outproj_kernel(a_ref, b_ref, wa_ref, wb_ref, x_ref, gt_ref, o_ref):
    acc = _dot(a_ref[...], wa_ref[...]) + _dot(b_ref[...], wb_ref[...])
    o_ref[...] = x_ref[...] + gt_ref[...] * acc


def _outproj(mix_a, mix_b, w, x, mod, layer, row_fn, tm):
    m, d = x.shape
    ka, kb = mix_a.shape[1], mix_b.shape[1]
    tn = 1024
    gt_spec = pl.BlockSpec((None, None, None, 1, tn), lambda i, j: (layer, row_fn(i), 2, 0, j))
    return pl.pallas_call(
        _outproj_kernel,
        grid=(m // tm, d // tn),
        in_specs=[
            pl.BlockSpec((tm, ka), lambda i, j: (i, 0)),
            pl.BlockSpec((tm, kb), lambda i, j: (i, 0)),
            pl.BlockSpec((ka, tn), lambda i, j: (0, j)),
            pl.BlockSpec((kb, tn), lambda i, j: (1, j)),
            pl.BlockSpec((tm, tn), lambda i, j: (i, j)),
            gt_spec,
        ],
        out_specs=pl.BlockSpec((tm, tn), lambda i, j: (i, j)),
        out_shape=jax.ShapeDtypeStruct((m, d), F32),
        compiler_params=_params("parallel", "parallel"),
        name="outproj",
    )(mix_a, mix_b, w, w, x, mod)


def _mlp_kernel(x_ref, g_ref, sc_ref, sh_ref, gt_ref, w1_ref, w2_ref, gf_ref, o_ref, h_ref, acc_ref, *, final_norm):
    j = pl.program_id(1)

    @pl.when(j == 0)
    def _():
        h_ref[...] = _rms_mod(x_ref[...], g_ref[...], sc_ref[...], sh_ref[...]).astype(BF16)
        acc_ref[...] = jnp.zeros_like(acc_ref)

    hid = jnp.maximum(_dot(h_ref[...], w1_ref[...]), 0.0)
    acc_ref[...] += _dot((hid * hid).astype(BF16), w2_ref[...])

    @pl.when(j == pl.num_programs(1) - 1)
    def _():
        y = x_ref[...] + gt_ref[...] * acc_ref[...]
        if final_norm:
            y = _rms(y, gf_ref[...])
        o_ref[...] = y


def _mlp(x, g, mod, layer, row_fn, w1, w2, gf, final_norm, tm):
    m, d = x.shape
    ff = w1.shape[1]
    tf = 512
    return pl.pallas_call(
        functools.partial(_mlp_kernel, final_norm=final_norm),
        grid=(m // tm, ff // tf),
        in_specs=[
            pl.BlockSpec((tm, d), lambda i, j: (i, 0)),
            pl.BlockSpec((1, d), lambda i, j: (0, 0)),
            _mod_spec(layer, 4, row_fn),
            _mod_spec(layer, 3, row_fn),
            _mod_spec(layer, 5, row_fn),
            pl.BlockSpec((d, tf), lambda i, j: (0, j)),
            pl.BlockSpec((tf, d), lambda i, j: (j, 0)),
            pl.BlockSpec((1, d), lambda i, j: (0, 0)),
        ],
        out_specs=pl.BlockSpec((tm, d), lambda i, j: (i, 0)),
        out_shape=jax.ShapeDtypeStruct((m, d), F32),
        scratch_shapes=[pltpu.VMEM((tm, d), BF16), pltpu.VMEM((tm, d), F32)],
        compiler_params=_params("parallel", "arbitrary"),
        name="mlp",
    )(x, g, mod, mod, mod, w1, w2, gf)


def _rope(x, cos, sin):
    lane = lax.broadcasted_iota(jnp.int32, x.shape, 1)
    first_half = (lane % (2 * ROPE_FREQ)) < ROPE_FREQ
    partner = jnp.where(first_half, pltpu.roll(x, LANES - ROPE_FREQ, 1), pltpu.roll(x, ROPE_FREQ, 1))
    return x * cos + partner * sin


def _attn_kernel(*refs, group, use_norm, use_rope, has_ctx, write_k):
    it = iter(refs)
    q_ref, k_ref, v_ref = next(it), next(it), next(it)
    qn_ref = kn_ref = cq_ref = sq_ref = ck_ref = sk_ref = kc_ref = vc_ref = knew_ref = None
    if use_norm:
        qn_ref, kn_ref = next(it), next(it)
    if use_rope:
        cq_ref, sq_ref, ck_ref, sk_ref = next(it), next(it), next(it), next(it)
    if has_ctx:
        kc_ref, vc_ref = next(it), next(it)
    o_ref = next(it)
    if write_k:
        knew_ref = next(it)
    kbuf, vbuf = next(it), next(it)
    scale = HEAD_DIM ** -0.5

    @pl.when(pl.program_id(2) == 0)
    def _():
        k = k_ref[...]
        if use_norm:
            k = _rms(k, kn_ref[...])
        if write_k:
            knew_ref[...] = k
        if use_rope:
            k = _rope(k, ck_ref[...], sk_ref[...])
        kbuf[...] = k.astype(BF16)
        vbuf[...] = v_ref[...].astype(BF16)

    heads = range(group)
    cols = [slice(g * HEAD_DIM, (g + 1) * HEAD_DIM) for g in heads]
    qb = []
    for g in heads:
        q = q_ref[:, cols[g]]
        if use_norm:
            q = _rms(q, qn_ref[...])
        if use_rope:
            q = _rope(q, cq_ref[...], sq_ref[...])
        qb.append(q.astype(BF16))
    s = [_dot_nt(qb[g], kbuf[...]) * scale for g in heads]
    m = [jnp.max(s[g], axis=-1, keepdims=True) for g in heads]
    if has_ctx:
        kc16 = kc_ref[...].astype(BF16)
        s_c = [_dot_nt(qb[g], kc16) * scale for g in heads]
        m = [jnp.maximum(m[g], jnp.max(s_c[g], axis=-1, keepdims=True)) for g in heads]
    p = [jnp.exp(s[g] - m[g]) for g in heads]
    l = [jnp.sum(p[g], axis=-1, keepdims=True) for g in heads]
    o = [_dot(p[g].astype(BF16), vbuf[...]) for g in heads]
    if has_ctx:
        vc16 = vc_ref[...].astype(BF16)
        p_c = [jnp.exp(s_c[g] - m[g]) for g in heads]
        l = [l[g] + jnp.sum(p_c[g], axis=-1, keepdims=True) for g in heads]
        o = [o[g] + _dot(p_c[g].astype(BF16), vc16) for g in heads]
    for g in heads:
        o_ref[:, cols[g]] = (o[g] / l[g]).astype(o_ref.dtype)


def _attention(proj, n_batch, seq, q_col, k_col, v_col, n_kv, group, norms=None, rope=None, ctx=None,
               write_k=False):
    tq = min(seq, 256)
    nq = seq // tq
    gw = group * HEAD_DIM
    in_specs = [
        pl.BlockSpec((tq, gw), lambda b, h, i: (b * nq + i, q_col // gw + h)),
        pl.BlockSpec((seq, HEAD_DIM), lambda b, h, i: (b, k_col // HEAD_DIM + h)),
        pl.BlockSpec((seq, HEAD_DIM), lambda b, h, i: (b, v_col // HEAD_DIM + h)),
    ]
    args = [proj, proj, proj]
    vec = pl.BlockSpec((1, HEAD_DIM), lambda b, h, i: (0, 0))
    if norms is not None:
        in_specs += [vec, vec]
        args += [norms[0], norms[1]]
    if rope is not None:
        tab_q = pl.BlockSpec((tq, HEAD_DIM), lambda b, h, i: (i, 0))
        tab_k = pl.BlockSpec((seq, HEAD_DIM), lambda b, h, i: (0, 0))
        in_specs += [tab_q, tab_q, tab_k, tab_k]
        args += [rope[0], rope[1], rope[0], rope[1]]
    if ctx is not None:
        past = ctx[0].shape[1]
        c_spec = pl.BlockSpec((None, past, HEAD_DIM), lambda b, h, i: (b, 0, h))
        in_specs += [c_spec, c_spec]
        args += [ctx[0], ctx[1]]
    o_spec = pl.BlockSpec((tq, gw), lambda b, h, i: (b * nq + i, h))
    o_shape = jax.ShapeDtypeStruct((n_batch * seq, n_kv * gw), BF16)
    if write_k:
        out_specs = [o_spec, pl.BlockSpec((seq, HEAD_DIM), lambda b, h, i: (b, h))]
        out_shape = [o_shape, jax.ShapeDtypeStruct((n_batch * seq, n_kv * HEAD_DIM), F32)]
    else:
        out_specs, out_shape = o_spec, o_shape
    return pl.pallas_call(
        functools.partial(_attn_kernel, group=group, use_norm=norms is not None, use_rope=rope is not None,
                          has_ctx=ctx is not None, write_k=write_k),
        grid=(n_batch, n_kv, nq),
        in_specs=in_specs,
        out_specs=out_specs,
        out_shape=out_shape,
        scratch_shapes=[pltpu.VMEM((seq, HEAD_DIM), BF16), pltpu.VMEM((seq, HEAD_DIM), BF16)],
        compiler_params=_params("parallel", "parallel", "arbitrary"),
        name="attention",
    )(*args)


def _na_row_start(r, rows):
    return jnp.clip(r - NA_WIN_R // 2, 0, rows - NA_WIN_R)


def _na_kernel(q_ref, k_ref, v_ref, kc_ref, vc_ref, bias_ref, o_ref, kb, vb, kcb, vcb, *, rows):
    win = NA_WIN_R * GRID_W
    scale = HEAD_DIM ** -0.5
    kb[...] = k_ref[...].astype(BF16)
    vb[...] = v_ref[...].astype(BF16)
    kcb[...] = kc_ref[...].astype(BF16)
    vcb[...] = vc_ref[...].astype(BF16)

    def body(j, carry):
        rr = [j * NA_ROWS_PER_STEP + i for i in range(NA_ROWS_PER_STEP)]
        rs = [_na_row_start(r, rows) for r in rr]
        k_rows = [pl.ds(pl.multiple_of(s * GRID_W, GRID_W), win) for s in rs]
        q_rows = [pl.ds(pl.multiple_of(r * GRID_W, GRID_W), GRID_W) for r in rr]
        qb = [q_ref[q_rows[i], :].astype(BF16) for i in range(NA_ROWS_PER_STEP)]
        s_w = [_dot_nt(qb[i], kb[k_rows[i], :]) * scale for i in range(NA_ROWS_PER_STEP)]
        s_c = [_dot_nt(qb[i], kcb[...]) * scale for i in range(NA_ROWS_PER_STEP)]
        p_w, p_c, l = [], [], []
        for i in range(NA_ROWS_PER_STEP):
            delta = rr[i] - rs[i]
            bias = jnp.concatenate(
                [bias_ref[2 * jj - delta + NA_WIN_R - 1] for jj in range(NA_WIN_R // 2)], axis=-1)
            sw = s_w[i] + bias
            m = jnp.maximum(jnp.max(sw, axis=-1, keepdims=True), jnp.max(s_c[i], axis=-1, keepdims=True))
            pw = jnp.exp(sw - m)
            pc = jnp.exp(s_c[i] - m)
            l.append(jnp.sum(pw, axis=-1, keepdims=True) + jnp.sum(pc, axis=-1, keepdims=True))
            p_w.append(pw.astype(BF16))
            p_c.append(pc.astype(BF16))
        o_w = [_dot(p_w[i], vb[k_rows[i], :]) for i in range(NA_ROWS_PER_STEP)]
        o_c = [_dot(p_c[i], vcb[...]) for i in range(NA_ROWS_PER_STEP)]
        for i in range(NA_ROWS_PER_STEP):
            o_ref[q_rows[i], :] = ((o_w[i] + o_c[i]) / l[i]).astype(o_ref.dtype)
        return carry

    lax.fori_loop(0, rows // NA_ROWS_PER_STEP, body, 0)


def _na_bias_table(rpb):
    cols = jnp.arange(GRID_W)
    start = jnp.clip(cols - NA_WIN_C // 2, 0, GRID_W - NA_WIN_C)
    inside = (cols[None, :] >= start[:, None]) & (cols[None, :] < start[:, None] + NA_WIN_C)
    rel = jnp.clip(cols[None, :] - cols[:, None] + NA_WIN_C - 1, 0, 2 * NA_WIN_C - 2)
    tab = jnp.where(inside[None, None], rpb[:, :, rel], MASK_VALUE)
    return jnp.concatenate([tab[:, :-1], tab[:, 1:]], axis=-1)


def _neighbourhood(proj, n_batch, seq, k_ctx, v_ctx, bias):
    rows = seq // GRID_W
    past = k_ctx.shape[1]
    c_spec = pl.BlockSpec((None, past, HEAD_DIM), lambda b, h: (b, 0, h))

    def col_spec(col0):
        return pl.BlockSpec((seq, HEAD_DIM), lambda b, h: (b, col0 // HEAD_DIM + h))

    return pl.pallas_call(
        functools.partial(_na_kernel, rows=rows),
        grid=(n_batch, NA_H),
        in_specs=[
            col_spec(OD_NQ), col_spec(OD_NK), col_spec(OD_NV), c_spec, c_spec,
            pl.BlockSpec((None, 2 * NA_WIN_R - 2, GRID_W, 2 * GRID_W), lambda b, h: (h, 0, 0, 0)),
        ],
        out_specs=pl.BlockSpec((seq, HEAD_DIM), lambda b, h: (b, h)),
        out_shape=jax.ShapeDtypeStruct((n_batch * seq, NA_H * HEAD_DIM), BF16),
        scratch_shapes=[pltpu.VMEM((seq, HEAD_DIM), BF16), pltpu.VMEM((seq, HEAD_DIM), BF16),
                        pltpu.VMEM((past, HEAD_DIM), BF16), pltpu.VMEM((past, HEAD_DIM), BF16)],
        compiler_params=_params("parallel", "parallel"),
        name="neighbourhood",
    )(proj, proj, proj, k_ctx, v_ctx, bias)


def _order_masks(n=CHUNK):
    row = lax.broadcasted_iota(jnp.int32, (n, n), 0)
    col = lax.broadcasted_iota(jnp.int32, (n, n), 1)
    return row, col


def _chunk_cumsum(x, chunk, reverse):
    seq = x.shape[0]
    pos = lax.broadcasted_iota(jnp.int32, x.shape, 0) % chunk
    shift = 1
    while shift < chunk:
        if reverse:
            x = x + jnp.where(pos < chunk - shift, pltpu.roll(x, seq - shift, 0), 0.0)
        else:
            x = x + jnp.where(pos >= shift, pltpu.roll(x, shift, 0), 0.0)
        shift *= 2
    return x


def _gla_kernel(*refs, seq, has_s0, write_state):
    it = iter(refs)
    q_ref, k_ref, v_ref, gg_ref, lo_ref, w2_ref, b2_ref, gn_ref = (next(it) for _ in range(8))
    s0_ref = next(it) if has_s0 else None
    mix_ref = next(it)
    st_ref = next(it) if write_state else None
    o_s, bc_s = next(it), next(it)
    n_chunks = seq // CHUNK
    lane = lax.broadcasted_iota(jnp.int32, (1, LANES), 1)
    head_mask = [(lane < GLA_DK).astype(F32), (lane >= GLA_DK).astype(F32)]
    row, col = _order_masks()
    before = [col <= row, col >= row]

    lo = lo_ref[...].astype(BF16)
    for d in range(2):
        z = _dot(lo, w2_ref[d].astype(BF16)) + b2_ref[d]
        bc_s[d] = _chunk_cumsum(-_softplus(-z) * (1.0 / GLA_TAU), CHUNK, d == 1)

    def body(i, carry):
        rows = [pl.ds(pl.multiple_of(n * CHUNK, CHUNK), CHUNK) for n in (i, n_chunks - 1 - i)]
        chains = [(d, p) for d in range(2) for p in range(2)]
        qe, ke, kd, dec = [], [], [], []
        for d in range(2):
            bc = bc_s[d, rows[d], :]
            bl = bc[CHUNK - 1:CHUNK, :] if d == 0 else bc[0:1, :]
            k = k_ref[rows[d], :]
            qe.append(q_ref[rows[d], :] * (GLA_DK ** -0.5) * jnp.exp(bc))
            ke.append((k * jnp.exp(-bc)).astype(BF16))
            kd.append(k * jnp.exp(bl - bc))
            dec.append(jnp.exp(bl))
        cols = [slice(p * GLA_DV, (p + 1) * GLA_DV) for p in range(2)]
        qep = [(qe[d] * head_mask[p]).astype(BF16) for d, p in chains]
        kdp = [(kd[d] * head_mask[p]).astype(BF16) for d, p in chains]
        vp = [v_ref[rows[d], cols[p]].astype(BF16) for d, p in chains]
        sc = [jnp.where(before[d], _dot_nt(qep[c], ke[d]), 0.0).astype(BF16) for c, (d, p) in enumerate(chains)]
        inter = [_dot_nt(qep[c], carry[c].astype(BF16)) for c in range(4)]
        kv = [_dot_tn(vp[c], kdp[c]) for c in range(4)]
        for c, (d, p) in enumerate(chains):
            o_s[d, rows[d], cols[p]] = _dot(sc[c], vp[c]) + inter[c]
        return tuple(carry[c] * dec[d] + kv[c] for c, (d, p) in enumerate(chains))

    init = []
    for d in range(2):
        if has_s0:
            s0t = s0_ref[d].T
            init += [s0t * head_mask[0], s0t * head_mask[1]]
        else:
            init += [jnp.zeros((GLA_DV, LANES), F32), jnp.zeros((GLA_DV, LANES), F32)]
    fin = lax.fori_loop(0, n_chunks, body, tuple(init), unroll=2)
    if write_state:
        for d in range(2):
            st_ref[d] = (fin[2 * d] + fin[2 * d + 1]).T

    for p in range(2):
        cols = slice(p * GLA_DV, (p + 1) * GLA_DV)
        y = _rms(o_s[0, :, cols] + o_s[1, :, cols], gn_ref[...]) * _silu(gg_ref[:, cols])
        mix_ref[:, cols] = y.astype(mix_ref.dtype)


def _gla(proj, n_batch, seq, w2pad, b2, gn, s0, write_state):
    pairs = GLA_H // 2
    pw = 2 * GLA_DV
    st_spec = pl.BlockSpec((None, 2, None, LANES, GLA_DV), lambda b, p: (b, 0, p, 0, 0))
    in_specs = [
        pl.BlockSpec((seq, LANES), lambda b, p: (b, EV_Q // LANES + p)),
        pl.BlockSpec((seq, LANES), lambda b, p: (b, EV_K // LANES + p)),
        pl.BlockSpec((seq, pw), lambda b, p: (b, EV_V // pw + p)),
        pl.BlockSpec((seq, pw), lambda b, p: (b, EV_G // pw + p)),
        pl.BlockSpec((seq, LANES), lambda b, p: (b, EV_LO // LANES)),
        pl.BlockSpec((2, LANES, LANES), lambda b, p: (0, 0, p)),
        pl.BlockSpec((2, 1, LANES), lambda b, p: (0, 0, p)),
        pl.BlockSpec((1, GLA_DV), lambda b, p: (0, 0)),
    ]
    args = [proj, proj, proj, proj, proj, w2pad, b2, gn]
    if s0 is not None:
        in_specs.append(st_spec)
        args.append(s0)
    mix_spec = pl.BlockSpec((seq, pw), lambda b, p: (b, p))
    mix_shape = jax.ShapeDtypeStruct((n_batch * seq, GLA_H * GLA_DV), BF16)
    if write_state:
        out_specs = [mix_spec, st_spec]
        out_shape = [mix_shape, jax.ShapeDtypeStruct((n_batch, 2, pairs, LANES, GLA_DV), F32)]
    else:
        out_specs, out_shape = mix_spec, mix_shape
    return pl.pallas_call(
        functools.partial(_gla_kernel, seq=seq, has_s0=s0 is not None, write_state=write_state),
        grid=(n_batch, pairs),
        in_specs=in_specs,
        out_specs=out_specs,
        out_shape=out_shape,
        scratch_shapes=[pltpu.VMEM((2, seq, pw), F32), pltpu.VMEM((2, seq, LANES), F32)],
        compiler_params=_params("parallel", "parallel"),
        name="gla",
    )(*args)


def _conv_silu(x, w):
    seq = x.shape[0]
    t = lax.broadcasted_iota(jnp.int32, x.shape, 0)
    prev = jnp.where(t == 0, 0.0, pltpu.roll(x, 1, 0))
    nxt = jnp.where(t == seq - 1, 0.0, pltpu.roll(x, seq - 1, 0))
    return _silu(prev * w[0:1, :] + x * w[1:2, :] + nxt * w[2:3, :])


def _l2norm(x):
    return x * lax.rsqrt(jnp.sum(x * x, axis=-1, keepdims=True) + EPS)


def _delta_kernel(*refs, seq, has_s0, write_state):
    it = iter(refs)
    (q_ref, k_ref, v_ref, z_ref, ab_ref, wq_ref, wk_ref, wv_ref, alog_ref, dtb_ref,
     gn_ref) = (next(it) for _ in range(11))
    s0_ref = next(it) if has_s0 else None
    mix_ref = next(it)
    st_ref = next(it) if write_state else None
    qs, ks, vs, o_s, gc_s, b_s, u0_s, kc_s, at_s, qe_s, kd_s = (next(it) for _ in range(11))
    n_chunks = seq // DN_CHUNK
    h = pl.program_id(1)
    lane = lax.broadcasted_iota(jnp.int32, (1, LANES), 1)
    row, col = _order_masks(DN_CHUNK)
    eye = (row == col).astype(F32)
    before = [col <= row, col >= row]
    strict = [col < row, col > row]

    qs[...] = _l2norm(_conv_silu(q_ref[...], wq_ref[...])) * (HEAD_DIM ** -0.5)
    ks[...] = _l2norm(_conv_silu(k_ref[...], wk_ref[...]))
    vs[...] = _conv_silu(v_ref[...], wv_ref[...])
    ab = ab_ref[...]
    g_all = -jnp.exp(alog_ref[...]) * _softplus(ab + dtb_ref[...])
    beta_all = _sigmoid(ab)
    for d in range(2):
        sel_g = lane == d * DN_H + h
        sel_b = lane == 2 * DN_H + d * DN_H + h
        g_col = jnp.broadcast_to(jnp.sum(jnp.where(sel_g, g_all, 0.0), axis=-1, keepdims=True), (seq, LANES))
        gc_s[d] = _chunk_cumsum(g_col, DN_CHUNK, d == 1)
        b_s[d] = jnp.broadcast_to(jnp.sum(jnp.where(sel_b, beta_all, 0.0), axis=-1, keepdims=True), (seq, LANES))

    def chunk_rows(n):
        return pl.ds(pl.multiple_of(n * DN_CHUNK, DN_CHUNK), DN_CHUNK)

    def last_row(d, n):
        return pl.ds(n * DN_CHUNK + (DN_CHUNK - 1 if d == 0 else 0), 1)

    def wy_chunks(units):
        nu = range(len(units))
        rows = [chunk_rows(n) for _, n in units]
        gc = [gc_s[d, rows[i], :] for i, (d, _) in enumerate(units)]
        bb = [b_s[d, rows[i], :] for i, (d, _) in enumerate(units)]
        k = [ks[rows[i], :] for i in nu]
        k16 = [k[i].astype(BF16) for i in nu]
        kbeta = [k[i] * bb[i] for i in nu]
        kk = [_dot_nt(kbeta[i].astype(BF16), k16[i]) for i in nu]
        qk = [_dot_nt(qs[rows[i], :].astype(BF16), k16[i]) for i in nu]
        mm = []
        for i, (d, _) in enumerate(units):
            decay = jnp.where(before[d], jnp.exp(jnp.where(before[d], gc[i] - gc[i].T, 0.0)), 0.0)
            mm.append(jnp.where(strict[d], kk[i] * decay, 0.0))
            at_s[d, rows[i], :] = (qk[i] * decay).astype(BF16)
        p16 = [(-mm[i]).astype(BF16) for i in nu]
        inv = [eye - mm[i] for i in nu]
        for _ in range(DN_CHUNK.bit_length() - 2):
            p16 = [_dot(p16[i], p16[i]).astype(BF16) for i in nu]
            inv = [inv[i] + _dot(p16[i], inv[i].astype(BF16)) for i in nu]
        inv16 = [inv[i].astype(BF16) for i in nu]
        egc = [jnp.exp(gc[i]) for i in nu]
        rhs = [jnp.concatenate([vs[rows[i], :] * bb[i], kbeta[i] * egc[i]], axis=-1) for i in nu]
        sol = [_dot(inv16[i], rhs[i].astype(BF16)) for i in nu]
        for _ in range(DN_REFINE_STEPS):
            res = [rhs[i] - sol[i] - _dot_hi(mm[i], sol[i]) for i in nu]
            sol = [sol[i] + _dot(inv16[i], res[i].astype(BF16)) for i in nu]
        for i, (d, _) in enumerate(units):
            u0_s[d, rows[i], :] = sol[i][:, :HEAD_DIM]
            kc_s[d, rows[i], :] = sol[i][:, HEAD_DIM:].astype(BF16)
            g_last = gc[i][DN_CHUNK - 1:DN_CHUNK, :] if d == 0 else gc[i][0:1, :]
            qe_s[d, rows[i], :] = (qs[rows[i], :] * egc[i]).astype(BF16)
            kd_s[d, rows[i], :] = (k[i] * jnp.exp(g_last - gc[i])).astype(BF16)

    def wy_body(j, carry):
        wy_chunks([(0, 2 * j), (1, 2 * j), (0, 2 * j + 1), (1, 2 * j + 1)])
        return carry

    lax.fori_loop(0, n_chunks // 2, wy_body, 0)

    def scan_body(i, carry):
        ns = [i, n_chunks - 1 - i]
        rows = [chunk_rows(n) for n in ns]
        s16 = [carry[d].astype(BF16) for d in range(2)]
        ks_ = [_dot(kc_s[d, rows[d], :], s16[d]) for d in range(2)]
        qs_ = [_dot(qe_s[d, rows[d], :], s16[d]) for d in range(2)]
        u16 = [(u0_s[d, rows[d], :] - ks_[d]).astype(BF16) for d in range(2)]
        au = [_dot(at_s[d, rows[d], :], u16[d]) for d in range(2)]
        ku = [_dot_tn(kd_s[d, rows[d], :], u16[d]) for d in range(2)]
        for d in range(2):
            o_s[d, rows[d], :] = qs_[d] + au[d]
        return tuple(carry[d] * jnp.exp(gc_s[d, last_row(d, ns[d]), :]) + ku[d] for d in range(2))

    if has_s0:
        init = (s0_ref[0], s0_ref[1])
    else:
        init = (jnp.zeros((HEAD_DIM, HEAD_DIM), F32), jnp.zeros((HEAD_DIM, HEAD_DIM), F32))
    fin = lax.fori_loop(0, n_chunks, scan_body, init)
    if write_state:
        st_ref[0] = fin[0]
        st_ref[1] = fin[1]

    mix_ref[...] = (_rms(o_s[0] + o_s[1], gn_ref[...]) * _silu(z_ref[...])).astype(mix_ref.dtype)


def _delta(proj, n_batch, seq, conv_w, alog_row, dtb_row, gn, s0, write_state):
    def col_spec(col0):
        return pl.BlockSpec((seq, HEAD_DIM), lambda b, h: (b, col0 // HEAD_DIM + h))

    def conv_spec(part):
        return pl.BlockSpec((3, HEAD_DIM), lambda b, h: (0, part * DN_H + h))

    vec = pl.BlockSpec((1, LANES), lambda b, h: (0, 0))
    st_spec = pl.BlockSpec((None, 2, None, HEAD_DIM, HEAD_DIM), lambda b, h: (b, 0, h, 0, 0))
    in_specs = [col_spec(OD_DQ), col_spec(OD_DK), col_spec(OD_DV), col_spec(OD_DZ),
                pl.BlockSpec((seq, LANES), lambda b, h: (b, OD_AB // LANES)),
                conv_spec(0), conv_spec(1), conv_spec(2), vec, vec, vec]
    args = [proj, proj, proj, proj, proj, conv_w, conv_w, conv_w, alog_row, dtb_row, gn]
    if s0 is not None:
        in_specs.append(st_spec)
        args.append(s0)
    mix_spec = pl.BlockSpec((seq, HEAD_DIM), lambda b, h: (b, h))
    mix_shape = jax.ShapeDtypeStruct((n_batch * seq, DN_H * HEAD_DIM), BF16)
    if write_state:
        out_specs = [mix_spec, st_spec]
        out_shape = [mix_shape, jax.ShapeDtypeStruct((n_batch, 2, DN_H, HEAD_DIM, HEAD_DIM), F32)]
    else:
        out_specs, out_shape = mix_spec, mix_shape
    tok = pltpu.VMEM((seq, HEAD_DIM), F32)
    both = pltpu.VMEM((2, seq, HEAD_DIM), F32)
    both16 = pltpu.VMEM((2, seq, HEAD_DIM), BF16)
    return pl.pallas_call(
        functools.partial(_delta_kernel, seq=seq, has_s0=s0 is not None, write_state=write_state),
        grid=(n_batch, DN_H),
        in_specs=in_specs,
        out_specs=out_specs,
        out_shape=out_shape,
        scratch_shapes=[tok, tok, tok, both, both, both, both, both16, both16, both16, both16],
        compiler_params=_params("parallel", "arbitrary"),
        name="deltanet",
    )(*args)


def _rope_tables(n_tok):
    t = jnp.arange(n_tok)
    inv = 1.0 / (ROPE_THETA ** (jnp.arange(ROPE_FREQ, dtype=F32) / ROPE_FREQ))
    pos = jnp.stack([t // GRID_W, t % GRID_W], axis=1).astype(F32)
    ang = pos[:, :, None] * inv
    cos = jnp.concatenate([jnp.cos(ang), jnp.cos(ang)], axis=-1).reshape(n_tok, HEAD_DIM)
    sin = jnp.concatenate([-jnp.sin(ang), jnp.sin(ang)], axis=-1).reshape(n_tok, HEAD_DIM)
    return cos, sin


def _pad_lanes(v):
    return jnp.pad(v.reshape(1, -1), ((0, 0), (0, LANES - v.size)))


def kernel(x_prompt, x_sample, state_gla, cache_gqa_k, cache_gqa_v, cache_na_k, cache_na_v, state_delta, c, c_ctx, norm1, norm2, w_ada, b_ada, w_mlp1, w_mlp2, ev_w_in, ev_w_a2, ev_b_a2, ev_gla_norm, ev_q_norm, ev_k_norm, ev_w_out, od_w_in, od_conv, od_a_log, od_dt_bias, od_dn_norm, od_rpb, od_w_out, norm_f):
    n_ctx, seq_ctx, d = x_prompt.shape
    n_lat, seq_lat, _ = x_sample.shape
    depth = w_ada.shape[0]

    w_ev = ev_w_in[0]
    w_ev = jnp.concatenate(
        [w_ev[:, :EV_AQ], w_ev[:, EV_AQ + 2 * GLA_RANK:], w_ev[:, EV_AQ:EV_AQ + 2 * GLA_RANK],
         jnp.zeros((d, EV_COLS_PAD - w_ev.shape[1]), F32)], axis=1).astype(BF16)
    w_od = jnp.pad(od_w_in[0], ((0, 0), (0, OD_COLS_PAD - od_w_in.shape[2]))).astype(BF16)
    w_in = [w_ev, w_od]
    w_out = [ev_w_out[0].astype(BF16), od_w_out[0].astype(BF16)]
    w1 = w_mlp1.astype(BF16)
    w2 = w_mlp2.astype(BF16)
    w2pad = jnp.zeros((2, LANES, GLA_H * GLA_DK), F32)
    for dd in range(2):
        w2pad = w2pad.at[dd, dd * GLA_RANK:(dd + 1) * GLA_RANK].set(ev_w_a2[0, dd])
    b2 = ev_b_a2[0].reshape(2, 1, GLA_H * GLA_DK)
    alog_row = _pad_lanes(od_a_log[0])
    dtb_row = _pad_lanes(od_dt_bias[0])
    na_bias = _na_bias_table(od_rpb[0])
    rope = _rope_tables(seq_lat)

    cond8 = jnp.concatenate([c_ctx[None, :], c, jnp.zeros((8 - 1 - n_lat, d), F32)], axis=0)
    mod = _adaln(cond8, w_ada, b_ada).reshape(depth, 8, 6, 1, d)

    def trunk(x, n_batch, seq, row_fn, caches):
        latent = caches is not None
        tm = 512
        outs = {}
        proj = _inproj(x, norm1[0:1], mod, 0, row_fn, w_in[0], tm)
        if latent:
            s0 = caches["gla"].reshape(n_batch, 2, GLA_H // 2, LANES, GLA_DV)
            mix_a = _gla(proj, n_batch, seq, w2pad, b2, ev_gla_norm, s0, False)
            mix_b = _attention(proj, n_batch, seq, EV_AQ, EV_AK, EV_AV, GQA_KV, GQA_H // GQA_KV,
                               norms=(ev_q_norm, ev_k_norm), rope=rope, ctx=caches["gqa"])
        else:
            mix_a, st = _gla(proj, n_batch, seq, w2pad, b2, ev_gla_norm, None, True)
            mix_b, k_new = _attention(proj, n_batch, seq, EV_AQ, EV_AK, EV_AV, GQA_KV, GQA_H // GQA_KV,
                                      norms=(ev_q_norm, ev_k_norm), write_k=True)
            outs["st_gla"] = st.reshape(n_batch, 1, 2, GLA_H, GLA_DK, GLA_DV)
            outs["ck_gqa"] = k_new.reshape(n_batch, 1, seq, GQA_KV, HEAD_DIM)
            outs["cv_gqa"] = proj[:, EV_AV:EV_AV + GQA_KV * HEAD_DIM].reshape(n_batch, 1, seq, GQA_KV, HEAD_DIM)
        x = _outproj(mix_a, mix_b, w_out[0], x, mod, 0, row_fn, tm)
        x = _mlp(x, norm2[0:1], mod, 0, row_fn, w1[0], w2[0], norm_f[None, :], False, tm)
        proj = _inproj(x, norm1[1:2], mod, 1, row_fn, w_in[1], tm)
        if latent:
            mix_a = _neighbourhood(proj, n_batch, seq, caches["na"][0], caches["na"][1], na_bias)
            s0 = caches["delta"].reshape(n_batch, 2, DN_H, HEAD_DIM, HEAD_DIM)
            mix_b = _delta(proj, n_batch, seq, od_conv[0], alog_row, dtb_row, od_dn_norm, s0, False)
        else:
            mix_a = _attention(proj, n_batch, seq, OD_NQ, OD_NK, OD_NV, NA_H, 1)
            mix_b, st = _delta(proj, n_batch, seq, od_conv[0], alog_row, dtb_row, od_dn_norm, None, True)
            outs["ck_na"] = proj[:, OD_NK:OD_NK + NA_H * HEAD_DIM].reshape(n_batch, 1, seq, NA_H, HEAD_DIM)
            outs["cv_na"] = proj[:, OD_NV:OD_NV + NA_H * HEAD_DIM].reshape(n_batch, 1, seq, NA_H, HEAD_DIM)
            outs["st_dn"] = st.reshape(n_batch, 1, 2, DN_H, HEAD_DIM, HEAD_DIM)
        x = _outproj(mix_a, mix_b, w_out[1], x, mod, 1, row_fn, tm)
        y = _mlp(x, norm2[1:2], mod, 1, row_fn, w1[1], w2[1], norm_f[None, :], True, tm)
        return y.reshape(n_batch, seq, d), outs

    y_prompt, new = trunk(x_prompt.reshape(n_ctx * seq_ctx, d), n_ctx, seq_ctx, lambda i: 0, None)
    past = cache_gqa_k.shape[2]
    caches = {
        "gla": state_gla[:, 0],
        "gqa": (cache_gqa_k[:, 0].reshape(n_lat, past, GQA_KV * HEAD_DIM),
                cache_gqa_v[:, 0].reshape(n_lat, past, GQA_KV * HEAD_DIM)),
        "na": (cache_na_k[:, 0].reshape(n_lat, past, NA_H * HEAD_DIM),
               cache_na_v[:, 0].reshape(n_lat, past, NA_H * HEAD_DIM)),
        "delta": state_delta[:, 0],
    }
    y_sample, _ = trunk(x_sample.reshape(n_lat * seq_lat, d), n_lat, seq_lat,
                        lambda i: 1 + (i * 512) // seq_lat, caches)
    return (y_prompt, y_sample, new["st_gla"], new["ck_gqa"], new["cv_gqa"], new["ck_na"], new["cv_na"],
            new["st_dn"])
```

```python
import functools

import jax
import jax.numpy as jnp
import numpy as np
from jax import lax
from jax.experimental import pallas as pl
from jax.experimental.pallas import tpu as pltpu

F32 = jnp.float32
BF16 = jnp.bfloat16
HIGHEST = lax.Precision.HIGHEST

D_MODEL = 2048
D_FF = 4 * D_MODEL
HEAD_DIM = 128
LANES = 128
GRID_W = 64
GLA_H = 8
GLA_DK = 64
GLA_DV = 128
GLA_RANK = 16
GLA_TAU = 16.0
GQA_H = 8
GQA_KV = 2
NA_H = 8
NA_WIN_R = 8
NA_WIN_C = 16
DN_H = 8
CHUNK = 64
DN_CHUNK = 128
DN_REFINE_STEPS = 1
DN_UNITS = 8
TM_INPROJ = 1024
TM_MLP = 512
NA_ROWS_PER_STEP = 4
ROPE_THETA = 10000.0
ROPE_FREQ = HEAD_DIM // 4
EPS = 1e-6
MASK_VALUE = -1e30

EV_Q, EV_K, EV_V, EV_G = 0, 512, 1024, 2048
EV_AQ, EV_AK, EV_AV, EV_LO = 3072, 4096, 4352, 4608
EV_COLS_PAD = 5120
OD_NQ, OD_NK, OD_NV = 0, 1024, 2048
OD_DQ, OD_DK, OD_DV, OD_DZ, OD_AB = 3072, 4096, 5120, 6144, 7168
OD_COLS_PAD = 7680

VMEM_LIMIT = 56 * 1024 * 1024


def _params(*sem):
    return pltpu.CompilerParams(dimension_semantics=sem, vmem_limit_bytes=VMEM_LIMIT)


def _dot(a, b):
    return jnp.dot(a, b, preferred_element_type=F32)


def _dot_nt(a, b):
    return lax.dot_general(a, b, (((1,), (1,)), ((), ())), preferred_element_type=F32)


def _dot_tn(a, b):
    return lax.dot_general(a, b, (((0,), (0,)), ((), ())), preferred_element_type=F32)


def _dot_hi(a, b):
    return jnp.dot(a, b, precision=HIGHEST, preferred_element_type=F32)


def _sigmoid(x):
    return 1.0 / (1.0 + jnp.exp(-x))


def _silu(x):
    return x * _sigmoid(x)


def _softplus(x):
    return jnp.maximum(x, 0.0) + jnp.log(1.0 + jnp.exp(-jnp.abs(x)))


def _rms(x, g):
    return x * lax.rsqrt(jnp.mean(x * x, axis=-1, keepdims=True) + EPS) * g


def _rms_mod(x, g, sc, sh):
    return _rms(x, g) * (1.0 + sc) + sh


def _adaln_kernel(c_ref, w_ref, b_ref, o_ref):
    a = _silu(c_ref[...]).astype(BF16)
    o_ref[...] = _dot(a, w_ref[...].astype(BF16)) + b_ref[...]


def _adaln(cond8, w_ada, b_ada):
    depth, d, n = w_ada.shape
    tn = 512
    return pl.pallas_call(
        _adaln_kernel,
        grid=(depth, n // tn),
        in_specs=[
            pl.BlockSpec((8, d), lambda l, j: (0, 0)),
            pl.BlockSpec((None, d, tn), lambda l, j: (l, 0, j)),
            pl.BlockSpec((None, 1, tn), lambda l, j: (l, 0, j)),
        ],
        out_specs=pl.BlockSpec((None, 8, tn), lambda l, j: (l, 0, j)),
        out_shape=jax.ShapeDtypeStruct((depth, 8, n), F32),
        compiler_params=_params("parallel", "parallel"),
        name="adaln",
    )(cond8, w_ada, b_ada.reshape(depth, 1, n))


def _mod_spec(layer, k, row_fn, tm, tn=D_MODEL):
    return pl.BlockSpec((None, None, None, 1, tn), lambda i, j: (layer, row_fn(i * tm), k, 0, j if tn < D_MODEL else 0))


def _inproj_kernel(x_ref, g_ref, sc_ref, sh_ref, w_ref, o_ref, h_ref):
    @pl.when(pl.program_id(1) == 0)
    def _():
        h_ref[...] = _rms_mod(x_ref[...], g_ref[...], sc_ref[...], sh_ref[...]).astype(BF16)

    o_ref[...] = _dot(h_ref[...], w_ref[...])


def _inproj(x, g, mod, layer, row_fn, w, tm):
    m, d = x.shape
    n = w.shape[1]
    tn = 512
    return pl.pallas_call(
        _inproj_kernel,
        grid=(m // tm, n // tn),
        in_specs=[
            pl.BlockSpec((tm, d), lambda i, j: (i, 0)),
            pl.BlockSpec((1, d), lambda i, j: (0, 0)),
            _mod_spec(layer, 1, row_fn, tm),
            _mod_spec(layer, 0, row_fn, tm),
            pl.BlockSpec((d, tn), lambda i, j: (0, j)),
        ],
        out_specs=pl.BlockSpec((tm, tn), lambda i, j: (i, j)),
        out_shape=jax.ShapeDtypeStruct((m, n), F32),
        scratch_shapes=[pltpu.VMEM((tm, d), BF16)],
        compiler_params=_params("parallel", "arbitrary"),
        name="inproj",
    )(x, g, mod, mod, w)


def _outproj_kernel(a_ref, b_ref, wa_ref, wb_ref, x_ref, gt_ref, o_ref):
    acc = _dot(a_ref[...], wa_ref[...]) + _dot(b_ref[...], wb_ref[...])
    o_ref[...] = x_ref[...] + gt_ref[...] * acc


def _outproj(mix_a, mix_b, w, x, mod, layer, row_fn, tm):
    m, d = x.shape
    ka, kb = mix_a.shape[1], mix_b.shape[1]
    tn = d
    gt_spec = _mod_spec(layer, 2, row_fn, tm, tn)
    return pl.pallas_call(
        _outproj_kernel,
        grid=(m // tm, d // tn),
        in_specs=[
            pl.BlockSpec((tm, ka), lambda i, j: (i, 0)),
            pl.BlockSpec((tm, kb), lambda i, j: (i, 0)),
            pl.BlockSpec((ka, tn), lambda i, j: (0, j)),
            pl.BlockSpec((kb, tn), lambda i, j: (1, j)),
            pl.BlockSpec((tm, tn), lambda i, j: (i, j)),
            gt_spec,
        ],
        out_specs=pl.BlockSpec((tm, tn), lambda i, j: (i, j)),
        out_shape=jax.ShapeDtypeStruct((m, d), F32),
        compiler_params=_params("parallel", "parallel"),
        name="outproj",
    )(mix_a, mix_b, w, w, x, mod)


def _mlp_kernel(x_ref, g_ref, sc_ref, sh_ref, gt_ref, w1_ref, w2_ref, gf_ref, o_ref, h_ref, acc_ref, *, final_norm):
    j = pl.program_id(1)

    @pl.when(j == 0)
    def _():
        h_ref[...] = _rms_mod(x_ref[...], g_ref[...], sc_ref[...], sh_ref[...]).astype(BF16)
        acc_ref[...] = jnp.zeros_like(acc_ref)

    hid = jnp.maximum(_dot(h_ref[...], w1_ref[...]), 0.0)
    acc_ref[...] += _dot((hid * hid).astype(BF16), w2_ref[...])

    @pl.when(j == pl.num_programs(1) - 1)
    def _():
        y = x_ref[...] + gt_ref[...] * acc_ref[...]
        if final_norm:
            y = _rms(y, gf_ref[...])
        o_ref[...] = y


def _mlp(x, g, mod, layer, row_fn, w1, w2, gf, final_norm, tm):
    m, d = x.shape
    ff = w1.shape[2]
    tf = 512
    return pl.pallas_call(
        functools.partial(_mlp_kernel, final_norm=final_norm),
        grid=(m // tm, ff // tf),
        in_specs=[
            pl.BlockSpec((tm, d), lambda i, j: (i, 0)),
            pl.BlockSpec((1, d), lambda i, j: (0, 0)),
            _mod_spec(layer, 4, row_fn, tm),
            _mod_spec(layer, 3, row_fn, tm),
            _mod_spec(layer, 5, row_fn, tm),
            pl.BlockSpec((None, d, tf), lambda i, j: (layer, 0, j)),
            pl.BlockSpec((None, tf, d), lambda i, j: (layer, j, 0)),
            pl.BlockSpec((1, d), lambda i, j: (0, 0)),
        ],
        out_specs=pl.BlockSpec((tm, d), lambda i, j: (i, 0)),
        out_shape=jax.ShapeDtypeStruct((m, d), F32),
        scratch_shapes=[pltpu.VMEM((tm, d), BF16), pltpu.VMEM((tm, d), F32)],
        compiler_params=_params("parallel", "arbitrary"),
        name="mlp",
    )(x, g, mod, mod, mod, w1, w2, gf)


def _rope(x, cos, sin):
    lane = lax.broadcasted_iota(jnp.int32, x.shape, 1)
    first_half = (lane % (2 * ROPE_FREQ)) < ROPE_FREQ
    partner = jnp.where(first_half, pltpu.roll(x, LANES - ROPE_FREQ, 1), pltpu.roll(x, ROPE_FREQ, 1))
    return x * cos + partner * sin


def _attn_kernel(*refs, group, use_norm, use_rope, has_ctx, write_k):
    it = iter(refs)
    q_ref, k_ref, v_ref = next(it), next(it), next(it)
    qn_ref = kn_ref = cq_ref = sq_ref = ck_ref = sk_ref = kc_ref = vc_ref = knew_ref = None
    if use_norm:
        qn_ref, kn_ref = next(it), next(it)
    if use_rope:
        cq_ref, sq_ref, ck_ref, sk_ref = next(it), next(it), next(it), next(it)
    if has_ctx:
        kc_ref, vc_ref = next(it), next(it)
    o_ref = next(it)
    if write_k:
        knew_ref = next(it)
    kbuf, vbuf = next(it), next(it)
    scale = HEAD_DIM ** -0.5

    @pl.when(pl.program_id(2) == 0)
    def _():
        k = k_ref[...]
        if use_norm:
            k = _rms(k, kn_ref[...])
        if write_k:
            knew_ref[...] = k
        if use_rope:
            k = _rope(k, ck_ref[...], sk_ref[...])
        kbuf[...] = k.astype(BF16)
        vbuf[...] = v_ref[...].astype(BF16)

    heads = range(group)
    cols = [slice(g * HEAD_DIM, (g + 1) * HEAD_DIM) for g in heads]
    qb = []
    for g in heads:
        q = q_ref[:, cols[g]]
        if use_norm:
            q = _rms(q, qn_ref[...])
        if use_rope:
            q = _rope(q, cq_ref[...], sq_ref[...])
        qb.append(q.astype(BF16))
    s = [_dot_nt(qb[g], kbuf[...]) * scale for g in heads]
    m = [jnp.max(s[g], axis=-1, keepdims=True) for g in heads]
    if has_ctx:
        kc16 = kc_ref[...].astype(BF16)
        s_c = [_dot_nt(qb[g], kc16) * scale for g in heads]
        m = [jnp.maximum(m[g], jnp.max(s_c[g], axis=-1, keepdims=True)) for g in heads]
    p = [jnp.exp(s[g] - m[g]) for g in heads]
    l = [jnp.sum(p[g], axis=-1, keepdims=True) for g in heads]
    o = [_dot(p[g].astype(BF16), vbuf[...]) for g in heads]
    if has_ctx:
        vc16 = vc_ref[...].astype(BF16)
        p_c = [jnp.exp(s_c[g] - m[g]) for g in heads]
        l = [l[g] + jnp.sum(p_c[g], axis=-1, keepdims=True) for g in heads]
        o = [o[g] + _dot(p_c[g].astype(BF16), vc16) for g in heads]
    for g in heads:
        o_ref[:, cols[g]] = (o[g] / l[g]).astype(o_ref.dtype)


def _attention(proj, n_batch, seq, q_col, k_col, v_col, n_kv, group, norms=None, rope=None, ctx=None,
               write_k=False):
    tq = min(seq, 256)
    nq = seq // tq
    gw = group * HEAD_DIM
    in_specs = [
        pl.BlockSpec((tq, gw), lambda b, h, i: (b * nq + i, q_col // gw + h)),
        pl.BlockSpec((seq, HEAD_DIM), lambda b, h, i: (b, k_col // HEAD_DIM + h)),
        pl.BlockSpec((seq, HEAD_DIM), lambda b, h, i: (b, v_col // HEAD_DIM + h)),
    ]
    args = [proj, proj, proj]
    vec = pl.BlockSpec((1, HEAD_DIM), lambda b, h, i: (0, 0))
    if norms is not None:
        in_specs += [vec, vec]
        args += [norms[0], norms[1]]
    if rope is not None:
        tab_q = pl.BlockSpec((tq, HEAD_DIM), lambda b, h, i: (i, 0))
        tab_k = pl.BlockSpec((seq, HEAD_DIM), lambda b, h, i: (0, 0))
        in_specs += [tab_q, tab_q, tab_k, tab_k]
        args += [rope[0], rope[1], rope[0], rope[1]]
    if ctx is not None:
        past = ctx[0].shape[1]
        c_spec = pl.BlockSpec((None, past, HEAD_DIM), lambda b, h, i: (b, 0, h))
        in_specs += [c_spec, c_spec]
        args += [ctx[0], ctx[1]]
    o_spec = pl.BlockSpec((tq, gw), lambda b, h, i: (b * nq + i, h))
    o_shape = jax.ShapeDtypeStruct((n_batch * seq, n_kv * gw), BF16)
    if write_k:
        out_specs = [o_spec, pl.BlockSpec((seq, HEAD_DIM), lambda b, h, i: (b, h))]
        out_shape = [o_shape, jax.ShapeDtypeStruct((n_batch * seq, n_kv * HEAD_DIM), F32)]
    else:
        out_specs, out_shape = o_spec, o_shape
    return pl.pallas_call(
        functools.partial(_attn_kernel, group=group, use_norm=norms is not None, use_rope=rope is not None,
                          has_ctx=ctx is not None, write_k=write_k),
        grid=(n_batch, n_kv, nq),
        in_specs=in_specs,
        out_specs=out_specs,
        out_shape=out_shape,
        scratch_shapes=[pltpu.VMEM((seq, HEAD_DIM), BF16), pltpu.VMEM((seq, HEAD_DIM), BF16)],
        compiler_params=_params("parallel", "parallel", "arbitrary"),
        name="attention",
    )(*args)


def _na_row_start(r, rows):
    return jnp.clip(r - NA_WIN_R // 2, 0, rows - NA_WIN_R)


def _na_kernel(q_ref, k_ref, v_ref, kc_ref, vc_ref, bias_ref, o_ref, kb, vb, kcb, vcb, *, rows):
    win = NA_WIN_R * GRID_W
    scale = HEAD_DIM ** -0.5
    kb[...] = k_ref[...].astype(BF16)
    vb[...] = v_ref[...].astype(BF16)
    kcb[...] = kc_ref[...].astype(BF16)
    vcb[...] = vc_ref[...].astype(BF16)

    def body(j, carry):
        rr = [j * NA_ROWS_PER_STEP + i for i in range(NA_ROWS_PER_STEP)]
        rs = [_na_row_start(r, rows) for r in rr]
        k_rows = [pl.ds(pl.multiple_of(s * GRID_W, GRID_W), win) for s in rs]
        q_rows = [pl.ds(pl.multiple_of(r * GRID_W, GRID_W), GRID_W) for r in rr]
        qb = [q_ref[q_rows[i], :].astype(BF16) for i in range(NA_ROWS_PER_STEP)]
        s_w = [_dot_nt(qb[i], kb[k_rows[i], :]) * scale for i in range(NA_ROWS_PER_STEP)]
        s_c = [_dot_nt(qb[i], kcb[...]) * scale for i in range(NA_ROWS_PER_STEP)]
        p_w, p_c, l = [], [], []
        for i in range(NA_ROWS_PER_STEP):
            delta = rr[i] - rs[i]
            bias = jnp.concatenate(
                [bias_ref[2 * jj - delta + NA_WIN_R - 1] for jj in range(NA_WIN_R // 2)], axis=-1)
            sw = s_w[i] + bias
            m = jnp.maximum(jnp.max(sw, axis=-1, keepdims=True), jnp.max(s_c[i], axis=-1, keepdims=True))
            pw = jnp.exp(sw - m)
            pc = jnp.exp(s_c[i] - m)
            l.append(jnp.sum(pw, axis=-1, keepdims=True) + jnp.sum(pc, axis=-1, keepdims=True))
            p_w.append(pw.astype(BF16))
            p_c.append(pc.astype(BF16))
        o_w = [_dot(p_w[i], vb[k_rows[i], :]) for i in range(NA_ROWS_PER_STEP)]
        o_c = [_dot(p_c[i], vcb[...]) for i in range(NA_ROWS_PER_STEP)]
        for i in range(NA_ROWS_PER_STEP):
            o_ref[q_rows[i], :] = ((o_w[i] + o_c[i]) / l[i]).astype(o_ref.dtype)
        return carry

    lax.fori_loop(0, rows // NA_ROWS_PER_STEP, body, 0)


def _na_bias_table(rpb):
    cols = jnp.arange(GRID_W)
    start = jnp.clip(cols - NA_WIN_C // 2, 0, GRID_W - NA_WIN_C)
    inside = (cols[None, :] >= start[:, None]) & (cols[None, :] < start[:, None] + NA_WIN_C)
    rel = cols[None, :] - cols[:, None] + NA_WIN_C - 1
    onehot = (rel[:, :, None] == jnp.arange(2 * NA_WIN_C - 1)).astype(F32)
    picked = jnp.einsum("hrm,ckm->hrck", rpb, onehot, precision=HIGHEST)
    tab = jnp.where(inside[None, None], picked, MASK_VALUE)
    return jnp.concatenate([tab[:, :-1], tab[:, 1:]], axis=-1)


def _neighbourhood(proj, n_batch, seq, k_ctx, v_ctx, bias):
    rows = seq // GRID_W
    past = k_ctx.shape[1]
    c_spec = pl.BlockSpec((None, past, HEAD_DIM), lambda b, h: (b, 0, h))

    def col_spec(col0):
        return pl.BlockSpec((seq, HEAD_DIM), lambda b, h: (b, col0 // HEAD_DIM + h))

    return pl.pallas_call(
        functools.partial(_na_kernel, rows=rows),
        grid=(n_batch, NA_H),
        in_specs=[
            col_spec(OD_NQ), col_spec(OD_NK), col_spec(OD_NV), c_spec, c_spec,
            pl.BlockSpec((None, 2 * NA_WIN_R - 2, GRID_W, 2 * GRID_W), lambda b, h: (h, 0, 0, 0)),
        ],
        out_specs=pl.BlockSpec((seq, HEAD_DIM), lambda b, h: (b, h)),
        out_shape=jax.ShapeDtypeStruct((n_batch * seq, NA_H * HEAD_DIM), BF16),
        scratch_shapes=[pltpu.VMEM((seq, HEAD_DIM), BF16), pltpu.VMEM((seq, HEAD_DIM), BF16),
                        pltpu.VMEM((past, HEAD_DIM), BF16), pltpu.VMEM((past, HEAD_DIM), BF16)],
        compiler_params=_params("parallel", "parallel"),
        name="neighbourhood",
    )(proj, proj, proj, k_ctx, v_ctx, bias)


def _order_masks(n=CHUNK):
    row = lax.broadcasted_iota(jnp.int32, (n, n), 0)
    col = lax.broadcasted_iota(jnp.int32, (n, n), 1)
    return row, col


def _chunk_cumsum(x, chunk, reverse):
    seq = x.shape[0]
    pos = lax.broadcasted_iota(jnp.int32, x.shape, 0) % chunk
    shift = 1
    while shift < chunk:
        if reverse:
            x = x + jnp.where(pos < chunk - shift, pltpu.roll(x, seq - shift, 0), 0.0)
        else:
            x = x + jnp.where(pos >= shift, pltpu.roll(x, shift, 0), 0.0)
        shift *= 2
    return x


def _gla_kernel(*refs, seq, has_s0, write_state):
    it = iter(refs)
    q_ref, k_ref, v_ref, gg_ref, lo_ref, w2_ref, b2_ref, gn_ref = (next(it) for _ in range(8))
    s0_ref = next(it) if has_s0 else None
    mix_ref = next(it)
    st_ref = next(it) if write_state else None
    o_s, bc_s = next(it), next(it)
    n_chunks = seq // CHUNK
    lane = lax.broadcasted_iota(jnp.int32, (1, LANES), 1)
    head_mask = [(lane < GLA_DK).astype(F32), (lane >= GLA_DK).astype(F32)]
    row, col = _order_masks()
    before = [col <= row, col >= row]

    lo = lo_ref[...].astype(BF16)
    for d in range(2):
        z = _dot(lo, w2_ref[d].astype(BF16)) + b2_ref[d]
        bc_s[d] = _chunk_cumsum(-_softplus(-z) * (1.0 / GLA_TAU), CHUNK, d == 1)

    def body(i, carry):
        rows = [pl.ds(pl.multiple_of(n * CHUNK, CHUNK), CHUNK) for n in (i, n_chunks - 1 - i)]
        chains = [(d, p) for d in range(2) for p in range(2)]
        qe, ke, kd, dec = [], [], [], []
        for d in range(2):
            bc = bc_s[d, rows[d], :]
            bl = bc[CHUNK - 1:CHUNK, :] if d == 0 else bc[0:1, :]
            k = k_ref[rows[d], :]
            qe.append(q_ref[rows[d], :] * (GLA_DK ** -0.5) * jnp.exp(bc))
            ke.append((k * jnp.exp(-bc)).astype(BF16))
            kd.append(k * jnp.exp(bl - bc))
            dec.append(jnp.exp(bl))
        cols = [slice(p * GLA_DV, (p + 1) * GLA_DV) for p in range(2)]
        qep = [(qe[d] * head_mask[p]).astype(BF16) for d, p in chains]
        kdp = [(kd[d] * head_mask[p]).astype(BF16) for d, p in chains]
        vp = [v_ref[rows[d], cols[p]].astype(BF16) for d, p in chains]
        sc = [jnp.where(before[d], _dot_nt(qep[c], ke[d]), 0.0).astype(BF16) for c, (d, p) in enumerate(chains)]
        inter = [_dot_nt(qep[c], carry[c].astype(BF16)) for c in range(4)]
        kv = [_dot_tn(vp[c], kdp[c]) for c in range(4)]
        for c, (d, p) in enumerate(chains):
            o_s[d, rows[d], cols[p]] = _dot(sc[c], vp[c]) + inter[c]
        return tuple(carry[c] * dec[d] + kv[c] for c, (d, p) in enumerate(chains))

    init = []
    for d in range(2):
        if has_s0:
            s0t = s0_ref[d].T
            init += [s0t * head_mask[0], s0t * head_mask[1]]
        else:
            init += [jnp.zeros((GLA_DV, LANES), F32), jnp.zeros((GLA_DV, LANES), F32)]
    fin = lax.fori_loop(0, n_chunks, body, tuple(init), unroll=2)
    if write_state:
        for d in range(2):
            st_ref[d] = (fin[2 * d] + fin[2 * d + 1]).T

    for p in range(2):
        cols = slice(p * GLA_DV, (p + 1) * GLA_DV)
        y = _rms(o_s[0, :, cols] + o_s[1, :, cols], gn_ref[...]) * _silu(gg_ref[:, cols])
        mix_ref[:, cols] = y.astype(mix_ref.dtype)


def _gla(proj, n_batch, seq, w2pad, b2, gn, s0, write_state):
    pairs = GLA_H // 2
    pw = 2 * GLA_DV
    st_spec = pl.BlockSpec((None, 2, None, LANES, GLA_DV), lambda b, p: (b, 0, p, 0, 0))
    in_specs = [
        pl.BlockSpec((seq, LANES), lambda b, p: (b, EV_Q // LANES + p)),
        pl.BlockSpec((seq, LANES), lambda b, p: (b, EV_K // LANES + p)),
        pl.BlockSpec((seq, pw), lambda b, p: (b, EV_V // pw + p)),
        pl.BlockSpec((seq, pw), lambda b, p: (b, EV_G // pw + p)),
        pl.BlockSpec((seq, LANES), lambda b, p: (b, EV_LO // LANES)),
        pl.BlockSpec((2, LANES, LANES), lambda b, p: (0, 0, p)),
        pl.BlockSpec((2, 1, LANES), lambda b, p: (0, 0, p)),
        pl.BlockSpec((1, GLA_DV), lambda b, p: (0, 0)),
    ]
    args = [proj, proj, proj, proj, proj, w2pad, b2, gn]
    if s0 is not None:
        in_specs.append(st_spec)
        args.append(s0)
    mix_spec = pl.BlockSpec((seq, pw), lambda b, p: (b, p))
    mix_shape = jax.ShapeDtypeStruct((n_batch * seq, GLA_H * GLA_DV), BF16)
    if write_state:
        out_specs = [mix_spec, st_spec]
        out_shape = [mix_shape, jax.ShapeDtypeStruct((n_batch, 2, pairs, LANES, GLA_DV), F32)]
    else:
        out_specs, out_shape = mix_spec, mix_shape
    return pl.pallas_call(
        functools.partial(_gla_kernel, seq=seq, has_s0=s0 is not None, write_state=write_state),
        grid=(n_batch, pairs),
        in_specs=in_specs,
        out_specs=out_specs,
        out_shape=out_shape,
        scratch_shapes=[pltpu.VMEM((2, seq, pw), F32), pltpu.VMEM((2, seq, LANES), F32)],
        compiler_params=_params("parallel", "parallel"),
        name="gla",
    )(*args)


def _conv_silu(x, w):
    seq = x.shape[0]
    t = lax.broadcasted_iota(jnp.int32, x.shape, 0)
    prev = jnp.where(t == 0, 0.0, pltpu.roll(x, 1, 0))
    nxt = jnp.where(t == seq - 1, 0.0, pltpu.roll(x, seq - 1, 0))
    return _silu(prev * w[0:1, :] + x * w[1:2, :] + nxt * w[2:3, :])


def _hi_lo(x):
    hi = x.astype(BF16)
    return hi, (x - hi.astype(F32)).astype(BF16)


def _split_lanes(a):
    hi, lo = _hi_lo(a)
    return jnp.concatenate([hi, hi, lo], axis=1)


def _split_rows(b):
    hi, lo = _hi_lo(b)
    return jnp.concatenate([hi, lo, hi], axis=0)


def _l2norm(x):
    return x * lax.rsqrt(jnp.sum(x * x, axis=-1, keepdims=True) + EPS)


def _delta_kernel(*refs, seq, hp, cpi, has_s0, write_state):
    it = iter(refs)
    (q_ref, k_ref, v_ref, z_ref, ab_ref, wq_ref, wk_ref, wv_ref, alog_ref, dtb_ref,
     gn_ref) = (next(it) for _ in range(11))
    s0_ref = next(it) if has_s0 else None
    mix_ref = next(it)
    st_ref = next(it) if write_state else None
    qs, ks, vs, o_s, gc_s, b_s, u0_s, kc_s, at_s, qe_s, kd_s = (next(it) for _ in range(11))
    n_chunks = seq // DN_CHUNK
    lane = lax.broadcasted_iota(jnp.int32, (1, LANES), 1)
    row, col = _order_masks(DN_CHUNK)
    eye = (row == col).astype(F32)
    before = [col <= row, col >= row]
    strict = [col < row, col > row]
    couples = [((row >> (lvl + 1)) == (col >> (lvl + 1))) & ((row >> lvl) != (col >> lvl))
               for lvl in range(DN_CHUNK.bit_length() - 1)]

    ab = ab_ref[...]
    g_all = -jnp.exp(alog_ref[...]) * _softplus(ab + dtb_ref[...])
    beta_all = _sigmoid(ab)
    for hd in range(hp):
        cols = slice(hd * HEAD_DIM, (hd + 1) * HEAD_DIM)
        h = pl.program_id(1) * hp + hd
        qs[hd] = _l2norm(_conv_silu(q_ref[:, cols], wq_ref[:, cols])) * (HEAD_DIM ** -0.5)
        ks[hd] = _l2norm(_conv_silu(k_ref[:, cols], wk_ref[:, cols]))
        vs[hd] = _conv_silu(v_ref[:, cols], wv_ref[:, cols])
        for d in range(2):
            sel_g = lane == d * DN_H + h
            sel_b = lane == 2 * DN_H + d * DN_H + h
            g_col = jnp.broadcast_to(jnp.sum(jnp.where(sel_g, g_all, 0.0), axis=-1, keepdims=True), (seq, LANES))
            gc_s[hd, d] = _chunk_cumsum(g_col, DN_CHUNK, d == 1)
            b_s[hd, d] = jnp.broadcast_to(
                jnp.sum(jnp.where(sel_b, beta_all, 0.0), axis=-1, keepdims=True), (seq, LANES))

    def chunk_rows(n):
        return pl.ds(pl.multiple_of(n * DN_CHUNK, DN_CHUNK), DN_CHUNK)

    def last_row(d, n):
        return pl.ds(n * DN_CHUNK + (DN_CHUNK - 1 if d == 0 else 0), 1)

    def wy_chunks(units):
        nu = range(len(units))
        rows = [chunk_rows(n) for _, _, n in units]
        gc = [gc_s[hd, d, rows[i], :] for i, (hd, d, _) in enumerate(units)]
        bb = [b_s[hd, d, rows[i], :] for i, (hd, d, _) in enumerate(units)]
        k = [ks[hd, rows[i], :] for i, (hd, _, _) in enumerate(units)]
        k16 = [k[i].astype(BF16) for i in nu]
        kbeta = [k[i] * bb[i] for i in nu]
        kk = [_dot_nt(kbeta[i].astype(BF16), k16[i]) for i in nu]
        qk = [_dot_nt(qs[hd, rows[i], :].astype(BF16), k16[i]) for i, (hd, _, _) in enumerate(units)]
        mm = []
        for i, (hd, d, _) in enumerate(units):
            decay = jnp.where(before[d], jnp.exp(jnp.where(before[d], gc[i] - gc[i].T, 0.0)), 0.0)
            mm.append(jnp.where(strict[d], kk[i] * decay, 0.0))
            at_s[hd, d, rows[i], :] = (qk[i] * decay).astype(BF16)
        inv = [eye - jnp.where(couples[0], mm[i], 0.0) for i in nu]
        for lvl in range(1, len(couples)):
            inv16 = [inv[i].astype(BF16) for i in nu]
            c_inv = [_dot(jnp.where(couples[lvl], mm[i], 0.0).astype(BF16), inv16[i]).astype(BF16) for i in nu]
            inv = [inv[i] - _dot(inv16[i], c_inv[i]) for i in nu]
        inv16 = [inv[i].astype(BF16) for i in nu]
        egc = [jnp.exp(gc[i]) for i in nu]
        rhs = [jnp.concatenate([vs[hd, rows[i], :] * bb[i], kbeta[i] * egc[i]], axis=-1)
               for i, (hd, _, _) in enumerate(units)]
        sol = [_dot(inv16[i], rhs[i].astype(BF16)) for i in nu]
        mm_split = [_split_lanes(mm[i]) for i in nu]
        for _ in range(DN_REFINE_STEPS):
            res = [rhs[i] - sol[i] - _dot(mm_split[i], _split_rows(sol[i])) for i in nu]
            sol = [sol[i] + _dot(inv16[i], res[i].astype(BF16)) for i in nu]
        for i, (hd, d, _) in enumerate(units):
            u0_s[hd, d, rows[i], :] = sol[i][:, :HEAD_DIM]
            kc_s[hd, d, rows[i], :] = sol[i][:, HEAD_DIM:].astype(BF16)
            g_last = gc[i][DN_CHUNK - 1:DN_CHUNK, :] if d == 0 else gc[i][0:1, :]
            qe_s[hd, d, rows[i], :] = (qs[hd, rows[i], :] * egc[i]).astype(BF16)
            kd_s[hd, d, rows[i], :] = (k[i] * jnp.exp(g_last - gc[i])).astype(BF16)

    def wy_body(j, carry):
        wy_chunks([(hd, d, j * cpi + c) for c in range(cpi) for hd in range(hp) for d in range(2)])
        return carry

    lax.fori_loop(0, n_chunks // cpi, wy_body, 0)

    chains = [(hd, d) for hd in range(hp) for d in range(2)]

    def scan_body(i, carry):
        nc = range(len(chains))
        ns = [i if d == 0 else n_chunks - 1 - i for _, d in chains]
        rows = [chunk_rows(n) for n in ns]
        s16 = [carry[c].astype(BF16) for c in nc]
        ks_ = [_dot(kc_s[hd, d, rows[c], :], s16[c]) for c, (hd, d) in enumerate(chains)]
        qs_ = [_dot(qe_s[hd, d, rows[c], :], s16[c]) for c, (hd, d) in enumerate(chains)]
        u16 = [(u0_s[hd, d, rows[c], :] - ks_[c]).astype(BF16) for c, (hd, d) in enumerate(chains)]
        au = [_dot(at_s[hd, d, rows[c], :], u16[c]) for c, (hd, d) in enumerate(chains)]
        ku = [_dot_tn(kd_s[hd, d, rows[c], :], u16[c]) for c, (hd, d) in enumerate(chains)]
        for c, (hd, d) in enumerate(chains):
            o_s[hd, d, rows[c], :] = qs_[c] + au[c]
        return tuple(carry[c] * jnp.exp(gc_s[hd, d, last_row(d, ns[c]), :]) + ku[c]
                     for c, (hd, d) in enumerate(chains))

    if has_s0:
        init = tuple(s0_ref[d, hd] for hd, d in chains)
    else:
        init = tuple(jnp.zeros((HEAD_DIM, HEAD_DIM), F32) for _ in chains)
    fin = lax.fori_loop(0, n_chunks, scan_body, init)
    if write_state:
        for c, (hd, d) in enumerate(chains):
            st_ref[d, hd] = fin[c]

    for hd in range(hp):
        cols = slice(hd * HEAD_DIM, (hd + 1) * HEAD_DIM)
        y = _rms(o_s[hd, 0] + o_s[hd, 1], gn_ref[...]) * _silu(z_ref[:, cols])
        mix_ref[:, cols] = y.astype(mix_ref.dtype)


def _delta(proj, n_batch, seq, conv_w, alog_row, dtb_row, gn, s0, write_state):
    n_chunks = seq // DN_CHUNK
    hp = max(1, min(DN_H, DN_UNITS // (2 * n_chunks)))
    cpi = DN_UNITS // (2 * hp)
    hw = hp * HEAD_DIM

    def col_spec(col0):
        return pl.BlockSpec((seq, hw), lambda b, h: (b, col0 // hw + h))

    def conv_spec(part):
        return pl.BlockSpec((3, hw), lambda b, h: (0, part * (DN_H // hp) + h))

    vec = pl.BlockSpec((1, LANES), lambda b, h: (0, 0))
    st_spec = pl.BlockSpec((None, 2, hp, HEAD_DIM, HEAD_DIM), lambda b, h: (b, 0, h, 0, 0))
    in_specs = [col_spec(OD_DQ), col_spec(OD_DK), col_spec(OD_DV), col_spec(OD_DZ),
                pl.BlockSpec((seq, LANES), lambda b, h: (b, OD_AB // LANES)),
                conv_spec(0), conv_spec(1), conv_spec(2), vec, vec, vec]
    args = [proj, proj, proj, proj, proj, conv_w, conv_w, conv_w, alog_row, dtb_row, gn]
    if s0 is not None:
        in_specs.append(st_spec)
        args.append(s0)
    mix_spec = pl.BlockSpec((seq, hw), lambda b, h: (b, h))
    mix_shape = jax.ShapeDtypeStruct((n_batch * seq, DN_H * HEAD_DIM), BF16)
    if write_state:
        out_specs = [mix_spec, st_spec]
        out_shape = [mix_shape, jax.ShapeDtypeStruct((n_batch, 2, DN_H, HEAD_DIM, HEAD_DIM), F32)]
    else:
        out_specs, out_shape = mix_spec, mix_shape
    tok = pltpu.VMEM((hp, seq, HEAD_DIM), F32)
    both = pltpu.VMEM((hp, 2, seq, HEAD_DIM), F32)
    both16 = pltpu.VMEM((hp, 2, seq, HEAD_DIM), BF16)
    return pl.pallas_call(
        functools.partial(_delta_kernel, seq=seq, hp=hp, cpi=cpi, has_s0=s0 is not None, write_state=write_state),
        grid=(n_batch, DN_H // hp),
        in_specs=in_specs,
        out_specs=out_specs,
        out_shape=out_shape,
        scratch_shapes=[tok, tok, tok, both, both, both, both, both16, both16, both16, both16],
        compiler_params=_params("parallel", "arbitrary"),
        name="deltanet",
    )(*args)


def _rope_tables(n_tok):
    t = jnp.arange(n_tok)
    inv = 1.0 / (ROPE_THETA ** (jnp.arange(ROPE_FREQ, dtype=F32) / ROPE_FREQ))
    pos = jnp.stack([t // GRID_W, t % GRID_W], axis=1).astype(F32)
    ang = pos[:, :, None] * inv
    cos = jnp.concatenate([jnp.cos(ang), jnp.cos(ang)], axis=-1).reshape(n_tok, HEAD_DIM)
    sin = jnp.concatenate([-jnp.sin(ang), jnp.sin(ang)], axis=-1).reshape(n_tok, HEAD_DIM)
    return cos, sin


def _pad_lanes(v):
    return jnp.pad(v.reshape(1, -1), ((0, 0), (0, LANES - v.size)))


def kernel(x_prompt, x_sample, state_gla, cache_gqa_k, cache_gqa_v, cache_na_k, cache_na_v, state_delta, c, c_ctx, norm1, norm2, w_ada, b_ada, w_mlp1, w_mlp2, ev_w_in, ev_w_a2, ev_b_a2, ev_gla_norm, ev_q_norm, ev_k_norm, ev_w_out, od_w_in, od_conv, od_a_log, od_dt_bias, od_dn_norm, od_rpb, od_w_out, norm_f):
    n_ctx, seq_ctx, d = x_prompt.shape
    n_lat, seq_lat, _ = x_sample.shape
    depth = w_ada.shape[0]

    w_ev = ev_w_in[0]
    w_ev = jnp.concatenate(
        [w_ev[:, :EV_AQ], w_ev[:, EV_AQ + 2 * GLA_RANK:], w_ev[:, EV_AQ:EV_AQ + 2 * GLA_RANK],
         jnp.zeros((d, EV_COLS_PAD - w_ev.shape[1]), F32)], axis=1).astype(BF16)
    w_od = jnp.pad(od_w_in[0], ((0, 0), (0, OD_COLS_PAD - od_w_in.shape[2]))).astype(BF16)
    w_in = [w_ev, w_od]
    w_out = [ev_w_out[0].astype(BF16), od_w_out[0].astype(BF16)]
    w1 = w_mlp1.astype(BF16)
    w2 = w_mlp2.astype(BF16)
    w2pad = jnp.zeros((2, LANES, GLA_H * GLA_DK), F32)
    for dd in range(2):
        w2pad = w2pad.at[dd, dd * GLA_RANK:(dd + 1) * GLA_RANK].set(ev_w_a2[0, dd])
    b2 = ev_b_a2[0].reshape(2, 1, GLA_H * GLA_DK)
    alog_row = _pad_lanes(od_a_log[0])
    dtb_row = _pad_lanes(od_dt_bias[0])
    na_bias = _na_bias_table(od_rpb[0])
    rope = _rope_tables(seq_lat)

    cond8 = jnp.concatenate([c_ctx[None, :], c, jnp.zeros((8 - 1 - n_lat, d), F32)], axis=0)
    mod = _adaln(cond8, w_ada, b_ada).reshape(depth, 8, 6, 1, d)

    def trunk(x, n_batch, seq, row_fn, caches):
        latent = caches is not None
        tm = TM_MLP
        outs = {}
        proj = _inproj(x, norm1[0:1], mod, 0, row_fn, w_in[0], TM_INPROJ)
        if latent:
            s0 = caches["gla"].reshape(n_batch, 2, GLA_H // 2, LANES, GLA_DV)
            mix_a = _gla(proj, n_batch, seq, w2pad, b2, ev_gla_norm, s0, False)
            mix_b = _attention(proj, n_batch, seq, EV_AQ, EV_AK, EV_AV, GQA_KV, GQA_H // GQA_KV,
                               norms=(ev_q_norm, ev_k_norm), rope=rope, ctx=caches["gqa"])
        else:
            mix_a, st = _gla(proj, n_batch, seq, w2pad, b2, ev_gla_norm, None, True)
            mix_b, k_new = _attention(proj, n_batch, seq, EV_AQ, EV_AK, EV_AV, GQA_KV, GQA_H // GQA_KV,
                                      norms=(ev_q_norm, ev_k_norm), write_k=True)
            outs["st_gla"] = st.reshape(n_batch, 1, 2, GLA_H, GLA_DK, GLA_DV)
            outs["ck_gqa"] = k_new.reshape(n_batch, 1, seq, GQA_KV, HEAD_DIM)
            outs["cv_gqa"] = proj[:, EV_AV:EV_AV + GQA_KV * HEAD_DIM].reshape(n_batch, 1, seq, GQA_KV, HEAD_DIM)
        x = _outproj(mix_a, mix_b, w_out[0], x, mod, 0, row_fn, tm)
        x = _mlp(x, norm2[0:1], mod, 0, row_fn, w1, w2, norm_f[None, :], False, tm)
        proj = _inproj(x, norm1[1:2], mod, 1, row_fn, w_in[1], TM_INPROJ)
        if latent:
            mix_a = _neighbourhood(proj, n_batch, seq, caches["na"][0], caches["na"][1], na_bias)
            s0 = caches["delta"].reshape(n_batch, 2, DN_H, HEAD_DIM, HEAD_DIM)
            mix_b = _delta(proj, n_batch, seq, od_conv[0], alog_row, dtb_row, od_dn_norm, s0, False)
        else:
            mix_a = _attention(proj, n_batch, seq, OD_NQ, OD_NK, OD_NV, NA_H, 1)
            mix_b, st = _delta(proj, n_batch, seq, od_conv[0], alog_row, dtb_row, od_dn_norm, None, True)
            outs["ck_na"] = proj[:, OD_NK:OD_NK + NA_H * HEAD_DIM].reshape(n_batch, 1, seq, NA_H, HEAD_DIM)
            outs["cv_na"] = proj[:, OD_NV:OD_NV + NA_H * HEAD_DIM].reshape(n_batch, 1, seq, NA_H, HEAD_DIM)
            outs["st_dn"] = st.reshape(n_batch, 1, 2, DN_H, HEAD_DIM, HEAD_DIM)
        x = _outproj(mix_a, mix_b, w_out[1], x, mod, 1, row_fn, tm)
        y = _mlp(x, norm2[1:2], mod, 1, row_fn, w1, w2, norm_f[None, :], True, tm)
        return y.reshape(n_batch, seq, d), outs

    y_prompt, new = trunk(x_prompt.reshape(n_ctx * seq_ctx, d), n_ctx, seq_ctx, lambda tok: 0, None)
    past = cache_gqa_k.shape[2]
    caches = {
        "gla": state_gla[:, 0],
        "gqa": (cache_gqa_k[:, 0].reshape(n_lat, past, GQA_KV * HEAD_DIM),
                cache_gqa_v[:, 0].reshape(n_lat, past, GQA_KV * HEAD_DIM)),
        "na": (cache_na_k[:, 0].reshape(n_lat, past, NA_H * HEAD_DIM),
               cache_na_v[:, 0].reshape(n_lat, past, NA_H * HEAD_DIM)),
        "delta": state_delta[:, 0],
    }
    y_sample, _ = trunk(x_sample.reshape(n_lat * seq_lat, d), n_lat, seq_lat,
                        lambda tok: 1 + tok // seq_lat, caches)
    return (y_prompt, y_sample, new["st_gla"], new["ck_gqa"], new["cv_gqa"], new["ck_na"], new["cv_na"],
            new["st_dn"])
```

```python
import functools

import jax
import jax.numpy as jnp
import numpy as np
from jax import lax
from jax.experimental import pallas as pl
from jax.experimental.pallas import tpu as pltpu

F32 = jnp.float32
BF16 = jnp.bfloat16
HIGHEST = lax.Precision.HIGHEST

D_MODEL = 2048
D_FF = 4 * D_MODEL
HEAD_DIM = 128
LANES = 128
GRID_W = 64
GLA_H = 8
GLA_DK = 64
GLA_DV = 128
GLA_RANK = 16
GLA_TAU = 16.0
GQA_H = 8
GQA_KV = 2
NA_H = 8
NA_WIN_R = 8
NA_WIN_C = 16
DN_H = 8
CHUNK = 64
DN_CHUNK = 128
DN_REFINE_STEPS = 1
DN_UNITS = 8
TM_INPROJ = 1024
TN_INPROJ = 1280
TM_MLP = 512
TF_MLP = 1024
NA_ROWS_PER_STEP = 4
ROPE_THETA = 10000.0
ROPE_FREQ = HEAD_DIM // 4
EPS = 1e-6
MASK_VALUE = -1e30

EV_Q, EV_K, EV_V, EV_G = 0, 512, 1024, 2048
EV_AQ, EV_AK, EV_AV, EV_LO = 3072, 4096, 4352, 4608
EV_COLS_PAD = 5120
OD_NQ, OD_NK, OD_NV = 0, 1024, 2048
OD_DQ, OD_DK, OD_DV, OD_DZ, OD_AB = 3072, 4096, 5120, 6144, 7168
OD_COLS_PAD = 7680

VMEM_LIMIT = 56 * 1024 * 1024


def _params(*sem):
    return pltpu.CompilerParams(dimension_semantics=sem, vmem_limit_bytes=VMEM_LIMIT)


def _dot(a, b):
    return jnp.dot(a, b, preferred_element_type=F32)


def _dot_nt(a, b):
    return lax.dot_general(a, b, (((1,), (1,)), ((), ())), preferred_element_type=F32)


def _dot_tn(a, b):
    return lax.dot_general(a, b, (((0,), (0,)), ((), ())), preferred_element_type=F32)


def _dot_hi(a, b):
    return jnp.dot(a, b, precision=HIGHEST, preferred_element_type=F32)


def _sigmoid(x):
    return 1.0 / (1.0 + jnp.exp(-x))


def _silu(x):
    return x * _sigmoid(x)


def _softplus(x):
    return jnp.maximum(x, 0.0) + jnp.log(1.0 + jnp.exp(-jnp.abs(x)))


def _rms(x, g):
    return x * lax.rsqrt(jnp.mean(x * x, axis=-1, keepdims=True) + EPS) * g


def _rms_mod(x, g, sc, sh):
    return _rms(x, g) * (1.0 + sc) + sh


def _adaln_kernel(c_ref, w_ref, b_ref, o_ref):
    a = _silu(c_ref[...]).astype(BF16)
    o_ref[...] = _dot(a, w_ref[...].astype(BF16)) + b_ref[...]


def _adaln(cond8, w_ada, b_ada):
    depth, d, n = w_ada.shape
    tn = 512
    return pl.pallas_call(
        _adaln_kernel,
        grid=(depth, n // tn),
        in_specs=[
            pl.BlockSpec((8, d), lambda l, j: (0, 0)),
            pl.BlockSpec((None, d, tn), lambda l, j: (l, 0, j)),
            pl.BlockSpec((None, 1, tn), lambda l, j: (l, 0, j)),
        ],
        out_specs=pl.BlockSpec((None, 8, tn), lambda l, j: (l, 0, j)),
        out_shape=jax.ShapeDtypeStruct((depth, 8, n), F32),
        compiler_params=_params("parallel", "parallel"),
        name="adaln",
    )(cond8, w_ada, b_ada.reshape(depth, 1, n))


def _mod_spec(layer, k, row_fn, tm, tn=D_MODEL):
    return pl.BlockSpec((None, None, None, 1, tn), lambda i, j: (layer, row_fn(i * tm), k, 0, j if tn < D_MODEL else 0))


def _inproj_kernel(x_ref, g_ref, sc_ref, sh_ref, w_ref, o_ref, h_ref):
    @pl.when(pl.program_id(1) == 0)
    def _():
        h_ref[...] = _rms_mod(x_ref[...], g_ref[...], sc_ref[...], sh_ref[...]).astype(BF16)

    o_ref[...] = _dot(h_ref[...], w_ref[...])


def _inproj(x, g, mod, layer, row_fn, w, tm):
    m, d = x.shape
    n = w.shape[1]
    tn = TN_INPROJ
    return pl.pallas_call(
        _inproj_kernel,
        grid=(m // tm, n // tn),
        in_specs=[
            pl.BlockSpec((tm, d), lambda i, j: (i, 0)),
            pl.BlockSpec((1, d), lambda i, j: (0, 0)),
            _mod_spec(layer, 1, row_fn, tm),
            _mod_spec(layer, 0, row_fn, tm),
            pl.BlockSpec((d, tn), lambda i, j: (0, j)),
        ],
        out_specs=pl.BlockSpec((tm, tn), lambda i, j: (i, j)),
        out_shape=jax.ShapeDtypeStruct((m, n), F32),
        scratch_shapes=[pltpu.VMEM((tm, d), BF16)],
        compiler_params=_params("parallel", "arbitrary"),
        name="inproj",
    )(x, g, mod, mod, w)


def _outproj_kernel(a_ref, b_ref, wa_ref, wb_ref, x_ref, gt_ref, o_ref):
    acc = _dot(a_ref[...], wa_ref[...]) + _dot(b_ref[...], wb_ref[...])
    o_ref[...] = x_ref[...] + gt_ref[...] * acc


def _outproj(mix_a, mix_b, w, x, mod, layer, row_fn, tm):
    m, d = x.shape
    ka, kb = mix_a.shape[1], mix_b.shape[1]
    tn = d
    gt_spec = _mod_spec(layer, 2, row_fn, tm, tn)
    return pl.pallas_call(
        _outproj_kernel,
        grid=(m // tm, d // tn),
        in_specs=[
            pl.BlockSpec((tm, ka), lambda i, j: (i, 0)),
            pl.BlockSpec((tm, kb), lambda i, j: (i, 0)),
            pl.BlockSpec((ka, tn), lambda i, j: (0, j)),
            pl.BlockSpec((kb, tn), lambda i, j: (1, j)),
            pl.BlockSpec((tm, tn), lambda i, j: (i, j)),
            gt_spec,
        ],
        out_specs=pl.BlockSpec((tm, tn), lambda i, j: (i, j)),
        out_shape=jax.ShapeDtypeStruct((m, d), F32),
        compiler_params=_params("parallel", "parallel"),
        name="outproj",
    )(mix_a, mix_b, w, w, x, mod)


def _mlp_kernel(x_ref, g_ref, sc_ref, sh_ref, gt_ref, w1_ref, w2_ref, gf_ref, o_ref, h_ref, acc_ref, *, final_norm):
    j = pl.program_id(1)

    @pl.when(j == 0)
    def _():
        h_ref[...] = _rms_mod(x_ref[...], g_ref[...], sc_ref[...], sh_ref[...]).astype(BF16)
        acc_ref[...] = jnp.zeros_like(acc_ref)

    hid = jnp.maximum(_dot(h_ref[...], w1_ref[...]), 0.0)
    acc_ref[...] += _dot((hid * hid).astype(BF16), w2_ref[...])

    @pl.when(j == pl.num_programs(1) - 1)
    def _():
        y = x_ref[...] + gt_ref[...] * acc_ref[...]
        if final_norm:
            y = _rms(y, gf_ref[...])
        o_ref[...] = y


def _mlp(x, g, mod, layer, row_fn, w1, w2, gf, final_norm, tm):
    m, d = x.shape
    ff = w1.shape[2]
    tf = TF_MLP
    return pl.pallas_call(
        functools.partial(_mlp_kernel, final_norm=final_norm),
        grid=(m // tm, ff // tf),
        in_specs=[
            pl.BlockSpec((tm, d), lambda i, j: (i, 0)),
            pl.BlockSpec((1, d), lambda i, j: (0, 0)),
            _mod_spec(layer, 4, row_fn, tm),
            _mod_spec(layer, 3, row_fn, tm),
            _mod_spec(layer, 5, row_fn, tm),
            pl.BlockSpec((None, d, tf), lambda i, j: (layer, 0, j)),
            pl.BlockSpec((None, tf, d), lambda i, j: (layer, j, 0)),
            pl.BlockSpec((1, d), lambda i, j: (0, 0)),
        ],
        out_specs=pl.BlockSpec((tm, d), lambda i, j: (i, 0)),
        out_shape=jax.ShapeDtypeStruct((m, d), F32),
        scratch_shapes=[pltpu.VMEM((tm, d), BF16), pltpu.VMEM((tm, d), F32)],
        compiler_params=_params("parallel", "arbitrary"),
        name="mlp",
    )(x, g, mod, mod, mod, w1, w2, gf)


def _rope(x, cos, sin):
    lane = lax.broadcasted_iota(jnp.int32, x.shape, 1)
    first_half = (lane % (2 * ROPE_FREQ)) < ROPE_FREQ
    partner = jnp.where(first_half, pltpu.roll(x, LANES - ROPE_FREQ, 1), pltpu.roll(x, ROPE_FREQ, 1))
    return x * cos + partner * sin


def _attn_kernel(*refs, group, use_norm, use_rope, has_ctx, write_k):
    it = iter(refs)
    q_ref, k_ref, v_ref = next(it), next(it), next(it)
    qn_ref = kn_ref = cq_ref = sq_ref = ck_ref = sk_ref = kc_ref = vc_ref = knew_ref = None
    if use_norm:
        qn_ref, kn_ref = next(it), next(it)
    if use_rope:
        cq_ref, sq_ref, ck_ref, sk_ref = next(it), next(it), next(it), next(it)
    if has_ctx:
        kc_ref, vc_ref = next(it), next(it)
    o_ref = next(it)
    if write_k:
        knew_ref = next(it)
    kbuf, vbuf = next(it), next(it)
    scale = HEAD_DIM ** -0.5 * float(np.log2(np.e))

    @pl.when(pl.program_id(2) == 0)
    def _():
        k = k_ref[...]
        if use_norm:
            k = _rms(k, kn_ref[...])
        if write_k:
            knew_ref[...] = k
        if use_rope:
            k = _rope(k, ck_ref[...], sk_ref[...])
        kbuf[...] = k.astype(BF16)
        vbuf[...] = v_ref[...].astype(BF16)

    heads = range(group)
    cols = [slice(g * HEAD_DIM, (g + 1) * HEAD_DIM) for g in heads]
    qb = []
    for g in heads:
        q = q_ref[:, cols[g]]
        if use_norm:
            q = _rms(q, qn_ref[...])
        if use_rope:
            q = _rope(q, cq_ref[...], sq_ref[...])
        qb.append((q * scale).astype(BF16))
    s = [_dot_nt(qb[g], kbuf[...]) for g in heads]
    m = [jnp.max(s[g], axis=-1, keepdims=True) for g in heads]
    if has_ctx:
        kc16 = kc_ref[...].astype(BF16)
        s_c = [_dot_nt(qb[g], kc16) for g in heads]
        m = [jnp.maximum(m[g], jnp.max(s_c[g], axis=-1, keepdims=True)) for g in heads]
    p = [jnp.exp2(s[g] - m[g]) for g in heads]
    l = [jnp.sum(p[g], axis=-1, keepdims=True) for g in heads]
    o = [_dot(p[g].astype(BF16), vbuf[...]) for g in heads]
    if has_ctx:
        vc16 = vc_ref[...].astype(BF16)
        p_c = [jnp.exp2(s_c[g] - m[g]) for g in heads]
        l = [l[g] + jnp.sum(p_c[g], axis=-1, keepdims=True) for g in heads]
        o = [o[g] + _dot(p_c[g].astype(BF16), vc16) for g in heads]
    for g in heads:
        o_ref[:, cols[g]] = (o[g] / l[g]).astype(o_ref.dtype)


def _attention(proj, n_batch, seq, q_col, k_col, v_col, n_kv, group, norms=None, rope=None, ctx=None,
               write_k=False):
    tq = min(seq, 256)
    nq = seq // tq
    gw = group * HEAD_DIM
    in_specs = [
        pl.BlockSpec((tq, gw), lambda b, h, i: (b * nq + i, q_col // gw + h)),
        pl.BlockSpec((seq, HEAD_DIM), lambda b, h, i: (b, k_col // HEAD_DIM + h)),
        pl.BlockSpec((seq, HEAD_DIM), lambda b, h, i: (b, v_col // HEAD_DIM + h)),
    ]
    args = [proj, proj, proj]
    vec = pl.BlockSpec((1, HEAD_DIM), lambda b, h, i: (0, 0))
    if norms is not None:
        in_specs += [vec, vec]
        args += [norms[0], norms[1]]
    if rope is not None:
        tab_q = pl.BlockSpec((tq, HEAD_DIM), lambda b, h, i: (i, 0))
        tab_k = pl.BlockSpec((seq, HEAD_DIM), lambda b, h, i: (0, 0))
        in_specs += [tab_q, tab_q, tab_k, tab_k]
        args += [rope[0], rope[1], rope[0], rope[1]]
    if ctx is not None:
        past = ctx[0].shape[1]
        c_spec = pl.BlockSpec((None, past, HEAD_DIM), lambda b, h, i: (b, 0, h))
        in_specs += [c_spec, c_spec]
        args += [ctx[0], ctx[1]]
    o_spec = pl.BlockSpec((tq, gw), lambda b, h, i: (b * nq + i, h))
    o_shape = jax.ShapeDtypeStruct((n_batch * seq, n_kv * gw), BF16)
    if write_k:
        out_specs = [o_spec, pl.BlockSpec((seq, HEAD_DIM), lambda b, h, i: (b, h))]
        out_shape = [o_shape, jax.ShapeDtypeStruct((n_batch * seq, n_kv * HEAD_DIM), F32)]
    else:
        out_specs, out_shape = o_spec, o_shape
    return pl.pallas_call(
        functools.partial(_attn_kernel, group=group, use_norm=norms is not None, use_rope=rope is not None,
                          has_ctx=ctx is not None, write_k=write_k),
        grid=(n_batch, n_kv, nq),
        in_specs=in_specs,
        out_specs=out_specs,
        out_shape=out_shape,
        scratch_shapes=[pltpu.VMEM((seq, HEAD_DIM), BF16), pltpu.VMEM((seq, HEAD_DIM), BF16)],
        compiler_params=_params("parallel", "parallel", "arbitrary"),
        name="attention",
    )(*args)


def _na_row_start(r, rows):
    return jnp.clip(r - NA_WIN_R // 2, 0, rows - NA_WIN_R)


def _na_kernel(q_ref, k_ref, v_ref, kc_ref, vc_ref, bias_ref, o_ref, kb, vb, kcb, vcb, *, rows):
    win = NA_WIN_R * GRID_W
    scale = HEAD_DIM ** -0.5
    kb[...] = k_ref[...].astype(BF16)
    vb[...] = v_ref[...].astype(BF16)
    kcb[...] = kc_ref[...].astype(BF16)
    vcb[...] = vc_ref[...].astype(BF16)

    def body(j, carry):
        rr = [j * NA_ROWS_PER_STEP + i for i in range(NA_ROWS_PER_STEP)]
        rs = [_na_row_start(r, rows) for r in rr]
        k_rows = [pl.ds(pl.multiple_of(s * GRID_W, GRID_W), win) for s in rs]
        q_rows = [pl.ds(pl.multiple_of(r * GRID_W, GRID_W), GRID_W) for r in rr]
        qb = [q_ref[q_rows[i], :].astype(BF16) for i in range(NA_ROWS_PER_STEP)]
        s_w = [_dot_nt(qb[i], kb[k_rows[i], :]) * scale for i in range(NA_ROWS_PER_STEP)]
        s_c = [_dot_nt(qb[i], kcb[...]) * scale for i in range(NA_ROWS_PER_STEP)]
        p_w, p_c, l = [], [], []
        for i in range(NA_ROWS_PER_STEP):
            delta = rr[i] - rs[i]
            bias = jnp.concatenate(
                [bias_ref[2 * jj - delta + NA_WIN_R - 1] for jj in range(NA_WIN_R // 2)], axis=-1)
            sw = s_w[i] + bias
            m = jnp.maximum(jnp.max(sw, axis=-1, keepdims=True), jnp.max(s_c[i], axis=-1, keepdims=True))
            pw = jnp.exp(sw - m)
            pc = jnp.exp(s_c[i] - m)
            l.append(jnp.sum(pw, axis=-1, keepdims=True) + jnp.sum(pc, axis=-1, keepdims=True))
            p_w.append(pw.astype(BF16))
            p_c.append(pc.astype(BF16))
        o_w = [_dot(p_w[i], vb[k_rows[i], :]) for i in range(NA_ROWS_PER_STEP)]
        o_c = [_dot(p_c[i], vcb[...]) for i in range(NA_ROWS_PER_STEP)]
        for i in range(NA_ROWS_PER_STEP):
            o_ref[q_rows[i], :] = ((o_w[i] + o_c[i]) / l[i]).astype(o_ref.dtype)
        return carry

    lax.fori_loop(0, rows // NA_ROWS_PER_STEP, body, 0)


def _na_bias_table(rpb):
    cols = jnp.arange(GRID_W)
    start = jnp.clip(cols - NA_WIN_C // 2, 0, GRID_W - NA_WIN_C)
    inside = (cols[None, :] >= start[:, None]) & (cols[None, :] < start[:, None] + NA_WIN_C)
    rel = cols[None, :] - cols[:, None] + NA_WIN_C - 1
    onehot = (rel[:, :, None] == jnp.arange(2 * NA_WIN_C - 1)).astype(F32)
    picked = jnp.einsum("hrm,ckm->hrck", rpb, onehot, precision=HIGHEST)
    tab = jnp.where(inside[None, None], picked, MASK_VALUE)
    return jnp.concatenate([tab[:, :-1], tab[:, 1:]], axis=-1)


def _neighbourhood(proj, n_batch, seq, k_ctx, v_ctx, bias):
    rows = seq // GRID_W
    past = k_ctx.shape[1]
    c_spec = pl.BlockSpec((None, past, HEAD_DIM), lambda b, h: (b, 0, h))

    def col_spec(col0):
        return pl.BlockSpec((seq, HEAD_DIM), lambda b, h: (b, col0 // HEAD_DIM + h))

    return pl.pallas_call(
        functools.partial(_na_kernel, rows=rows),
        grid=(n_batch, NA_H),
        in_specs=[
            col_spec(OD_NQ), col_spec(OD_NK), col_spec(OD_NV), c_spec, c_spec,
            pl.BlockSpec((None, 2 * NA_WIN_R - 2, GRID_W, 2 * GRID_W), lambda b, h: (h, 0, 0, 0)),
        ],
        out_specs=pl.BlockSpec((seq, HEAD_DIM), lambda b, h: (b, h)),
        out_shape=jax.ShapeDtypeStruct((n_batch * seq, NA_H * HEAD_DIM), BF16),
        scratch_shapes=[pltpu.VMEM((seq, HEAD_DIM), BF16), pltpu.VMEM((seq, HEAD_DIM), BF16),
                        pltpu.VMEM((past, HEAD_DIM), BF16), pltpu.VMEM((past, HEAD_DIM), BF16)],
        compiler_params=_params("parallel", "parallel"),
        name="neighbourhood",
    )(proj, proj, proj, k_ctx, v_ctx, bias)


def _order_masks(n=CHUNK):
    row = lax.broadcasted_iota(jnp.int32, (n, n), 0)
    col = lax.broadcasted_iota(jnp.int32, (n, n), 1)
    return row, col


def _chunk_cumsum(x, chunk, reverse):
    seq = x.shape[0]
    pos = lax.broadcasted_iota(jnp.int32, x.shape, 0) % chunk
    shift = 1
    while shift < chunk:
        if reverse:
            x = x + jnp.where(pos < chunk - shift, pltpu.roll(x, seq - shift, 0), 0.0)
        else:
            x = x + jnp.where(pos >= shift, pltpu.roll(x, shift, 0), 0.0)
        shift *= 2
    return x


def _gla_kernel(*refs, seq, has_s0, write_state):
    it = iter(refs)
    q_ref, k_ref, v_ref, gg_ref, lo_ref, w2_ref, b2_ref, gn_ref = (next(it) for _ in range(8))
    s0_ref = next(it) if has_s0 else None
    mix_ref = next(it)
    st_ref = next(it) if write_state else None
    o_s, bc_s = next(it), next(it)
    n_chunks = seq // CHUNK
    lane = lax.broadcasted_iota(jnp.int32, (1, LANES), 1)
    head_mask = [(lane < GLA_DK).astype(F32), (lane >= GLA_DK).astype(F32)]
    row, col = _order_masks()
    before = [col <= row, col >= row]

    lo = lo_ref[...].astype(BF16)
    for d in range(2):
        z = _dot(lo, w2_ref[d].astype(BF16)) + b2_ref[d]
        bc_s[d] = _chunk_cumsum(-_softplus(-z) * (1.0 / GLA_TAU), CHUNK, d == 1)

    def body(i, carry):
        rows = [pl.ds(pl.multiple_of(n * CHUNK, CHUNK), CHUNK) for n in (i, n_chunks - 1 - i)]
        chains = [(d, p) for d in range(2) for p in range(2)]
        qe, ke, kd, dec = [], [], [], []
        for d in range(2):
            bc = bc_s[d, rows[d], :]
            bl = bc[CHUNK - 1:CHUNK, :] if d == 0 else bc[0:1, :]
            k = k_ref[rows[d], :]
            qe.append(q_ref[rows[d], :] * (GLA_DK ** -0.5) * jnp.exp(bc))
            ke.append((k * jnp.exp(-bc)).astype(BF16))
            kd.append(k * jnp.exp(bl - bc))
            dec.append(jnp.exp(bl))
        cols = [slice(p * GLA_DV, (p + 1) * GLA_DV) for p in range(2)]
        qep = [(qe[d] * head_mask[p]).astype(BF16) for d, p in chains]
        kdp = [(kd[d] * head_mask[p]).astype(BF16) for d, p in chains]
        vp = [v_ref[rows[d], cols[p]].astype(BF16) for d, p in chains]
        sc = [jnp.where(before[d], _dot_nt(qep[c], ke[d]), 0.0).astype(BF16) for c, (d, p) in enumerate(chains)]
        inter = [_dot_nt(qep[c], carry[c].astype(BF16)) for c in range(4)]
        kv = [_dot_tn(vp[c], kdp[c]) for c in range(4)]
        for c, (d, p) in enumerate(chains):
            o_s[d, rows[d], cols[p]] = _dot(sc[c], vp[c]) + inter[c]
        return tuple(carry[c] * dec[d] + kv[c] for c, (d, p) in enumerate(chains))

    init = []
    for d in range(2):
        if has_s0:
            s0t = s0_ref[d].T
            init += [s0t * head_mask[0], s0t * head_mask[1]]
        else:
            init += [jnp.zeros((GLA_DV, LANES), F32), jnp.zeros((GLA_DV, LANES), F32)]
    fin = lax.fori_loop(0, n_chunks, body, tuple(init), unroll=2)
    if write_state:
        for d in range(2):
            st_ref[d] = (fin[2 * d] + fin[2 * d + 1]).T

    for p in range(2):
        cols = slice(p * GLA_DV, (p + 1) * GLA_DV)
        y = _rms(o_s[0, :, cols] + o_s[1, :, cols], gn_ref[...]) * _silu(gg_ref[:, cols])
        mix_ref[:, cols] = y.astype(mix_ref.dtype)


def _gla(proj, n_batch, seq, w2pad, b2, gn, s0, write_state):
    pairs = GLA_H // 2
    pw = 2 * GLA_DV
    st_spec = pl.BlockSpec((None, 2, None, LANES, GLA_DV), lambda b, p: (b, 0, p, 0, 0))
    in_specs = [
        pl.BlockSpec((seq, LANES), lambda b, p: (b, EV_Q // LANES + p)),
        pl.BlockSpec((seq, LANES), lambda b, p: (b, EV_K // LANES + p)),
        pl.BlockSpec((seq, pw), lambda b, p: (b, EV_V // pw + p)),
        pl.BlockSpec((seq, pw), lambda b, p: (b, EV_G // pw + p)),
        pl.BlockSpec((seq, LANES), lambda b, p: (b, EV_LO // LANES)),
        pl.BlockSpec((2, LANES, LANES), lambda b, p: (0, 0, p)),
        pl.BlockSpec((2, 1, LANES), lambda b, p: (0, 0, p)),
        pl.BlockSpec((1, GLA_DV), lambda b, p: (0, 0)),
    ]
    args = [proj, proj, proj, proj, proj, w2pad, b2, gn]
    if s0 is not None:
        in_specs.append(st_spec)
        args.append(s0)
    mix_spec = pl.BlockSpec((seq, pw), lambda b, p: (b, p))
    mix_shape = jax.ShapeDtypeStruct((n_batch * seq, GLA_H * GLA_DV), BF16)
    if write_state:
        out_specs = [mix_spec, st_spec]
        out_shape = [mix_shape, jax.ShapeDtypeStruct((n_batch, 2, pairs, LANES, GLA_DV), F32)]
    else:
        out_specs, out_shape = mix_spec, mix_shape
    return pl.pallas_call(
        functools.partial(_gla_kernel, seq=seq, has_s0=s0 is not None, write_state=write_state),
        grid=(n_batch, pairs),
        in_specs=in_specs,
        out_specs=out_specs,
        out_shape=out_shape,
        scratch_shapes=[pltpu.VMEM((2, seq, pw), F32), pltpu.VMEM((2, seq, LANES), F32)],
        compiler_params=_params("parallel", "parallel"),
        name="gla",
    )(*args)


def _conv_silu(x, w):
    seq = x.shape[0]
    t = lax.broadcasted_iota(jnp.int32, x.shape, 0)
    prev = jnp.where(t == 0, 0.0, pltpu.roll(x, 1, 0))
    nxt = jnp.where(t == seq - 1, 0.0, pltpu.roll(x, seq - 1, 0))
    return _silu(prev * w[0:1, :] + x * w[1:2, :] + nxt * w[2:3, :])


def _hi_lo(x):
    hi = x.astype(BF16)
    return hi, (x - hi.astype(F32)).astype(BF16)


def _split_lanes(a):
    hi, lo = _hi_lo(a)
    return jnp.concatenate([hi, hi, lo], axis=1)


def _split_rows(b):
    hi, lo = _hi_lo(b)
    return jnp.concatenate([hi, lo, hi], axis=0)


def _l2norm(x):
    return x * lax.rsqrt(jnp.sum(x * x, axis=-1, keepdims=True) + EPS)


def _delta_kernel(*refs, seq, hp, cpi, has_s0, write_state):
    it = iter(refs)
    q_ref, k_ref, v_ref, z_ref, gates_ref, wq_ref, wk_ref, wv_ref, gn_ref = (next(it) for _ in range(9))
    s0_ref = next(it) if has_s0 else None
    mix_ref = next(it)
    st_ref = next(it) if write_state else None
    qs, ks, vs, o_s, gc_s, b_s, u0_s, kc_s, at_s, qe_s, kd_s = (next(it) for _ in range(11))
    n_chunks = seq // DN_CHUNK
    lane = lax.broadcasted_iota(jnp.int32, (1, LANES), 1)
    row, col = _order_masks(DN_CHUNK)
    eye = (row == col).astype(F32)
    before = [col <= row, col >= row]
    strict = [col < row, col > row]
    couples = [((row >> (lvl + 1)) == (col >> (lvl + 1))) & ((row >> lvl) != (col >> lvl))
               for lvl in range(DN_CHUNK.bit_length() - 1)]

    gates = gates_ref[...]
    for hd in range(hp):
        cols = slice(hd * HEAD_DIM, (hd + 1) * HEAD_DIM)
        h = pl.program_id(1) * hp + hd
        qs[hd] = _l2norm(_conv_silu(q_ref[:, cols], wq_ref[:, cols])) * (HEAD_DIM ** -0.5)
        ks[hd] = _l2norm(_conv_silu(k_ref[:, cols], wk_ref[:, cols]))
        vs[hd] = _conv_silu(v_ref[:, cols], wv_ref[:, cols])
        for d in range(2):
            sel_g = lane == d * DN_H + h
            sel_b = lane == 2 * DN_H + d * DN_H + h
            gc_s[hd, d] = jnp.broadcast_to(
                jnp.sum(jnp.where(sel_g, gates, 0.0), axis=-1, keepdims=True), (seq, LANES))
            b_s[hd, d] = jnp.broadcast_to(
                jnp.sum(jnp.where(sel_b, gates, 0.0), axis=-1, keepdims=True), (seq, LANES))

    def chunk_rows(n):
        return pl.ds(pl.multiple_of(n * DN_CHUNK, DN_CHUNK), DN_CHUNK)

    def last_row(d, n):
        return pl.ds(n * DN_CHUNK + (DN_CHUNK - 1 if d == 0 else 0), 1)

    def wy_chunks(units):
        nu = range(len(units))
        rows = [chunk_rows(n) for _, _, n in units]
        gc = [gc_s[hd, d, rows[i], :] for i, (hd, d, _) in enumerate(units)]
        bb = [b_s[hd, d, rows[i], :] for i, (hd, d, _) in enumerate(units)]
        k = [ks[hd, rows[i], :] for i, (hd, _, _) in enumerate(units)]
        k16 = [k[i].astype(BF16) for i in nu]
        kbeta = [k[i] * bb[i] for i in nu]
        kk = [_dot_nt(kbeta[i].astype(BF16), k16[i]) for i in nu]
        qk = [_dot_nt(qs[hd, rows[i], :].astype(BF16), k16[i]) for i, (hd, _, _) in enumerate(units)]
        mm = []
        for i, (hd, d, _) in enumerate(units):
            decay = jnp.where(before[d], jnp.exp(jnp.where(before[d], gc[i] - gc[i].T, 0.0)), 0.0)
            mm.append(jnp.where(strict[d], kk[i] * decay, 0.0))
            at_s[hd, d, rows[i], :] = (qk[i] * decay).astype(BF16)
        inv = [eye - jnp.where(couples[0], mm[i], 0.0) for i in nu]
        for lvl in range(1, len(couples)):
            inv16 = [inv[i].astype(BF16) for i in nu]
            c_inv = [_dot(jnp.where(couples[lvl], mm[i], 0.0).astype(BF16), inv16[i]).astype(BF16) for i in nu]
            inv = [inv[i] - _dot(inv16[i], c_inv[i]) for i in nu]
        inv16 = [inv[i].astype(BF16) for i in nu]
        egc = [jnp.exp(gc[i]) for i in nu]
        rhs = [jnp.concatenate([vs[hd, rows[i], :] * bb[i], kbeta[i] * egc[i]], axis=-1)
               for i, (hd, _, _) in enumerate(units)]
        sol = [_dot(inv16[i], rhs[i].astype(BF16)) for i in nu]
        mm_split = [_split_lanes(mm[i]) for i in nu]
        for _ in range(DN_REFINE_STEPS):
            res = [rhs[i] - sol[i] - _dot(mm_split[i], _split_rows(sol[i])) for i in nu]
            sol = [sol[i] + _dot(inv16[i], res[i].astype(BF16)) for i in nu]
        for i, (hd, d, _) in enumerate(units):
            u0_s[hd, d, rows[i], :] = sol[i][:, :HEAD_DIM]
            kc_s[hd, d, rows[i], :] = sol[i][:, HEAD_DIM:].astype(BF16)
            g_last = gc[i][DN_CHUNK - 1:DN_CHUNK, :] if d == 0 else gc[i][0:1, :]
            qe_s[hd, d, rows[i], :] = (qs[hd, rows[i], :] * egc[i]).astype(BF16)
            kd_s[hd, d, rows[i], :] = (k[i] * jnp.exp(g_last - gc[i])).astype(BF16)

    def wy_body(j, carry):
        wy_chunks([(hd, d, j * cpi + c) for c in range(cpi) for hd in range(hp) for d in range(2)])
        return carry

    lax.fori_loop(0, n_chunks // cpi, wy_body, 0)

    chains = [(hd, d) for hd in range(hp) for d in range(2)]

    def scan_body(i, carry):
        nc = range(len(chains))
        ns = [i if d == 0 else n_chunks - 1 - i for _, d in chains]
        rows = [chunk_rows(n) for n in ns]
        s16 = [carry[c].astype(BF16) for c in nc]
        ks_ = [_dot(kc_s[hd, d, rows[c], :], s16[c]) for c, (hd, d) in enumerate(chains)]
        qs_ = [_dot(qe_s[hd, d, rows[c], :], s16[c]) for c, (hd, d) in enumerate(chains)]
        u16 = [(u0_s[hd, d, rows[c], :] - ks_[c]).astype(BF16) for c, (hd, d) in enumerate(chains)]
        au = [_dot(at_s[hd, d, rows[c], :], u16[c]) for c, (hd, d) in enumerate(chains)]
        ku = [_dot_tn(kd_s[hd, d, rows[c], :], u16[c]) for c, (hd, d) in enumerate(chains)]
        for c, (hd, d) in enumerate(chains):
            o_s[hd, d, rows[c], :] = qs_[c] + au[c]
        return tuple(carry[c] * jnp.exp(gc_s[hd, d, last_row(d, ns[c]), :]) + ku[c]
                     for c, (hd, d) in enumerate(chains))

    if has_s0:
        init = tuple(s0_ref[d, hd] for hd, d in chains)
    else:
        init = tuple(jnp.zeros((HEAD_DIM, HEAD_DIM), F32) for _ in chains)
    fin = lax.fori_loop(0, n_chunks, scan_body, init)
    if write_state:
        for c, (hd, d) in enumerate(chains):
            st_ref[d, hd] = fin[c]

    for hd in range(hp):
        cols = slice(hd * HEAD_DIM, (hd + 1) * HEAD_DIM)
        y = _rms(o_s[hd, 0] + o_s[hd, 1], gn_ref[...]) * _silu(z_ref[:, cols])
        mix_ref[:, cols] = y.astype(mix_ref.dtype)


def _dn_gates_kernel(ab_ref, alog_ref, dtb_ref, o_ref):
    ab = ab_ref[...]
    lane = lax.broadcasted_iota(jnp.int32, (1, LANES), 1)
    g = -jnp.exp(alog_ref[...]) * _softplus(ab + dtb_ref[...])
    fwd = _chunk_cumsum(g, DN_CHUNK, False)
    bwd = _chunk_cumsum(g, DN_CHUNK, True)
    o_ref[...] = jnp.where(lane < DN_H, fwd, jnp.where(lane < 2 * DN_H, bwd, _sigmoid(ab)))


def _dn_gates(proj, n_batch, seq, alog_row, dtb_row):
    vec = pl.BlockSpec((1, LANES), lambda b: (0, 0))
    return pl.pallas_call(
        _dn_gates_kernel,
        grid=(n_batch,),
        in_specs=[pl.BlockSpec((seq, LANES), lambda b: (b, OD_AB // LANES)), vec, vec],
        out_specs=pl.BlockSpec((seq, LANES), lambda b: (b, 0)),
        out_shape=jax.ShapeDtypeStruct((n_batch * seq, LANES), F32),
        compiler_params=_params("parallel"),
        name="dn_gates",
    )(proj, alog_row, dtb_row)


def _delta(proj, gates, n_batch, seq, conv_w, gn, s0, write_state):
    n_chunks = seq // DN_CHUNK
    hp = max(1, min(DN_H, DN_UNITS // (2 * n_chunks)))
    cpi = DN_UNITS // (2 * hp)
    hw = hp * HEAD_DIM

    def col_spec(col0):
        return pl.BlockSpec((seq, hw), lambda b, h: (b, col0 // hw + h))

    def conv_spec(part):
        return pl.BlockSpec((3, hw), lambda b, h: (0, part * (DN_H // hp) + h))

    vec = pl.BlockSpec((1, LANES), lambda b, h: (0, 0))
    st_spec = pl.BlockSpec((None, 2, hp, HEAD_DIM, HEAD_DIM), lambda b, h: (b, 0, h, 0, 0))
    in_specs = [col_spec(OD_DQ), col_spec(OD_DK), col_spec(OD_DV), col_spec(OD_DZ),
                pl.BlockSpec((seq, LANES), lambda b, h: (b, 0)),
                conv_spec(0), conv_spec(1), conv_spec(2), vec]
    args = [proj, proj, proj, proj, gates, conv_w, conv_w, conv_w, gn]
    if s0 is not None:
        in_specs.append(st_spec)
        args.append(s0)
    mix_spec = pl.BlockSpec((seq, hw), lambda b, h: (b, h))
    mix_shape = jax.ShapeDtypeStruct((n_batch * seq, DN_H * HEAD_DIM), BF16)
    if write_state:
        out_specs = [mix_spec, st_spec]
        out_shape = [mix_shape, jax.ShapeDtypeStruct((n_batch, 2, DN_H, HEAD_DIM, HEAD_DIM), F32)]
    else:
        out_specs, out_shape = mix_spec, mix_shape
    tok = pltpu.VMEM((hp, seq, HEAD_DIM), F32)
    both = pltpu.VMEM((hp, 2, seq, HEAD_DIM), F32)
    both16 = pltpu.VMEM((hp, 2, seq, HEAD_DIM), BF16)
    return pl.pallas_call(
        functools.partial(_delta_kernel, seq=seq, hp=hp, cpi=cpi, has_s0=s0 is not None, write_state=write_state),
        grid=(n_batch, DN_H // hp),
        in_specs=in_specs,
        out_specs=out_specs,
        out_shape=out_shape,
        scratch_shapes=[tok, tok, tok, both, both, both, both, both16, both16, both16, both16],
        compiler_params=_params("parallel", "arbitrary"),
        name="deltanet",
    )(*args)


def _rope_tables(n_tok):
    t = jnp.arange(n_tok)
    inv = 1.0 / (ROPE_THETA ** (jnp.arange(ROPE_FREQ, dtype=F32) / ROPE_FREQ))
    pos = jnp.stack([t // GRID_W, t % GRID_W], axis=1).astype(F32)
    ang = pos[:, :, None] * inv
    cos = jnp.concatenate([jnp.cos(ang), jnp.cos(ang)], axis=-1).reshape(n_tok, HEAD_DIM)
    sin = jnp.concatenate([-jnp.sin(ang), jnp.sin(ang)], axis=-1).reshape(n_tok, HEAD_DIM)
    return cos, sin


def _pad_lanes(v):
    return jnp.pad(v.reshape(1, -1), ((0, 0), (0, LANES - v.size)))


def kernel(x_prompt, x_sample, state_gla, cache_gqa_k, cache_gqa_v, cache_na_k, cache_na_v, state_delta, c, c_ctx, norm1, norm2, w_ada, b_ada, w_mlp1, w_mlp2, ev_w_in, ev_w_a2, ev_b_a2, ev_gla_norm, ev_q_norm, ev_k_norm, ev_w_out, od_w_in, od_conv, od_a_log, od_dt_bias, od_dn_norm, od_rpb, od_w_out, norm_f):
    n_ctx, seq_ctx, d = x_prompt.shape
    n_lat, seq_lat, _ = x_sample.shape
    depth = w_ada.shape[0]

    w_ev = ev_w_in[0]
    w_ev = jnp.concatenate(
        [w_ev[:, :EV_AQ], w_ev[:, EV_AQ + 2 * GLA_RANK:], w_ev[:, EV_AQ:EV_AQ + 2 * GLA_RANK],
         jnp.zeros((d, EV_COLS_PAD - w_ev.shape[1]), F32)], axis=1).astype(BF16)
    w_od = jnp.pad(od_w_in[0], ((0, 0), (0, OD_COLS_PAD - od_w_in.shape[2]))).astype(BF16)
    w_in = [w_ev, w_od]
    w_out = [ev_w_out[0].astype(BF16), od_w_out[0].astype(BF16)]
    w1 = w_mlp1.astype(BF16)
    w2 = w_mlp2.astype(BF16)
    w2pad = jnp.zeros((2, LANES, GLA_H * GLA_DK), F32)
    for dd in range(2):
        w2pad = w2pad.at[dd, dd * GLA_RANK:(dd + 1) * GLA_RANK].set(ev_w_a2[0, dd])
    b2 = ev_b_a2[0].reshape(2, 1, GLA_H * GLA_DK)
    alog_row = _pad_lanes(od_a_log[0])
    dtb_row = _pad_lanes(od_dt_bias[0])
    na_bias = _na_bias_table(od_rpb[0])
    rope = _rope_tables(seq_lat)

    cond8 = jnp.concatenate([c_ctx[None, :], c, jnp.zeros((8 - 1 - n_lat, d), F32)], axis=0)
    mod = _adaln(cond8, w_ada, b_ada).reshape(depth, 8, 6, 1, d)

    def trunk(x, n_batch, seq, row_fn, caches):
        latent = caches is not None
        tm = TM_MLP
        outs = {}
        proj = _inproj(x, norm1[0:1], mod, 0, row_fn, w_in[0], TM_INPROJ)
        if latent:
            s0 = caches["gla"].reshape(n_batch, 2, GLA_H // 2, LANES, GLA_DV)
            mix_a = _gla(proj, n_batch, seq, w2pad, b2, ev_gla_norm, s0, False)
            mix_b = _attention(proj, n_batch, seq, EV_AQ, EV_AK, EV_AV, GQA_KV, GQA_H // GQA_KV,
                               norms=(ev_q_norm, ev_k_norm), rope=rope, ctx=caches["gqa"])
        else:
            mix_a, st = _gla(proj, n_batch, seq, w2pad, b2, ev_gla_norm, None, True)
            mix_b, k_new = _attention(proj, n_batch, seq, EV_AQ, EV_AK, EV_AV, GQA_KV, GQA_H // GQA_KV,
                                      norms=(ev_q_norm, ev_k_norm), write_k=True)
            outs["st_gla"] = st.reshape(n_batch, 1, 2, GLA_H, GLA_DK, GLA_DV)
            outs["ck_gqa"] = k_new.reshape(n_batch, 1, seq, GQA_KV, HEAD_DIM)
            outs["cv_gqa"] = proj[:, EV_AV:EV_AV + GQA_KV * HEAD_DIM].reshape(n_batch, 1, seq, GQA_KV, HEAD_DIM)
        x = _outproj(mix_a, mix_b, w_out[0], x, mod, 0, row_fn, tm)
        x = _mlp(x, norm2[0:1], mod, 0, row_fn, w1, w2, norm_f[None, :], False, tm)
        proj = _inproj(x, norm1[1:2], mod, 1, row_fn, w_in[1], TM_INPROJ)
        if latent:
            mix_a = _neighbourhood(proj, n_batch, seq, caches["na"][0], caches["na"][1], na_bias)
            s0 = caches["delta"].reshape(n_batch, 2, DN_H, HEAD_DIM, HEAD_DIM)
            gates = _dn_gates(proj, n_batch, seq, alog_row, dtb_row)
            mix_b = _delta(proj, gates, n_batch, seq, od_conv[0], od_dn_norm, s0, False)
        else:
            mix_a = _attention(proj, n_batch, seq, OD_NQ, OD_NK, OD_NV, NA_H, 1)
            gates = _dn_gates(proj, n_batch, seq, alog_row, dtb_row)
            mix_b, st = _delta(proj, gates, n_batch, seq, od_conv[0], od_dn_norm, None, True)
            outs["ck_na"] = proj[:, OD_NK:OD_NK + NA_H * HEAD_DIM].reshape(n_batch, 1, seq, NA_H, HEAD_DIM)
            outs["cv_na"] = proj[:, OD_NV:OD_NV + NA_H * HEAD_DIM].reshape(n_batch, 1, seq, NA_H, HEAD_DIM)
            outs["st_dn"] = st.reshape(n_batch, 1, 2, DN_H, HEAD_DIM, HEAD_DIM)
        x = _outproj(mix_a, mix_b, w_out[1], x, mod, 1, row_fn, tm)
        y = _mlp(x, norm2[1:2], mod, 1, row_fn, w1, w2, norm_f[None, :], True, tm)
        return y.reshape(n_batch, seq, d), outs

    y_prompt, new = trunk(x_prompt.reshape(n_ctx * seq_ctx, d), n_ctx, seq_ctx, lambda tok: 0, None)
    past = cache_gqa_k.shape[2]
    caches = {
        "gla": state_gla[:, 0],
        "gqa": (cache_gqa_k[:, 0].reshape(n_lat, past, GQA_KV * HEAD_DIM),
                cache_gqa_v[:, 0].reshape(n_lat, past, GQA_KV * HEAD_DIM)),
        "na": (cache_na_k[:, 0].reshape(n_lat, past, NA_H * HEAD_DIM),
               cache_na_v[:, 0].reshape(n_lat, past, NA_H * HEAD_DIM)),
        "delta": state_delta[:, 0],
    }
    y_sample, _ = trunk(x_sample.reshape(n_lat * seq_lat, d), n_lat, seq_lat,
                        lambda tok: 1 + tok // seq_lat, caches)
    return (y_prompt, y_sample, new["st_gla"], new["ck_gqa"], new["cv_gqa"], new["ck_na"], new["cv_na"],
            new["st_dn"])
```

```python
import functools

import jax
import jax.numpy as jnp
import numpy as np
from jax import lax
from jax.experimental import pallas as pl
from jax.experimental.pallas import tpu as pltpu

F32 = jnp.float32
BF16 = jnp.bfloat16
HIGHEST = lax.Precision.HIGHEST

D_MODEL = 2048
D_FF = 4 * D_MODEL
HEAD_DIM = 128
LANES = 128
GRID_W = 64
GLA_H = 8
GLA_DK = 64
GLA_DV = 128
GLA_RANK = 16
GLA_TAU = 16.0
GQA_H = 8
GQA_KV = 2
NA_H = 8
NA_WIN_R = 8
NA_WIN_C = 16
DN_H = 8
CHUNK = 64
DN_CHUNK = 128
DN_REFINE_STEPS = 1
DN_UNITS = 8
TM_INPROJ = 1024
TN_INPROJ = 1280
TM_MLP = 512
TF_MLP = 1024
GLA_STEPS = 2
NA_ROWS_PER_STEP = 4
ROPE_THETA = 10000.0
ROPE_FREQ = HEAD_DIM // 4
EPS = 1e-6
MASK_VALUE = -1e30

EV_Q, EV_K, EV_V, EV_G = 0, 512, 1024, 2048
EV_AQ, EV_AK, EV_AV, EV_LO = 3072, 4096, 4352, 4608
EV_COLS_PAD = 5120
OD_NQ, OD_NK, OD_NV = 0, 1024, 2048
OD_DQ, OD_DK, OD_DV, OD_DZ, OD_AB = 3072, 4096, 5120, 6144, 7168
OD_COLS_PAD = 7680

VMEM_LIMIT = 56 * 1024 * 1024


def _params(*sem):
    return pltpu.CompilerParams(dimension_semantics=sem, vmem_limit_bytes=VMEM_LIMIT)


def _dot(a, b):
    return jnp.dot(a, b, preferred_element_type=F32)


def _dot_nt(a, b):
    return lax.dot_general(a, b, (((1,), (1,)), ((), ())), preferred_element_type=F32)


def _dot_tn(a, b):
    return lax.dot_general(a, b, (((0,), (0,)), ((), ())), preferred_element_type=F32)


def _dot_hi(a, b):
    return jnp.dot(a, b, precision=HIGHEST, preferred_element_type=F32)


def _sigmoid(x):
    return 1.0 / (1.0 + jnp.exp(-x))


def _silu(x):
    return x * _sigmoid(x)


def _softplus(x):
    return jnp.maximum(x, 0.0) + jnp.log(1.0 + jnp.exp(-jnp.abs(x)))


def _rms(x, g):
    return x * lax.rsqrt(jnp.mean(x * x, axis=-1, keepdims=True) + EPS) * g


def _rms_mod(x, g, sc, sh):
    return _rms(x, g) * (1.0 + sc) + sh


def _adaln_kernel(c_ref, w_ref, b_ref, o_ref):
    a = _silu(c_ref[...]).astype(BF16)
    o_ref[...] = _dot(a, w_ref[...].astype(BF16)) + b_ref[...]


def _adaln(cond8, w_ada, b_ada):
    depth, d, n = w_ada.shape
    tn = 512
    return pl.pallas_call(
        _adaln_kernel,
        grid=(depth, n // tn),
        in_specs=[
            pl.BlockSpec((8, d), lambda l, j: (0, 0)),
            pl.BlockSpec((None, d, tn), lambda l, j: (l, 0, j)),
            pl.BlockSpec((None, 1, tn), lambda l, j: (l, 0, j)),
        ],
        out_specs=pl.BlockSpec((None, 8, tn), lambda l, j: (l, 0, j)),
        out_shape=jax.ShapeDtypeStruct((depth, 8, n), F32),
        compiler_params=_params("parallel", "parallel"),
        name="adaln",
    )(cond8, w_ada, b_ada.reshape(depth, 1, n))


def _mod_spec(layer, k, row_fn, tm, tn=D_MODEL):
    return pl.BlockSpec((None, None, None, 1, tn), lambda i, j: (layer, row_fn(i * tm), k, 0, j if tn < D_MODEL else 0))


def _inproj_kernel(x_ref, g_ref, sc_ref, sh_ref, w_ref, o_ref, h_ref):
    @pl.when(pl.program_id(1) == 0)
    def _():
        h_ref[...] = _rms_mod(x_ref[...], g_ref[...], sc_ref[...], sh_ref[...]).astype(BF16)

    o_ref[...] = _dot(h_ref[...], w_ref[...])


def _inproj(x, g, mod, layer, row_fn, w, tm):
    m, d = x.shape
    n = w.shape[1]
    tn = TN_INPROJ
    return pl.pallas_call(
        _inproj_kernel,
        grid=(m // tm, n // tn),
        in_specs=[
            pl.BlockSpec((tm, d), lambda i, j: (i, 0)),
            pl.BlockSpec((1, d), lambda i, j: (0, 0)),
            _mod_spec(layer, 1, row_fn, tm),
            _mod_spec(layer, 0, row_fn, tm),
            pl.BlockSpec((d, tn), lambda i, j: (0, j)),
        ],
        out_specs=pl.BlockSpec((tm, tn), lambda i, j: (i, j)),
        out_shape=jax.ShapeDtypeStruct((m, n), F32),
        scratch_shapes=[pltpu.VMEM((tm, d), BF16)],
        compiler_params=_params("parallel", "arbitrary"),
        name="inproj",
    )(x, g, mod, mod, w)


def _outproj_kernel(a_ref, b_ref, wa_ref, wb_ref, x_ref, gt_ref, o_ref):
    acc = _dot(a_ref[...], wa_ref[...]) + _dot(b_ref[...], wb_ref[...])
    o_ref[...] = x_ref[...] + gt_ref[...] * acc


def _outproj(mix_a, mix_b, w, x, mod, layer, row_fn, tm):
    m, d = x.shape
    ka, kb = mix_a.shape[1], mix_b.shape[1]
    tn = d
    gt_spec = _mod_spec(layer, 2, row_fn, tm, tn)
    return pl.pallas_call(
        _outproj_kernel,
        grid=(m // tm, d // tn),
        in_specs=[
            pl.BlockSpec((tm, ka), lambda i, j: (i, 0)),
            pl.BlockSpec((tm, kb), lambda i, j: (i, 0)),
            pl.BlockSpec((ka, tn), lambda i, j: (0, j)),
            pl.BlockSpec((kb, tn), lambda i, j: (1, j)),
            pl.BlockSpec((tm, tn), lambda i, j: (i, j)),
            gt_spec,
        ],
        out_specs=pl.BlockSpec((tm, tn), lambda i, j: (i, j)),
        out_shape=jax.ShapeDtypeStruct((m, d), F32),
        compiler_params=_params("parallel", "parallel"),
        name="outproj",
    )(mix_a, mix_b, w, w, x, mod)


def _mlp_kernel(x_ref, g_ref, sc_ref, sh_ref, gt_ref, w1_ref, w2_ref, gf_ref, o_ref, h_ref, acc_ref, *, final_norm):
    j = pl.program_id(1)

    @pl.when(j == 0)
    def _():
        h_ref[...] = _rms_mod(x_ref[...], g_ref[...], sc_ref[...], sh_ref[...]).astype(BF16)
        acc_ref[...] = jnp.zeros_like(acc_ref)

    hid = jnp.maximum(_dot(h_ref[...], w1_ref[...]), 0.0)
    acc_ref[...] += _dot((hid * hid).astype(BF16), w2_ref[...])

    @pl.when(j == pl.num_programs(1) - 1)
    def _():
        y = x_ref[...] + gt_ref[...] * acc_ref[...]
        if final_norm:
            y = _rms(y, gf_ref[...])
        o_ref[...] = y


def _mlp(x, g, mod, layer, row_fn, w1, w2, gf, final_norm, tm):
    m, d = x.shape
    ff = w1.shape[2]
    tf = TF_MLP
    return pl.pallas_call(
        functools.partial(_mlp_kernel, final_norm=final_norm),
        grid=(m // tm, ff // tf),
        in_specs=[
            pl.BlockSpec((tm, d), lambda i, j: (i, 0)),
            pl.BlockSpec((1, d), lambda i, j: (0, 0)),
            _mod_spec(layer, 4, row_fn, tm),
            _mod_spec(layer, 3, row_fn, tm),
            _mod_spec(layer, 5, row_fn, tm),
            pl.BlockSpec((None, d, tf), lambda i, j: (layer, 0, j)),
            pl.BlockSpec((None, tf, d), lambda i, j: (layer, j, 0)),
            pl.BlockSpec((1, d), lambda i, j: (0, 0)),
        ],
        out_specs=pl.BlockSpec((tm, d), lambda i, j: (i, 0)),
        out_shape=jax.ShapeDtypeStruct((m, d), F32),
        scratch_shapes=[pltpu.VMEM((tm, d), BF16), pltpu.VMEM((tm, d), F32)],
        compiler_params=_params("parallel", "arbitrary"),
        name="mlp",
    )(x, g, mod, mod, mod, w1, w2, gf)


def _rope(x, cos, sin):
    lane = lax.broadcasted_iota(jnp.int32, x.shape, 1)
    first_half = (lane % (2 * ROPE_FREQ)) < ROPE_FREQ
    partner = jnp.where(first_half, pltpu.roll(x, LANES - ROPE_FREQ, 1), pltpu.roll(x, ROPE_FREQ, 1))
    return x * cos + partner * sin


def _attn_kernel(*refs, group, use_norm, use_rope, has_ctx, write_k):
    it = iter(refs)
    q_ref, k_ref, v_ref = next(it), next(it), next(it)
    qn_ref = kn_ref = cq_ref = sq_ref = ck_ref = sk_ref = kc_ref = vc_ref = knew_ref = None
    if use_norm:
        qn_ref, kn_ref = next(it), next(it)
    if use_rope:
        cq_ref, sq_ref, ck_ref, sk_ref = next(it), next(it), next(it), next(it)
    if has_ctx:
        kc_ref, vc_ref = next(it), next(it)
    o_ref = next(it)
    if write_k:
        knew_ref = next(it)
    kbuf, vbuf = next(it), next(it)
    scale = HEAD_DIM ** -0.5 * float(np.log2(np.e))

    @pl.when(pl.program_id(2) == 0)
    def _():
        k = k_ref[...]
        if use_norm:
            k = _rms(k, kn_ref[...])
        if write_k:
            knew_ref[...] = k
        if use_rope:
            k = _rope(k, ck_ref[...], sk_ref[...])
        kbuf[...] = k.astype(BF16)
        vbuf[...] = v_ref[...].astype(BF16)

    heads = range(group)
    cols = [slice(g * HEAD_DIM, (g + 1) * HEAD_DIM) for g in heads]
    qb = []
    for g in heads:
        q = q_ref[:, cols[g]]
        if use_norm:
            q = _rms(q, qn_ref[...])
        if use_rope:
            q = _rope(q, cq_ref[...], sq_ref[...])
        qb.append((q * scale).astype(BF16))
    s = [_dot_nt(qb[g], kbuf[...]) for g in heads]
    m = [jnp.max(s[g], axis=-1, keepdims=True) for g in heads]
    if has_ctx:
        kc16 = kc_ref[...].astype(BF16)
        s_c = [_dot_nt(qb[g], kc16) for g in heads]
        m = [jnp.maximum(m[g], jnp.max(s_c[g], axis=-1, keepdims=True)) for g in heads]
    p = [jnp.exp2(s[g] - m[g]) for g in heads]
    l = [jnp.sum(p[g], axis=-1, keepdims=True) for g in heads]
    o = [_dot(p[g].astype(BF16), vbuf[...]) for g in heads]
    if has_ctx:
        vc16 = vc_ref[...].astype(BF16)
        p_c = [jnp.exp2(s_c[g] - m[g]) for g in heads]
        l = [l[g] + jnp.sum(p_c[g], axis=-1, keepdims=True) for g in heads]
        o = [o[g] + _dot(p_c[g].astype(BF16), vc16) for g in heads]
    for g in heads:
        o_ref[:, cols[g]] = (o[g] / l[g]).astype(o_ref.dtype)


def _attention(proj, n_batch, seq, q_col, k_col, v_col, n_kv, group, norms=None, rope=None, ctx=None,
               write_k=False):
    tq = min(seq, 256)
    nq = seq // tq
    gw = group * HEAD_DIM
    in_specs = [
        pl.BlockSpec((tq, gw), lambda b, h, i: (b * nq + i, q_col // gw + h)),
        pl.BlockSpec((seq, HEAD_DIM), lambda b, h, i: (b, k_col // HEAD_DIM + h)),
        pl.BlockSpec((seq, HEAD_DIM), lambda b, h, i: (b, v_col // HEAD_DIM + h)),
    ]
    args = [proj, proj, proj]
    vec = pl.BlockSpec((1, HEAD_DIM), lambda b, h, i: (0, 0))
    if norms is not None:
        in_specs += [vec, vec]
        args += [norms[0], norms[1]]
    if rope is not None:
        tab_q = pl.BlockSpec((tq, HEAD_DIM), lambda b, h, i: (i, 0))
        tab_k = pl.BlockSpec((seq, HEAD_DIM), lambda b, h, i: (0, 0))
        in_specs += [tab_q, tab_q, tab_k, tab_k]
        args += [rope[0], rope[1], rope[0], rope[1]]
    if ctx is not None:
        past = ctx[0].shape[1]
        c_spec = pl.BlockSpec((None, past, HEAD_DIM), lambda b, h, i: (b, 0, h))
        in_specs += [c_spec, c_spec]
        args += [ctx[0], ctx[1]]
    o_spec = pl.BlockSpec((tq, gw), lambda b, h, i: (b * nq + i, h))
    o_shape = jax.ShapeDtypeStruct((n_batch * seq, n_kv * gw), BF16)
    if write_k:
        out_specs = [o_spec, pl.BlockSpec((seq, HEAD_DIM), lambda b, h, i: (b, h))]
        out_shape = [o_shape, jax.ShapeDtypeStruct((n_batch * seq, n_kv * HEAD_DIM), F32)]
    else:
        out_specs, out_shape = o_spec, o_shape
    return pl.pallas_call(
        functools.partial(_attn_kernel, group=group, use_norm=norms is not None, use_rope=rope is not None,
                          has_ctx=ctx is not None, write_k=write_k),
        grid=(n_batch, n_kv, nq),
        in_specs=in_specs,
        out_specs=out_specs,
        out_shape=out_shape,
        scratch_shapes=[pltpu.VMEM((seq, HEAD_DIM), BF16), pltpu.VMEM((seq, HEAD_DIM), BF16)],
        compiler_params=_params("parallel", "parallel", "arbitrary"),
        name="attention",
    )(*args)


def _na_row_start(r, rows):
    return jnp.clip(r - NA_WIN_R // 2, 0, rows - NA_WIN_R)


def _na_kernel(q_ref, k_ref, v_ref, kc_ref, vc_ref, bias_ref, o_ref, kb, vb, kcb, vcb, *, rows):
    win = NA_WIN_R * GRID_W
    scale = HEAD_DIM ** -0.5
    kb[...] = k_ref[...].astype(BF16)
    vb[...] = v_ref[...].astype(BF16)
    kcb[...] = kc_ref[...].astype(BF16)
    vcb[...] = vc_ref[...].astype(BF16)

    def body(j, carry):
        rr = [j * NA_ROWS_PER_STEP + i for i in range(NA_ROWS_PER_STEP)]
        rs = [_na_row_start(r, rows) for r in rr]
        k_rows = [pl.ds(pl.multiple_of(s * GRID_W, GRID_W), win) for s in rs]
        q_rows = [pl.ds(pl.multiple_of(r * GRID_W, GRID_W), GRID_W) for r in rr]
        qb = [q_ref[q_rows[i], :].astype(BF16) for i in range(NA_ROWS_PER_STEP)]
        s_w = [_dot_nt(qb[i], kb[k_rows[i], :]) * scale for i in range(NA_ROWS_PER_STEP)]
        s_c = [_dot_nt(qb[i], kcb[...]) * scale for i in range(NA_ROWS_PER_STEP)]
        p_w, p_c, l = [], [], []
        for i in range(NA_ROWS_PER_STEP):
            delta = rr[i] - rs[i]
            bias = jnp.concatenate(
                [bias_ref[2 * jj - delta + NA_WIN_R - 1] for jj in range(NA_WIN_R // 2)], axis=-1)
            sw = s_w[i] + bias
            m = jnp.maximum(jnp.max(sw, axis=-1, keepdims=True), jnp.max(s_c[i], axis=-1, keepdims=True))
            pw = jnp.exp(sw - m)
            pc = jnp.exp(s_c[i] - m)
            l.append(jnp.sum(pw, axis=-1, keepdims=True) + jnp.sum(pc, axis=-1, keepdims=True))
            p_w.append(pw.astype(BF16))
            p_c.append(pc.astype(BF16))
        o_w = [_dot(p_w[i], vb[k_rows[i], :]) for i in range(NA_ROWS_PER_STEP)]
        o_c = [_dot(p_c[i], vcb[...]) for i in range(NA_ROWS_PER_STEP)]
        for i in range(NA_ROWS_PER_STEP):
            o_ref[q_rows[i], :] = ((o_w[i] + o_c[i]) / l[i]).astype(o_ref.dtype)
        return carry

    lax.fori_loop(0, rows // NA_ROWS_PER_STEP, body, 0)


def _na_bias_table(rpb):
    cols = jnp.arange(GRID_W)
    start = jnp.clip(cols - NA_WIN_C // 2, 0, GRID_W - NA_WIN_C)
    inside = (cols[None, :] >= start[:, None]) & (cols[None, :] < start[:, None] + NA_WIN_C)
    rel = cols[None, :] - cols[:, None] + NA_WIN_C - 1
    onehot = (rel[:, :, None] == jnp.arange(2 * NA_WIN_C - 1)).astype(F32)
    picked = jnp.einsum("hrm,ckm->hrck", rpb, onehot, precision=HIGHEST)
    tab = jnp.where(inside[None, None], picked, MASK_VALUE)
    return jnp.concatenate([tab[:, :-1], tab[:, 1:]], axis=-1)


def _neighbourhood(proj, n_batch, seq, k_ctx, v_ctx, bias):
    rows = seq // GRID_W
    past = k_ctx.shape[1]
    c_spec = pl.BlockSpec((None, past, HEAD_DIM), lambda b, h: (b, 0, h))

    def col_spec(col0):
        return pl.BlockSpec((seq, HEAD_DIM), lambda b, h: (b, col0 // HEAD_DIM + h))

    return pl.pallas_call(
        functools.partial(_na_kernel, rows=rows),
        grid=(n_batch, NA_H),
        in_specs=[
            col_spec(OD_NQ), col_spec(OD_NK), col_spec(OD_NV), c_spec, c_spec,
            pl.BlockSpec((None, 2 * NA_WIN_R - 2, GRID_W, 2 * GRID_W), lambda b, h: (h, 0, 0, 0)),
        ],
        out_specs=pl.BlockSpec((seq, HEAD_DIM), lambda b, h: (b, h)),
        out_shape=jax.ShapeDtypeStruct((n_batch * seq, NA_H * HEAD_DIM), BF16),
        scratch_shapes=[pltpu.VMEM((seq, HEAD_DIM), BF16), pltpu.VMEM((seq, HEAD_DIM), BF16),
                        pltpu.VMEM((past, HEAD_DIM), BF16), pltpu.VMEM((past, HEAD_DIM), BF16)],
        compiler_params=_params("parallel", "parallel"),
        name="neighbourhood",
    )(proj, proj, proj, k_ctx, v_ctx, bias)


def _order_masks(n=CHUNK):
    row = lax.broadcasted_iota(jnp.int32, (n, n), 0)
    col = lax.broadcasted_iota(jnp.int32, (n, n), 1)
    return row, col


def _chunk_cumsum(x, chunk, reverse):
    seq = x.shape[0]
    pos = lax.broadcasted_iota(jnp.int32, x.shape, 0) % chunk
    shift = 1
    while shift < chunk:
        if reverse:
            x = x + jnp.where(pos < chunk - shift, pltpu.roll(x, seq - shift, 0), 0.0)
        else:
            x = x + jnp.where(pos >= shift, pltpu.roll(x, shift, 0), 0.0)
        shift *= 2
    return x


def _gla_kernel(*refs, seq, has_s0, write_state):
    it = iter(refs)
    q_ref, k_ref, v_ref, gg_ref, lo_ref, w2_ref, b2_ref, gn_ref = (next(it) for _ in range(8))
    s0_ref = next(it) if has_s0 else None
    mix_ref = next(it)
    st_ref = next(it) if write_state else None
    o_s, bc_s = next(it), next(it)
    n_chunks = seq // CHUNK
    lane = lax.broadcasted_iota(jnp.int32, (1, LANES), 1)
    head_mask = [(lane < GLA_DK).astype(F32), (lane >= GLA_DK).astype(F32)]
    row, col = _order_masks()
    before = [col <= row, col >= row]

    lo = lo_ref[...].astype(BF16)
    for d in range(2):
        z = _dot(lo, w2_ref[d].astype(BF16)) + b2_ref[d]
        bc_s[d] = _chunk_cumsum(-_softplus(-z) * (1.0 / GLA_TAU), CHUNK, d == 1)

    chains = [(d, p) for d in range(2) for p in range(2)]
    cols = [slice(p * GLA_DV, (p + 1) * GLA_DV) for p in range(2)]

    def body(i, carry):
        states = list(carry)
        pre = []
        for s in range(GLA_STEPS):
            step = i * GLA_STEPS + s
            rows = [pl.ds(pl.multiple_of(n * CHUNK, CHUNK), CHUNK) for n in (step, n_chunks - 1 - step)]
            qe, ke, kd, dec = [], [], [], []
            for d in range(2):
                bc = bc_s[d, rows[d], :]
                bl = bc[CHUNK - 1:CHUNK, :] if d == 0 else bc[0:1, :]
                k = k_ref[rows[d], :]
                qe.append(q_ref[rows[d], :] * (GLA_DK ** -0.5) * jnp.exp(bc))
                ke.append((k * jnp.exp(-bc)).astype(BF16))
                kd.append(k * jnp.exp(bl - bc))
                dec.append(jnp.exp(bl))
            qep = [(qe[d] * head_mask[p]).astype(BF16) for d, p in chains]
            kdp = [(kd[d] * head_mask[p]).astype(BF16) for d, p in chains]
            vp = [v_ref[rows[d], cols[p]].astype(BF16) for d, p in chains]
            sc = [jnp.where(before[d], _dot_nt(qep[c], ke[d]), 0.0).astype(BF16) for c, (d, p) in enumerate(chains)]
            kv = [_dot_tn(vp[c], kdp[c]) for c in range(4)]
            pre.append((rows, qep, vp, sc, kv, dec))
        for rows, qep, vp, sc, kv, dec in pre:
            inter = [_dot_nt(qep[c], states[c].astype(BF16)) for c in range(4)]
            for c, (d, p) in enumerate(chains):
                o_s[d, rows[d], cols[p]] = _dot(sc[c], vp[c]) + inter[c]
                states[c] = states[c] * dec[d] + kv[c]
        return tuple(states)

    init = []
    for d in range(2):
        if has_s0:
            s0t = s0_ref[d].T
            init += [s0t * head_mask[0], s0t * head_mask[1]]
        else:
            init += [jnp.zeros((GLA_DV, LANES), F32), jnp.zeros((GLA_DV, LANES), F32)]
    fin = lax.fori_loop(0, n_chunks // GLA_STEPS, body, tuple(init))
    if write_state:
        for d in range(2):
            st_ref[d] = (fin[2 * d] + fin[2 * d + 1]).T

    for p in range(2):
        cols = slice(p * GLA_DV, (p + 1) * GLA_DV)
        y = _rms(o_s[0, :, cols] + o_s[1, :, cols], gn_ref[...]) * _silu(gg_ref[:, cols])
        mix_ref[:, cols] = y.astype(mix_ref.dtype)


def _gla(proj, n_batch, seq, w2pad, b2, gn, s0, write_state):
    pairs = GLA_H // 2
    pw = 2 * GLA_DV
    st_spec = pl.BlockSpec((None, 2, None, LANES, GLA_DV), lambda b, p: (b, 0, p, 0, 0))
    in_specs = [
        pl.BlockSpec((seq, LANES), lambda b, p: (b, EV_Q // LANES + p)),
        pl.BlockSpec((seq, LANES), lambda b, p: (b, EV_K // LANES + p)),
        pl.BlockSpec((seq, pw), lambda b, p: (b, EV_V // pw + p)),
        pl.BlockSpec((seq, pw), lambda b, p: (b, EV_G // pw + p)),
        pl.BlockSpec((seq, LANES), lambda b, p: (b, EV_LO // LANES)),
        pl.BlockSpec((2, LANES, LANES), lambda b, p: (0, 0, p)),
        pl.BlockSpec((2, 1, LANES), lambda b, p: (0, 0, p)),
        pl.BlockSpec((1, GLA_DV), lambda b, p: (0, 0)),
    ]
    args = [proj, proj, proj, proj, proj, w2pad, b2, gn]
    if s0 is not None:
        in_specs.append(st_spec)
        args.append(s0)
    mix_spec = pl.BlockSpec((seq, pw), lambda b, p: (b, p))
    mix_shape = jax.ShapeDtypeStruct((n_batch * seq, GLA_H * GLA_DV), BF16)
    if write_state:
        out_specs = [mix_spec, st_spec]
        out_shape = [mix_shape, jax.ShapeDtypeStruct((n_batch, 2, pairs, LANES, GLA_DV), F32)]
    else:
        out_specs, out_shape = mix_spec, mix_shape
    return pl.pallas_call(
        functools.partial(_gla_kernel, seq=seq, has_s0=s0 is not None, write_state=write_state),
        grid=(n_batch, pairs),
        in_specs=in_specs,
        out_specs=out_specs,
        out_shape=out_shape,
        scratch_shapes=[pltpu.VMEM((2, seq, pw), F32), pltpu.VMEM((2, seq, LANES), F32)],
        compiler_params=_params("parallel", "parallel"),
        name="gla",
    )(*args)


def _conv_silu(x, w):
    seq = x.shape[0]
    t = lax.broadcasted_iota(jnp.int32, x.shape, 0)
    prev = jnp.where(t == 0, 0.0, pltpu.roll(x, 1, 0))
    nxt = jnp.where(t == seq - 1, 0.0, pltpu.roll(x, seq - 1, 0))
    return _silu(prev * w[0:1, :] + x * w[1:2, :] + nxt * w[2:3, :])


def _hi_lo(x):
    hi = x.astype(BF16)
    return hi, (x - hi.astype(F32)).astype(BF16)


def _split_lanes(a):
    hi, lo = _hi_lo(a)
    return jnp.concatenate([hi, hi, lo], axis=1)


def _split_rows(b):
    hi, lo = _hi_lo(b)
    return jnp.concatenate([hi, lo, hi], axis=0)


def _l2norm(x):
    return x * lax.rsqrt(jnp.sum(x * x, axis=-1, keepdims=True) + EPS)


def _delta_kernel(*refs, seq, hp, cpi, has_s0, write_state):
    it = iter(refs)
    q_ref, k_ref, v_ref, z_ref, gates_ref, wq_ref, wk_ref, wv_ref, gn_ref = (next(it) for _ in range(9))
    s0_ref = next(it) if has_s0 else None
    mix_ref = next(it)
    st_ref = next(it) if write_state else None
    qs, ks, vs, o_s, gc_s, b_s, u0_s, kc_s, at_s, qe_s, kd_s = (next(it) for _ in range(11))
    n_chunks = seq // DN_CHUNK
    lane = lax.broadcasted_iota(jnp.int32, (1, LANES), 1)
    row, col = _order_masks(DN_CHUNK)
    eye = (row == col).astype(F32)
    before = [col <= row, col >= row]
    strict = [col < row, col > row]
    couples = [((row >> (lvl + 1)) == (col >> (lvl + 1))) & ((row >> lvl) != (col >> lvl))
               for lvl in range(DN_CHUNK.bit_length() - 1)]

    gates = gates_ref[...]
    for hd in range(hp):
        cols = slice(hd * HEAD_DIM, (hd + 1) * HEAD_DIM)
        h = pl.program_id(1) * hp + hd
        qs[hd] = _l2norm(_conv_silu(q_ref[:, cols], wq_ref[:, cols])) * (HEAD_DIM ** -0.5)
        ks[hd] = _l2norm(_conv_silu(k_ref[:, cols], wk_ref[:, cols]))
        vs[hd] = _conv_silu(v_ref[:, cols], wv_ref[:, cols])
        for d in range(2):
            sel_g = lane == d * DN_H + h
            sel_b = lane == 2 * DN_H + d * DN_H + h
            gc_s[hd, d] = jnp.broadcast_to(
                jnp.sum(jnp.where(sel_g, gates, 0.0), axis=-1, keepdims=True), (seq, LANES))
            b_s[hd, d] = jnp.broadcast_to(
                jnp.sum(jnp.where(sel_b, gates, 0.0), axis=-1, keepdims=True), (seq, LANES))

    def chunk_rows(n):
        return pl.ds(pl.multiple_of(n * DN_CHUNK, DN_CHUNK), DN_CHUNK)

    def last_row(d, n):
        return pl.ds(n * DN_CHUNK + (DN_CHUNK - 1 if d == 0 else 0), 1)

    def wy_stages(units):
        nu = range(len(units))
        rows = [chunk_rows(n) for _, _, n in units]
        gc = [gc_s[hd, d, rows[i], :] for i, (hd, d, _) in enumerate(units)]
        bb = [b_s[hd, d, rows[i], :] for i, (hd, d, _) in enumerate(units)]
        k = [ks[hd, rows[i], :] for i, (hd, _, _) in enumerate(units)]
        k16 = [k[i].astype(BF16) for i in nu]
        kbeta = [k[i] * bb[i] for i in nu]
        kk = [_dot_nt(kbeta[i].astype(BF16), k16[i]) for i in nu]
        qk = [_dot_nt(qs[hd, rows[i], :].astype(BF16), k16[i]) for i, (hd, _, _) in enumerate(units)]
        yield
        mm = []
        for i, (hd, d, _) in enumerate(units):
            decay = jnp.where(before[d], jnp.exp(jnp.where(before[d], gc[i] - gc[i].T, 0.0)), 0.0)
            mm.append(jnp.where(strict[d], kk[i] * decay, 0.0))
            at_s[hd, d, rows[i], :] = (qk[i] * decay).astype(BF16)
        inv = [eye - jnp.where(couples[0], mm[i], 0.0) for i in nu]
        for lvl in range(1, len(couples)):
            inv16 = [inv[i].astype(BF16) for i in nu]
            c_inv = [_dot(jnp.where(couples[lvl], mm[i], 0.0).astype(BF16), inv16[i]).astype(BF16) for i in nu]
            yield
            inv = [inv[i] - _dot(inv16[i], c_inv[i]) for i in nu]
            yield
        inv16 = [inv[i].astype(BF16) for i in nu]
        egc = [jnp.exp(gc[i]) for i in nu]
        rhs = [jnp.concatenate([vs[hd, rows[i], :] * bb[i], kbeta[i] * egc[i]], axis=-1)
               for i, (hd, _, _) in enumerate(units)]
        sol = [_dot(inv16[i], rhs[i].astype(BF16)) for i in nu]
        yield
        mm_split = [_split_lanes(mm[i]) for i in nu]
        for _ in range(DN_REFINE_STEPS):
            res = [rhs[i] - sol[i] - _dot(mm_split[i], _split_rows(sol[i])) for i in nu]
            yield
            sol = [sol[i] + _dot(inv16[i], res[i].astype(BF16)) for i in nu]
            yield
        for i, (hd, d, _) in enumerate(units):
            u0_s[hd, d, rows[i], :] = sol[i][:, :HEAD_DIM]
            kc_s[hd, d, rows[i], :] = sol[i][:, HEAD_DIM:].astype(BF16)
            g_last = gc[i][DN_CHUNK - 1:DN_CHUNK, :] if d == 0 else gc[i][0:1, :]
            qe_s[hd, d, rows[i], :] = (qs[hd, rows[i], :] * egc[i]).astype(BF16)
            kd_s[hd, d, rows[i], :] = (k[i] * jnp.exp(g_last - gc[i])).astype(BF16)

    chains = [(hd, d) for hd in range(hp) for d in range(2)]

    def chunk_of(d, step):
        return step if d == 0 else n_chunks - 1 - step

    def scan_stages(state, steps):
        for step in steps:
            ns = [chunk_of(d, step) for _, d in chains]
            rows = [chunk_rows(n) for n in ns]
            s16 = [st.astype(BF16) for st in state]
            ks_ = [_dot(kc_s[hd, d, rows[c], :], s16[c]) for c, (hd, d) in enumerate(chains)]
            qs_ = [_dot(qe_s[hd, d, rows[c], :], s16[c]) for c, (hd, d) in enumerate(chains)]
            yield
            u16 = [(u0_s[hd, d, rows[c], :] - ks_[c]).astype(BF16) for c, (hd, d) in enumerate(chains)]
            au = [_dot(at_s[hd, d, rows[c], :], u16[c]) for c, (hd, d) in enumerate(chains)]
            ku = [_dot_tn(kd_s[hd, d, rows[c], :], u16[c]) for c, (hd, d) in enumerate(chains)]
            yield
            for c, (hd, d) in enumerate(chains):
                o_s[hd, d, rows[c], :] = qs_[c] + au[c]
                state[c] = state[c] * jnp.exp(gc_s[hd, d, last_row(d, ns[c]), :]) + ku[c]

    def run(*stages):
        live = list(stages)
        while live:
            for g in list(live):
                try:
                    next(g)
                except StopIteration:
                    live.remove(g)

    def group_units(j):
        return [(hd, d, chunk_of(d, j * cpi + c)) for c in range(cpi) for hd in range(hp) for d in range(2)]

    def group_steps(j):
        return [j * cpi + c for c in range(cpi)]

    n_groups = n_chunks // cpi
    if has_s0:
        init = tuple(s0_ref[d, hd] for hd, d in chains)
    else:
        init = tuple(jnp.zeros((HEAD_DIM, HEAD_DIM), F32) for _ in chains)
    run(wy_stages(group_units(0)))

    def body(j, carry):
        state = list(carry)
        run(wy_stages(group_units(j)), scan_stages(state, group_steps(j - 1)))
        return tuple(state)

    fin = list(lax.fori_loop(1, n_groups, body, init))
    run(scan_stages(fin, group_steps(n_groups - 1)))
    if write_state:
        for c, (hd, d) in enumerate(chains):
            st_ref[d, hd] = fin[c]

    for hd in range(hp):
        cols = slice(hd * HEAD_DIM, (hd + 1) * HEAD_DIM)
        y = _rms(o_s[hd, 0] + o_s[hd, 1], gn_ref[...]) * _silu(z_ref[:, cols])
        mix_ref[:, cols] = y.astype(mix_ref.dtype)


def _dn_gates_kernel(ab_ref, alog_ref, dtb_ref, o_ref):
    ab = ab_ref[...]
    lane = lax.broadcasted_iota(jnp.int32, (1, LANES), 1)
    g = -jnp.exp(alog_ref[...]) * _softplus(ab + dtb_ref[...])
    fwd = _chunk_cumsum(g, DN_CHUNK, False)
    bwd = _chunk_cumsum(g, DN_CHUNK, True)
    o_ref[...] = jnp.where(lane < DN_H, fwd, jnp.where(lane < 2 * DN_H, bwd, _sigmoid(ab)))


def _dn_gates(proj, n_batch, seq, alog_row, dtb_row):
    vec = pl.BlockSpec((1, LANES), lambda b: (0, 0))
    return pl.pallas_call(
        _dn_gates_kernel,
        grid=(n_batch,),
        in_specs=[pl.BlockSpec((seq, LANES), lambda b: (b, OD_AB // LANES)), vec, vec],
        out_specs=pl.BlockSpec((seq, LANES), lambda b: (b, 0)),
        out_shape=jax.ShapeDtypeStruct((n_batch * seq, LANES), F32),
        compiler_params=_params("parallel"),
        name="dn_gates",
    )(proj, alog_row, dtb_row)


def _delta(proj, gates, n_batch, seq, conv_w, gn, s0, write_state):
    n_chunks = seq // DN_CHUNK
    hp = max(1, min(DN_H, DN_UNITS // (2 * n_chunks)))
    cpi = DN_UNITS // (2 * hp)
    hw = hp * HEAD_DIM

    def col_spec(col0):
        return pl.BlockSpec((seq, hw), lambda b, h: (b, col0 // hw + h))

    def conv_spec(part):
        return pl.BlockSpec((3, hw), lambda b, h: (0, part * (DN_H // hp) + h))

    vec = pl.BlockSpec((1, LANES), lambda b, h: (0, 0))
    st_spec = pl.BlockSpec((None, 2, hp, HEAD_DIM, HEAD_DIM), lambda b, h: (b, 0, h, 0, 0))
    in_specs = [col_spec(OD_DQ), col_spec(OD_DK), col_spec(OD_DV), col_spec(OD_DZ),
                pl.BlockSpec((seq, LANES), lambda b, h: (b, 0)),
                conv_spec(0), conv_spec(1), conv_spec(2), vec]
    args = [proj, proj, proj, proj, gates, conv_w, conv_w, conv_w, gn]
    if s0 is not None:
        in_specs.append(st_spec)
        args.append(s0)
    mix_spec = pl.BlockSpec((seq, hw), lambda b, h: (b, h))
    mix_shape = jax.ShapeDtypeStruct((n_batch * seq, DN_H * HEAD_DIM), BF16)
    if write_state:
        out_specs = [mix_spec, st_spec]
        out_shape = [mix_shape, jax.ShapeDtypeStruct((n_batch, 2, DN_H, HEAD_DIM, HEAD_DIM), F32)]
    else:
        out_specs, out_shape = mix_spec, mix_shape
    tok = pltpu.VMEM((hp, seq, HEAD_DIM), F32)
    both = pltpu.VMEM((hp, 2, seq, HEAD_DIM), F32)
    both16 = pltpu.VMEM((hp, 2, seq, HEAD_DIM), BF16)
    return pl.pallas_call(
        functools.partial(_delta_kernel, seq=seq, hp=hp, cpi=cpi, has_s0=s0 is not None, write_state=write_state),
        grid=(n_batch, DN_H // hp),
        in_specs=in_specs,
        out_specs=out_specs,
        out_shape=out_shape,
        scratch_shapes=[tok, tok, tok, both, both, both, both, both16, both16, both16, both16],
        compiler_params=_params("parallel", "arbitrary"),
        name="deltanet",
    )(*args)


def _rope_tables(n_tok):
    t = jnp.arange(n_tok)
    inv = 1.0 / (ROPE_THETA ** (jnp.arange(ROPE_FREQ, dtype=F32) / ROPE_FREQ))
    pos = jnp.stack([t // GRID_W, t % GRID_W], axis=1).astype(F32)
    ang = pos[:, :, None] * inv
    cos = jnp.concatenate([jnp.cos(ang), jnp.cos(ang)], axis=-1).reshape(n_tok, HEAD_DIM)
    sin = jnp.concatenate([-jnp.sin(ang), jnp.sin(ang)], axis=-1).reshape(n_tok, HEAD_DIM)
    return cos, sin


def _pad_lanes(v):
    return jnp.pad(v.reshape(1, -1), ((0, 0), (0, LANES - v.size)))


def kernel(x_prompt, x_sample, state_gla, cache_gqa_k, cache_gqa_v, cache_na_k, cache_na_v, state_delta, c, c_ctx, norm1, norm2, w_ada, b_ada, w_mlp1, w_mlp2, ev_w_in, ev_w_a2, ev_b_a2, ev_gla_norm, ev_q_norm, ev_k_norm, ev_w_out, od_w_in, od_conv, od_a_log, od_dt_bias, od_dn_norm, od_rpb, od_w_out, norm_f):
    n_ctx, seq_ctx, d = x_prompt.shape
    n_lat, seq_lat, _ = x_sample.shape
    depth = w_ada.shape[0]

    w_ev = ev_w_in[0]
    w_ev = jnp.concatenate(
        [w_ev[:, :EV_AQ], w_ev[:, EV_AQ + 2 * GLA_RANK:], w_ev[:, EV_AQ:EV_AQ + 2 * GLA_RANK],
         jnp.zeros((d, EV_COLS_PAD - w_ev.shape[1]), F32)], axis=1).astype(BF16)
    w_od = jnp.pad(od_w_in[0], ((0, 0), (0, OD_COLS_PAD - od_w_in.shape[2]))).astype(BF16)
    w_in = [w_ev, w_od]
    w_out = [ev_w_out[0].astype(BF16), od_w_out[0].astype(BF16)]
    w1 = w_mlp1.astype(BF16)
    w2 = w_mlp2.astype(BF16)
    w2pad = jnp.zeros((2, LANES, GLA_H * GLA_DK), F32)
    for dd in range(2):
        w2pad = w2pad.at[dd, dd * GLA_RANK:(dd + 1) * GLA_RANK].set(ev_w_a2[0, dd])
    b2 = ev_b_a2[0].reshape(2, 1, GLA_H * GLA_DK)
    alog_row = _pad_lanes(od_a_log[0])
    dtb_row = _pad_lanes(od_dt_bias[0])
    na_bias = _na_bias_table(od_rpb[0])
    rope = _rope_tables(seq_lat)

    cond8 = jnp.concatenate([c_ctx[None, :], c, jnp.zeros((8 - 1 - n_lat, d), F32)], axis=0)
    mod = _adaln(cond8, w_ada, b_ada).reshape(depth, 8, 6, 1, d)

    def trunk(x, n_batch, seq, row_fn, caches):
        latent = caches is not None
        tm = TM_MLP
        outs = {}
        proj = _inproj(x, norm1[0:1], mod, 0, row_fn, w_in[0], TM_INPROJ)
        if latent:
            s0 = caches["gla"].reshape(n_batch, 2, GLA_H // 2, LANES, GLA_DV)
            mix_a = _gla(proj, n_batch, seq, w2pad, b2, ev_gla_norm, s0, False)
            mix_b = _attention(proj, n_batch, seq, EV_AQ, EV_AK, EV_AV, GQA_KV, GQA_H // GQA_KV,
                               norms=(ev_q_norm, ev_k_norm), rope=rope, ctx=caches["gqa"])
        else:
            mix_a, st = _gla(proj, n_batch, seq, w2pad, b2, ev_gla_norm, None, True)
            mix_b, k_new = _attention(proj, n_batch, seq, EV_AQ, EV_AK, EV_AV, GQA_KV, GQA_H // GQA_KV,
                                      norms=(ev_q_norm, ev_k_norm), write_k=True)
            outs["st_gla"] = st.reshape(n_batch, 1, 2, GLA_H, GLA_DK, GLA_DV)
            outs["ck_gqa"] = k_new.reshape(n_batch, 1, seq, GQA_KV, HEAD_DIM)
            outs["cv_gqa"] = proj[:, EV_AV:EV_AV + GQA_KV * HEAD_DIM].reshape(n_batch, 1, seq, GQA_KV, HEAD_DIM)
        x = _outproj(mix_a, mix_b, w_out[0], x, mod, 0, row_fn, tm)
        x = _mlp(x, norm2[0:1], mod, 0, row_fn, w1, w2, norm_f[None, :], False, tm)
        proj = _inproj(x, norm1[1:2], mod, 1, row_fn, w_in[1], TM_INPROJ)
        if latent:
            mix_a = _neighbourhood(proj, n_batch, seq, caches["na"][0], caches["na"][1], na_bias)
            s0 = caches["delta"].reshape(n_batch, 2, DN_H, HEAD_DIM, HEAD_DIM)
            gates = _dn_gates(proj, n_batch, seq, alog_row, dtb_row)
            mix_b = _delta(proj, gates, n_batch, seq, od_conv[0], od_dn_norm, s0, False)
        else:
            mix_a = _attention(proj, n_batch, seq, OD_NQ, OD_NK, OD_NV, NA_H, 1)
            gates = _dn_gates(proj, n_batch, seq, alog_row, dtb_row)
            mix_b, st = _delta(proj, gates, n_batch, seq, od_conv[0], od_dn_norm, None, True)
            outs["ck_na"] = proj[:, OD_NK:OD_NK + NA_H * HEAD_DIM].reshape(n_batch, 1, seq, NA_H, HEAD_DIM)
            outs["cv_na"] = proj[:, OD_NV:OD_NV + NA_H * HEAD_DIM].reshape(n_batch, 1, seq, NA_H, HEAD_DIM)
            outs["st_dn"] = st.reshape(n_batch, 1, 2, DN_H, HEAD_DIM, HEAD_DIM)
        x = _outproj(mix_a, mix_b, w_out[1], x, mod, 1, row_fn, tm)
        y = _mlp(x, norm2[1:2], mod, 1, row_fn, w1, w2, norm_f[None, :], True, tm)
        return y.reshape(n_batch, seq, d), outs

    y_prompt, new = trunk(x_prompt.reshape(n_ctx * seq_ctx, d), n_ctx, seq_ctx, lambda tok: 0, None)
    past = cache_gqa_k.shape[2]
    caches = {
        "gla": state_gla[:, 0],
        "gqa": (cache_gqa_k[:, 0].reshape(n_lat, past, GQA_KV * HEAD_DIM),
                cache_gqa_v[:, 0].reshape(n_lat, past, GQA_KV * HEAD_DIM)),
        "na": (cache_na_k[:, 0].reshape(n_lat, past, NA_H * HEAD_DIM),
               cache_na_v[:, 0].reshape(n_lat, past, NA_H * HEAD_DIM)),
        "delta": state_delta[:, 0],
    }
    y_sample, _ = trunk(x_sample.reshape(n_lat * seq_lat, d), n_lat, seq_lat,
                        lambda tok: 1 + tok // seq_lat, caches)
    return (y_prompt, y_sample, new["st_gla"], new["ck_gqa"], new["cv_gqa"], new["ck_na"], new["cv_na"],
            new["st_dn"])
```

```python
import functools

import jax
import jax.numpy as jnp
import numpy as np
from jax import lax
from jax.experimental import pallas as pl
from jax.experimental.pallas import tpu as pltpu

F32 = jnp.float32
BF16 = jnp.bfloat16
HIGHEST = lax.Precision.HIGHEST

D_MODEL = 2048
D_FF = 4 * D_MODEL
HEAD_DIM = 128
LANES = 128
GRID_W = 64
GLA_H = 8
GLA_DK = 64
GLA_DV = 128
GLA_RANK = 16
GLA_TAU = 16.0
GQA_H = 8
GQA_KV = 2
NA_H = 8
NA_WIN_R = 8
NA_WIN_C = 16
DN_H = 8
CHUNK = 64
DN_CHUNK = 128
DN_REFINE_STEPS = 1
DN_UNITS = 16
TM_INPROJ = 1024
TN_INPROJ = 1280
TM_MLP = 512
TF_MLP = 1024
GLA_STEPS = 4
NA_ROWS_PER_STEP = 16
ROPE_THETA = 10000.0
ROPE_FREQ = HEAD_DIM // 4
EPS = 1e-6
MASK_VALUE = -1e30

EV_Q, EV_K, EV_V, EV_G = 0, 512, 1024, 2048
EV_AQ, EV_AK, EV_AV, EV_LO = 3072, 4096, 4352, 4608
EV_COLS_PAD = 5120
OD_NQ, OD_NK, OD_NV = 0, 1024, 2048
OD_DQ, OD_DK, OD_DV, OD_DZ, OD_AB = 3072, 4096, 5120, 6144, 7168
OD_COLS_PAD = 7680

VMEM_LIMIT = 56 * 1024 * 1024


def _params(*sem):
    return pltpu.CompilerParams(dimension_semantics=sem, vmem_limit_bytes=VMEM_LIMIT)


def _dot(a, b):
    return jnp.dot(a, b, preferred_element_type=F32)


def _dot_nt(a, b):
    return lax.dot_general(a, b, (((1,), (1,)), ((), ())), preferred_element_type=F32)


def _dot_tn(a, b):
    return lax.dot_general(a, b, (((0,), (0,)), ((), ())), preferred_element_type=F32)


def _dot_hi(a, b):
    return jnp.dot(a, b, precision=HIGHEST, preferred_element_type=F32)


def _sigmoid(x):
    return 1.0 / (1.0 + jnp.exp(-x))


def _silu(x):
    return x * _sigmoid(x)


def _softplus(x):
    return jnp.maximum(x, 0.0) + jnp.log(1.0 + jnp.exp(-jnp.abs(x)))


def _rms(x, g):
    return x * lax.rsqrt(jnp.mean(x * x, axis=-1, keepdims=True) + EPS) * g


def _rms_mod(x, g, sc, sh):
    return _rms(x, g) * (1.0 + sc) + sh


def _adaln_kernel(c_ref, w_ref, b_ref, o_ref):
    a = _silu(c_ref[...]).astype(BF16)
    o_ref[...] = _dot(a, w_ref[...].astype(BF16)) + b_ref[...]


def _adaln(cond8, w_ada, b_ada):
    depth, d, n = w_ada.shape
    tn = 512
    return pl.pallas_call(
        _adaln_kernel,
        grid=(depth, n // tn),
        in_specs=[
            pl.BlockSpec((8, d), lambda l, j: (0, 0)),
            pl.BlockSpec((None, d, tn), lambda l, j: (l, 0, j)),
            pl.BlockSpec((None, 1, tn), lambda l, j: (l, 0, j)),
        ],
        out_specs=pl.BlockSpec((None, 8, tn), lambda l, j: (l, 0, j)),
        out_shape=jax.ShapeDtypeStruct((depth, 8, n), F32),
        compiler_params=_params("parallel", "parallel"),
        name="adaln",
    )(cond8, w_ada, b_ada.reshape(depth, 1, n))


def _mod_spec(layer, k, row_fn, tm, tn=D_MODEL):
    return pl.BlockSpec((None, None, None, 1, tn), lambda i, j: (layer, row_fn(i * tm), k, 0, j if tn < D_MODEL else 0))


def _inproj_kernel(x_ref, g_ref, sc_ref, sh_ref, w_ref, o_ref, h_ref):
    @pl.when(pl.program_id(1) == 0)
    def _():
        h_ref[...] = _rms_mod(x_ref[...], g_ref[...], sc_ref[...], sh_ref[...]).astype(BF16)

    o_ref[...] = _dot(h_ref[...], w_ref[...])


def _inproj(x, g, mod, layer, row_fn, w, tm):
    m, d = x.shape
    n = w.shape[1]
    tn = TN_INPROJ
    return pl.pallas_call(
        _inproj_kernel,
        grid=(m // tm, n // tn),
        in_specs=[
            pl.BlockSpec((tm, d), lambda i, j: (i, 0)),
            pl.BlockSpec((1, d), lambda i, j: (0, 0)),
            _mod_spec(layer, 1, row_fn, tm),
            _mod_spec(layer, 0, row_fn, tm),
            pl.BlockSpec((d, tn), lambda i, j: (0, j)),
        ],
        out_specs=pl.BlockSpec((tm, tn), lambda i, j: (i, j)),
        out_shape=jax.ShapeDtypeStruct((m, n), F32),
        scratch_shapes=[pltpu.VMEM((tm, d), BF16)],
        compiler_params=_params("parallel", "arbitrary"),
        name="inproj",
    )(x, g, mod, mod, w)


def _outproj_kernel(a_ref, b_ref, wa_ref, wb_ref, x_ref, gt_ref, o_ref):
    acc = _dot(a_ref[...], wa_ref[...]) + _dot(b_ref[...], wb_ref[...])
    o_ref[...] = x_ref[...] + gt_ref[...] * acc


def _outproj(mix_a, mix_b, w, x, mod, layer, row_fn, tm):
    m, d = x.shape
    ka, kb = mix_a.shape[1], mix_b.shape[1]
    tn = d
    gt_spec = _mod_spec(layer, 2, row_fn, tm, tn)
    return pl.pallas_call(
        _outproj_kernel,
        grid=(m // tm, d // tn),
        in_specs=[
            pl.BlockSpec((tm, ka), lambda i, j: (i, 0)),
            pl.BlockSpec((tm, kb), lambda i, j: (i, 0)),
            pl.BlockSpec((ka, tn), lambda i, j: (0, j)),
            pl.BlockSpec((kb, tn), lambda i, j: (1, j)),
            pl.BlockSpec((tm, tn), lambda i, j: (i, j)),
            gt_spec,
        ],
        out_specs=pl.BlockSpec((tm, tn), lambda i, j: (i, j)),
        out_shape=jax.ShapeDtypeStruct((m, d), F32),
        compiler_params=_params("parallel", "parallel"),
        name="outproj",
    )(mix_a, mix_b, w, w, x, mod)


def _mlp_kernel(x_ref, g_ref, sc_ref, sh_ref, gt_ref, w1_ref, w2_ref, gf_ref, o_ref, h_ref, acc_ref, *, final_norm):
    j = pl.program_id(1)

    @pl.when(j == 0)
    def _():
        h_ref[...] = _rms_mod(x_ref[...], g_ref[...], sc_ref[...], sh_ref[...]).astype(BF16)
        acc_ref[...] = jnp.zeros_like(acc_ref)

    hid = jnp.maximum(_dot(h_ref[...], w1_ref[...]), 0.0)
    acc_ref[...] += _dot((hid * hid).astype(BF16), w2_ref[...])

    @pl.when(j == pl.num_programs(1) - 1)
    def _():
        y = x_ref[...] + gt_ref[...] * acc_ref[...]
        if final_norm:
            y = _rms(y, gf_ref[...])
        o_ref[...] = y


def _mlp(x, g, mod, layer, row_fn, w1, w2, gf, final_norm, tm):
    m, d = x.shape
    ff = w1.shape[2]
    tf = TF_MLP
    return pl.pallas_call(
        functools.partial(_mlp_kernel, final_norm=final_norm),
        grid=(m // tm, ff // tf),
        in_specs=[
            pl.BlockSpec((tm, d), lambda i, j: (i, 0)),
            pl.BlockSpec((1, d), lambda i, j: (0, 0)),
            _mod_spec(layer, 4, row_fn, tm),
            _mod_spec(layer, 3, row_fn, tm),
            _mod_spec(layer, 5, row_fn, tm),
            pl.BlockSpec((None, d, tf), lambda i, j: (layer, 0, j)),
            pl.BlockSpec((None, tf, d), lambda i, j: (layer, j, 0)),
            pl.BlockSpec((1, d), lambda i, j: (0, 0)),
        ],
        out_specs=pl.BlockSpec((tm, d), lambda i, j: (i, 0)),
        out_shape=jax.ShapeDtypeStruct((m, d), F32),
        scratch_shapes=[pltpu.VMEM((tm, d), BF16), pltpu.VMEM((tm, d), F32)],
        compiler_params=_params("parallel", "arbitrary"),
        name="mlp",
    )(x, g, mod, mod, mod, w1, w2, gf)


def _rope(x, cos, sin):
    lane = lax.broadcasted_iota(jnp.int32, x.shape, 1)
    first_half = (lane % (2 * ROPE_FREQ)) < ROPE_FREQ
    partner = jnp.where(first_half, pltpu.roll(x, LANES - ROPE_FREQ, 1), pltpu.roll(x, ROPE_FREQ, 1))
    return x * cos + partner * sin


def _attn_kernel(*refs, group, use_norm, use_rope, has_ctx, write_k):
    it = iter(refs)
    q_ref, k_ref, v_ref = next(it), next(it), next(it)
    qn_ref = kn_ref = cq_ref = sq_ref = ck_ref = sk_ref = kc_ref = vc_ref = knew_ref = None
    if use_norm:
        qn_ref, kn_ref = next(it), next(it)
    if use_rope:
        cq_ref, sq_ref, ck_ref, sk_ref = next(it), next(it), next(it), next(it)
    if has_ctx:
        kc_ref, vc_ref = next(it), next(it)
    o_ref = next(it)
    if write_k:
        knew_ref = next(it)
    kbuf, vbuf = next(it), next(it)
    scale = HEAD_DIM ** -0.5 * float(np.log2(np.e))

    @pl.when(pl.program_id(2) == 0)
    def _():
        k = k_ref[...]
        if use_norm:
            k = _rms(k, kn_ref[...])
        if write_k:
            knew_ref[...] = k
        if use_rope:
            k = _rope(k, ck_ref[...], sk_ref[...])
        kbuf[...] = k.astype(BF16)
        vbuf[...] = v_ref[...].astype(BF16)

    heads = range(group)
    cols = [slice(g * HEAD_DIM, (g + 1) * HEAD_DIM) for g in heads]
    qb = []
    for g in heads:
        q = q_ref[:, cols[g]]
        if use_norm:
            q = _rms(q, qn_ref[...])
        if use_rope:
            q = _rope(q, cq_ref[...], sq_ref[...])
        qb.append((q * scale).astype(BF16))
    s = [_dot_nt(qb[g], kbuf[...]) for g in heads]
    m = [jnp.max(s[g], axis=-1, keepdims=True) for g in heads]
    if has_ctx:
        kc16 = kc_ref[...].astype(BF16)
        s_c = [_dot_nt(qb[g], kc16) for g in heads]
        m = [jnp.maximum(m[g], jnp.max(s_c[g], axis=-1, keepdims=True)) for g in heads]
    p = [jnp.exp2(s[g] - m[g]) for g in heads]
    l = [jnp.sum(p[g], axis=-1, keepdims=True) for g in heads]
    o = [_dot(p[g].astype(BF16), vbuf[...]) for g in heads]
    if has_ctx:
        vc16 = vc_ref[...].astype(BF16)
        p_c = [jnp.exp2(s_c[g] - m[g]) for g in heads]
        l = [l[g] + jnp.sum(p_c[g], axis=-1, keepdims=True) for g in heads]
        o = [o[g] + _dot(p_c[g].astype(BF16), vc16) for g in heads]
    for g in heads:
        o_ref[:, cols[g]] = (o[g] / l[g]).astype(o_ref.dtype)


def _attention(proj, n_batch, seq, q_col, k_col, v_col, n_kv, group, norms=None, rope=None, ctx=None,
               write_k=False):
    tq = min(seq, 256)
    nq = seq // tq
    gw = group * HEAD_DIM
    in_specs = [
        pl.BlockSpec((tq, gw), lambda b, h, i: (b * nq + i, q_col // gw + h)),
        pl.BlockSpec((seq, HEAD_DIM), lambda b, h, i: (b, k_col // HEAD_DIM + h)),
        pl.BlockSpec((seq, HEAD_DIM), lambda b, h, i: (b, v_col // HEAD_DIM + h)),
    ]
    args = [proj, proj, proj]
    vec = pl.BlockSpec((1, HEAD_DIM), lambda b, h, i: (0, 0))
    if norms is not None:
        in_specs += [vec, vec]
        args += [norms[0], norms[1]]
    if rope is not None:
        tab_q = pl.BlockSpec((tq, HEAD_DIM), lambda b, h, i: (i, 0))
        tab_k = pl.BlockSpec((seq, HEAD_DIM), lambda b, h, i: (0, 0))
        in_specs += [tab_q, tab_q, tab_k, tab_k]
        args += [rope[0], rope[1], rope[0], rope[1]]
    if ctx is not None:
        past = ctx[0].shape[1]
        c_spec = pl.BlockSpec((None, past, HEAD_DIM), lambda b, h, i: (b, 0, h))
        in_specs += [c_spec, c_spec]
        args += [ctx[0], ctx[1]]
    o_spec = pl.BlockSpec((tq, gw), lambda b, h, i: (b * nq + i, h))
    o_shape = jax.ShapeDtypeStruct((n_batch * seq, n_kv * gw), BF16)
    if write_k:
        out_specs = [o_spec, pl.BlockSpec((seq, HEAD_DIM), lambda b, h, i: (b, h))]
        out_shape = [o_shape, jax.ShapeDtypeStruct((n_batch * seq, n_kv * HEAD_DIM), F32)]
    else:
        out_specs, out_shape = o_spec, o_shape
    return pl.pallas_call(
        functools.partial(_attn_kernel, group=group, use_norm=norms is not None, use_rope=rope is not None,
                          has_ctx=ctx is not None, write_k=write_k),
        grid=(n_batch, n_kv, nq),
        in_specs=in_specs,
        out_specs=out_specs,
        out_shape=out_shape,
        scratch_shapes=[pltpu.VMEM((seq, HEAD_DIM), BF16), pltpu.VMEM((seq, HEAD_DIM), BF16)],
        compiler_params=_params("parallel", "parallel", "arbitrary"),
        name="attention",
    )(*args)


def _na_row_start(r, rows):
    return jnp.clip(r - NA_WIN_R // 2, 0, rows - NA_WIN_R)


def _na_kernel(q_ref, k_ref, v_ref, kc_ref, vc_ref, bias_ref, o_ref, kb, vb, kcb, vcb, *, rows):
    win = NA_WIN_R * GRID_W
    scale = HEAD_DIM ** -0.5
    kb[...] = k_ref[...].astype(BF16)
    vb[...] = v_ref[...].astype(BF16)
    kcb[...] = kc_ref[...].astype(BF16)
    vcb[...] = vc_ref[...].astype(BF16)

    def body(j, carry):
        rr = [j * NA_ROWS_PER_STEP + i for i in range(NA_ROWS_PER_STEP)]
        rs = [_na_row_start(r, rows) for r in rr]
        k_rows = [pl.ds(pl.multiple_of(s * GRID_W, GRID_W), win) for s in rs]
        q_rows = [pl.ds(pl.multiple_of(r * GRID_W, GRID_W), GRID_W) for r in rr]
        qb = [q_ref[q_rows[i], :].astype(BF16) for i in range(NA_ROWS_PER_STEP)]
        s_w = [_dot_nt(qb[i], kb[k_rows[i], :]) * scale for i in range(NA_ROWS_PER_STEP)]
        s_c = [_dot_nt(qb[i], kcb[...]) * scale for i in range(NA_ROWS_PER_STEP)]
        p_w, p_c, l = [], [], []
        for i in range(NA_ROWS_PER_STEP):
            delta = rr[i] - rs[i]
            bias = jnp.concatenate(
                [bias_ref[2 * jj - delta + NA_WIN_R - 1] for jj in range(NA_WIN_R // 2)], axis=-1)
            sw = s_w[i] + bias
            m = jnp.maximum(jnp.max(sw, axis=-1, keepdims=True), jnp.max(s_c[i], axis=-1, keepdims=True))
            pw = jnp.exp(sw - m)
            pc = jnp.exp(s_c[i] - m)
            l.append(jnp.sum(pw, axis=-1, keepdims=True) + jnp.sum(pc, axis=-1, keepdims=True))
            p_w.append(pw.astype(BF16))
            p_c.append(pc.astype(BF16))
        o_w = [_dot(p_w[i], vb[k_rows[i], :]) for i in range(NA_ROWS_PER_STEP)]
        o_c = [_dot(p_c[i], vcb[...]) for i in range(NA_ROWS_PER_STEP)]
        for i in range(NA_ROWS_PER_STEP):
            o_ref[q_rows[i], :] = ((o_w[i] + o_c[i]) / l[i]).astype(o_ref.dtype)
        return carry

    lax.fori_loop(0, rows // NA_ROWS_PER_STEP, body, 0)


def _na_bias_table(rpb):
    cols = jnp.arange(GRID_W)
    start = jnp.clip(cols - NA_WIN_C // 2, 0, GRID_W - NA_WIN_C)
    inside = (cols[None, :] >= start[:, None]) & (cols[None, :] < start[:, None] + NA_WIN_C)
    rel = cols[None, :] - cols[:, None] + NA_WIN_C - 1
    onehot = (rel[:, :, None] == jnp.arange(2 * NA_WIN_C - 1)).astype(F32)
    picked = jnp.einsum("hrm,ckm->hrck", rpb, onehot, precision=HIGHEST)
    tab = jnp.where(inside[None, None], picked, MASK_VALUE)
    return jnp.concatenate([tab[:, :-1], tab[:, 1:]], axis=-1)


def _neighbourhood(proj, n_batch, seq, k_ctx, v_ctx, bias):
    rows = seq // GRID_W
    past = k_ctx.shape[1]
    c_spec = pl.BlockSpec((None, past, HEAD_DIM), lambda b, h: (b, 0, h))

    def col_spec(col0):
        return pl.BlockSpec((seq, HEAD_DIM), lambda b, h: (b, col0 // HEAD_DIM + h))

    return pl.pallas_call(
        functools.partial(_na_kernel, rows=rows),
        grid=(n_batch, NA_H),
        in_specs=[
            col_spec(OD_NQ), col_spec(OD_NK), col_spec(OD_NV), c_spec, c_spec,
            pl.BlockSpec((None, 2 * NA_WIN_R - 2, GRID_W, 2 * GRID_W), lambda b, h: (h, 0, 0, 0)),
        ],
        out_specs=pl.BlockSpec((seq, HEAD_DIM), lambda b, h: (b, h)),
        out_shape=jax.ShapeDtypeStruct((n_batch * seq, NA_H * HEAD_DIM), BF16),
        scratch_shapes=[pltpu.VMEM((seq, HEAD_DIM), BF16), pltpu.VMEM((seq, HEAD_DIM), BF16),
                        pltpu.VMEM((past, HEAD_DIM), BF16), pltpu.VMEM((past, HEAD_DIM), BF16)],
        compiler_params=_params("parallel", "parallel"),
        name="neighbourhood",
    )(proj, proj, proj, k_ctx, v_ctx, bias)


def _order_masks(n=CHUNK):
    row = lax.broadcasted_iota(jnp.int32, (n, n), 0)
    col = lax.broadcasted_iota(jnp.int32, (n, n), 1)
    return row, col


def _chunk_cumsum(x, chunk, reverse):
    seq = x.shape[0]
    pos = lax.broadcasted_iota(jnp.int32, x.shape, 0) % chunk
    shift = 1
    while shift < chunk:
        if reverse:
            x = x + jnp.where(pos < chunk - shift, pltpu.roll(x, seq - shift, 0), 0.0)
        else:
            x = x + jnp.where(pos >= shift, pltpu.roll(x, shift, 0), 0.0)
        shift *= 2
    return x


def _gla_kernel(*refs, seq, has_s0, write_state):
    it = iter(refs)
    q_ref, k_ref, v_ref, gg_ref, lo_ref, w2_ref, b2_ref, gn_ref = (next(it) for _ in range(8))
    s0_ref = next(it) if has_s0 else None
    mix_ref = next(it)
    st_ref = next(it) if write_state else None
    o_s, bc_s = next(it), next(it)
    n_chunks = seq // CHUNK
    lane = lax.broadcasted_iota(jnp.int32, (1, LANES), 1)
    head_mask = [(lane < GLA_DK).astype(F32), (lane >= GLA_DK).astype(F32)]
    row, col = _order_masks()
    before = [col <= row, col >= row]

    lo = lo_ref[...].astype(BF16)
    for d in range(2):
        z = _dot(lo, w2_ref[d].astype(BF16)) + b2_ref[d]
        bc_s[d] = _chunk_cumsum(-_softplus(-z) * (1.0 / GLA_TAU), CHUNK, d == 1)

    chains = [(d, p) for d in range(2) for p in range(2)]
    cols = [slice(p * GLA_DV, (p + 1) * GLA_DV) for p in range(2)]

    def body(i, carry):
        states = list(carry)
        pre = []
        for s in range(GLA_STEPS):
            step = i * GLA_STEPS + s
            rows = [pl.ds(pl.multiple_of(n * CHUNK, CHUNK), CHUNK) for n in (step, n_chunks - 1 - step)]
            qe, ke, kd, dec = [], [], [], []
            for d in range(2):
                bc = bc_s[d, rows[d], :]
                bl = bc[CHUNK - 1:CHUNK, :] if d == 0 else bc[0:1, :]
                k = k_ref[rows[d], :]
                qe.append(q_ref[rows[d], :] * (GLA_DK ** -0.5) * jnp.exp(bc))
                ke.append((k * jnp.exp(-bc)).astype(BF16))
                kd.append(k * jnp.exp(bl - bc))
                dec.append(jnp.exp(bl))
            qep = [(qe[d] * head_mask[p]).astype(BF16) for d, p in chains]
            kdp = [(kd[d] * head_mask[p]).astype(BF16) for d, p in chains]
            vp = [v_ref[rows[d], cols[p]].astype(BF16) for d, p in chains]
            sc = [jnp.where(before[d], _dot_nt(qep[c], ke[d]), 0.0).astype(BF16) for c, (d, p) in enumerate(chains)]
            kv = [_dot_tn(vp[c], kdp[c]) for c in range(4)]
            pre.append((rows, qep, vp, sc, kv, dec))
        for rows, qep, vp, sc, kv, dec in pre:
            inter = [_dot_nt(qep[c], states[c].astype(BF16)) for c in range(4)]
            for c, (d, p) in enumerate(chains):
                o_s[d, rows[d], cols[p]] = _dot(sc[c], vp[c]) + inter[c]
                states[c] = states[c] * dec[d] + kv[c]
        return tuple(states)

    init = []
    for d in range(2):
        if has_s0:
            s0t = s0_ref[d].T
            init += [s0t * head_mask[0], s0t * head_mask[1]]
        else:
            init += [jnp.zeros((GLA_DV, LANES), F32), jnp.zeros((GLA_DV, LANES), F32)]
    fin = lax.fori_loop(0, n_chunks // GLA_STEPS, body, tuple(init))
    if write_state:
        for d in range(2):
            st_ref[d] = (fin[2 * d] + fin[2 * d + 1]).T

    for p in range(2):
        cols = slice(p * GLA_DV, (p + 1) * GLA_DV)
        y = _rms(o_s[0, :, cols] + o_s[1, :, cols], gn_ref[...]) * _silu(gg_ref[:, cols])
        mix_ref[:, cols] = y.astype(mix_ref.dtype)


def _gla(proj, n_batch, seq, w2pad, b2, gn, s0, write_state):
    pairs = GLA_H // 2
    pw = 2 * GLA_DV
    st_spec = pl.BlockSpec((None, 2, None, LANES, GLA_DV), lambda b, p: (b, 0, p, 0, 0))
    in_specs = [
        pl.BlockSpec((seq, LANES), lambda b, p: (b, EV_Q // LANES + p)),
        pl.BlockSpec((seq, LANES), lambda b, p: (b, EV_K // LANES + p)),
        pl.BlockSpec((seq, pw), lambda b, p: (b, EV_V // pw + p)),
        pl.BlockSpec((seq, pw), lambda b, p: (b, EV_G // pw + p)),
        pl.BlockSpec((seq, LANES), lambda b, p: (b, EV_LO // LANES)),
        pl.BlockSpec((2, LANES, LANES), lambda b, p: (0, 0, p)),
        pl.BlockSpec((2, 1, LANES), lambda b, p: (0, 0, p)),
        pl.BlockSpec((1, GLA_DV), lambda b, p: (0, 0)),
    ]
    args = [proj, proj, proj, proj, proj, w2pad, b2, gn]
    if s0 is not None:
        in_specs.append(st_spec)
        args.append(s0)
    mix_spec = pl.BlockSpec((seq, pw), lambda b, p: (b, p))
    mix_shape = jax.ShapeDtypeStruct((n_batch * seq, GLA_H * GLA_DV), BF16)
    if write_state:
        out_specs = [mix_spec, st_spec]
        out_shape = [mix_shape, jax.ShapeDtypeStruct((n_batch, 2, pairs, LANES, GLA_DV), F32)]
    else:
        out_specs, out_shape = mix_spec, mix_shape
    return pl.pallas_call(
        functools.partial(_gla_kernel, seq=seq, has_s0=s0 is not None, write_state=write_state),
        grid=(n_batch, pairs),
        in_specs=in_specs,
        out_specs=out_specs,
        out_shape=out_shape,
        scratch_shapes=[pltpu.VMEM((2, seq, pw), F32), pltpu.VMEM((2, seq, LANES), F32)],
        compiler_params=_params("parallel", "parallel"),
        name="gla",
    )(*args)


def _conv_silu(x, w):
    seq = x.shape[0]
    t = lax.broadcasted_iota(jnp.int32, x.shape, 0)
    prev = jnp.where(t == 0, 0.0, pltpu.roll(x, 1, 0))
    nxt = jnp.where(t == seq - 1, 0.0, pltpu.roll(x, seq - 1, 0))
    return _silu(prev * w[0:1, :] + x * w[1:2, :] + nxt * w[2:3, :])


def _hi_lo(x):
    hi = x.astype(BF16)
    return hi, (x - hi.astype(F32)).astype(BF16)


def _split_lanes(a):
    hi, lo = _hi_lo(a)
    return jnp.concatenate([hi, hi, lo], axis=1)


def _split_rows(b):
    hi, lo = _hi_lo(b)
    return jnp.concatenate([hi, lo, hi], axis=0)


def _l2norm(x):
    return x * lax.rsqrt(jnp.sum(x * x, axis=-1, keepdims=True) + EPS)


def _delta_kernel(*refs, seq, hp, cpi, has_s0, write_state):
    it = iter(refs)
    q_ref, k_ref, v_ref, z_ref, gates_ref, wq_ref, wk_ref, wv_ref, gn_ref = (next(it) for _ in range(9))
    s0_ref = next(it) if has_s0 else None
    mix_ref = next(it)
    st_ref = next(it) if write_state else None
    qs, ks, vs, o_s, gc_s, b_s, u0_s, kc_s, at_s, qe_s, kd_s = (next(it) for _ in range(11))
    n_chunks = seq // DN_CHUNK
    lane = lax.broadcasted_iota(jnp.int32, (1, LANES), 1)
    row, col = _order_masks(DN_CHUNK)
    eye = (row == col).astype(F32)
    before = [col <= row, col >= row]
    strict = [col < row, col > row]
    couples = [((row >> (lvl + 1)) == (col >> (lvl + 1))) & ((row >> lvl) != (col >> lvl))
               for lvl in range(DN_CHUNK.bit_length() - 1)]

    gates = gates_ref[...]
    for hd in range(hp):
        cols = slice(hd * HEAD_DIM, (hd + 1) * HEAD_DIM)
        h = pl.program_id(1) * hp + hd
        qs[hd] = _l2norm(_conv_silu(q_ref[:, cols], wq_ref[:, cols])) * (HEAD_DIM ** -0.5)
        ks[hd] = _l2norm(_conv_silu(k_ref[:, cols], wk_ref[:, cols]))
        vs[hd] = _conv_silu(v_ref[:, cols], wv_ref[:, cols])
        for d in range(2):
            sel_g = lane == d * DN_H + h
            sel_b = lane == 2 * DN_H + d * DN_H + h
            gc_s[hd, d] = jnp.broadcast_to(
                jnp.sum(jnp.where(sel_g, gates, 0.0), axis=-1, keepdims=True), (seq, LANES))
            b_s[hd, d] = jnp.broadcast_to(
                jnp.sum(jnp.where(sel_b, gates, 0.0), axis=-1, keepdims=True), (seq, LANES))

    def chunk_rows(n):
        return pl.ds(pl.multiple_of(n * DN_CHUNK, DN_CHUNK), DN_CHUNK)

    def last_row(d, n):
        return pl.ds(n * DN_CHUNK + (DN_CHUNK - 1 if d == 0 else 0), 1)

    def wy_stages(units):
        nu = range(len(units))
        rows = [chunk_rows(n) for _, _, n in units]
        gc = [gc_s[hd, d, rows[i], :] for i, (hd, d, _) in enumerate(units)]
        bb = [b_s[hd, d, rows[i], :] for i, (hd, d, _) in enumerate(units)]
        k = [ks[hd, rows[i], :] for i, (hd, _, _) in enumerate(units)]
        k16 = [k[i].astype(BF16) for i in nu]
        kbeta = [k[i] * bb[i] for i in nu]
        kk = [_dot_nt(kbeta[i].astype(BF16), k16[i]) for i in nu]
        qk = [_dot_nt(qs[hd, rows[i], :].astype(BF16), k16[i]) for i, (hd, _, _) in enumerate(units)]
        yield
        mm = []
        for i, (hd, d, _) in enumerate(units):
            decay = jnp.where(before[d], jnp.exp(jnp.where(before[d], gc[i] - gc[i].T, 0.0)), 0.0)
            mm.append(jnp.where(strict[d], kk[i] * decay, 0.0))
            at_s[hd, d, rows[i], :] = (qk[i] * decay).astype(BF16)
        inv = [eye - jnp.where(couples[0], mm[i], 0.0) for i in nu]
        for lvl in range(1, len(couples)):
            inv16 = [inv[i].astype(BF16) for i in nu]
            c_inv = [_dot(jnp.where(couples[lvl], mm[i], 0.0).astype(BF16), inv16[i]).astype(BF16) for i in nu]
            yield
            inv = [inv[i] - _dot(inv16[i], c_inv[i]) for i in nu]
            yield
        inv16 = [inv[i].astype(BF16) for i in nu]
        egc = [jnp.exp(gc[i]) for i in nu]
        rhs = [jnp.concatenate([vs[hd, rows[i], :] * bb[i], kbeta[i] * egc[i]], axis=-1)
               for i, (hd, _, _) in enumerate(units)]
        sol = [_dot(inv16[i], rhs[i].astype(BF16)) for i in nu]
        yield
        mm_split = [_split_lanes(mm[i]) for i in nu]
        for _ in range(DN_REFINE_STEPS):
            res = [rhs[i] - sol[i] - _dot(mm_split[i], _split_rows(sol[i])) for i in nu]
            yield
            sol = [sol[i] + _dot(inv16[i], res[i].astype(BF16)) for i in nu]
            yield
        for i, (hd, d, _) in enumerate(units):
            u0_s[hd, d, rows[i], :] = sol[i][:, :HEAD_DIM]
            kc_s[hd, d, rows[i], :] = sol[i][:, HEAD_DIM:].astype(BF16)
            g_last = gc[i][DN_CHUNK - 1:DN_CHUNK, :] if d == 0 else gc[i][0:1, :]
            qe_s[hd, d, rows[i], :] = (qs[hd, rows[i], :] * egc[i]).astype(BF16)
            kd_s[hd, d, rows[i], :] = (k[i] * jnp.exp(g_last - gc[i])).astype(BF16)

    chains = [(hd, d) for hd in range(hp) for d in range(2)]

    def chunk_of(d, step):
        return step if d == 0 else n_chunks - 1 - step

    def scan_stages(state, steps):
        for step in steps:
            ns = [chunk_of(d, step) for _, d in chains]
            rows = [chunk_rows(n) for n in ns]
            s16 = [st.astype(BF16) for st in state]
            ks_ = [_dot(kc_s[hd, d, rows[c], :], s16[c]) for c, (hd, d) in enumerate(chains)]
            qs_ = [_dot(qe_s[hd, d, rows[c], :], s16[c]) for c, (hd, d) in enumerate(chains)]
            yield
            u16 = [(u0_s[hd, d, rows[c], :] - ks_[c]).astype(BF16) for c, (hd, d) in enumerate(chains)]
            au = [_dot(at_s[hd, d, rows[c], :], u16[c]) for c, (hd, d) in enumerate(chains)]
            ku = [_dot_tn(kd_s[hd, d, rows[c], :], u16[c]) for c, (hd, d) in enumerate(chains)]
            yield
            for c, (hd, d) in enumerate(chains):
                o_s[hd, d, rows[c], :] = qs_[c] + au[c]
                state[c] = state[c] * jnp.exp(gc_s[hd, d, last_row(d, ns[c]), :]) + ku[c]

    def run(*stages):
        live = list(stages)
        while live:
            for g in list(live):
                try:
                    next(g)
                except StopIteration:
                    live.remove(g)

    def group_units(j):
        return [(hd, d, chunk_of(d, j * cpi + c)) for c in range(cpi) for hd in range(hp) for d in range(2)]

    def group_steps(j):
        return [j * cpi + c for c in range(cpi)]

    n_groups = n_chunks // cpi
    if has_s0:
        init = tuple(s0_ref[d, hd] for hd, d in chains)
    else:
        init = tuple(jnp.zeros((HEAD_DIM, HEAD_DIM), F32) for _ in chains)
    run(wy_stages(group_units(0)))

    def body(j, carry):
        state = list(carry)
        run(wy_stages(group_units(j)), scan_stages(state, group_steps(j - 1)))
        return tuple(state)

    fin = list(lax.fori_loop(1, n_groups, body, init))
    run(scan_stages(fin, group_steps(n_groups - 1)))
    if write_state:
        for c, (hd, d) in enumerate(chains):
            st_ref[d, hd] = fin[c]

    for hd in range(hp):
        cols = slice(hd * HEAD_DIM, (hd + 1) * HEAD_DIM)
        y = _rms(o_s[hd, 0] + o_s[hd, 1], gn_ref[...]) * _silu(z_ref[:, cols])
        mix_ref[:, cols] = y.astype(mix_ref.dtype)


def _dn_gates_kernel(ab_ref, alog_ref, dtb_ref, o_ref):
    ab = ab_ref[...]
    lane = lax.broadcasted_iota(jnp.int32, (1, LANES), 1)
    g = -jnp.exp(alog_ref[...]) * _softplus(ab + dtb_ref[...])
    fwd = _chunk_cumsum(g, DN_CHUNK, False)
    bwd = _chunk_cumsum(g, DN_CHUNK, True)
    o_ref[...] = jnp.where(lane < DN_H, fwd, jnp.where(lane < 2 * DN_H, bwd, _sigmoid(ab)))


def _dn_gates(proj, n_batch, seq, alog_row, dtb_row):
    vec = pl.BlockSpec((1, LANES), lambda b: (0, 0))
    return pl.pallas_call(
        _dn_gates_kernel,
        grid=(n_batch,),
        in_specs=[pl.BlockSpec((seq, LANES), lambda b: (b, OD_AB // LANES)), vec, vec],
        out_specs=pl.BlockSpec((seq, LANES), lambda b: (b, 0)),
        out_shape=jax.ShapeDtypeStruct((n_batch * seq, LANES), F32),
        compiler_params=_params("parallel"),
        name="dn_gates",
    )(proj, alog_row, dtb_row)


def _delta(proj, gates, n_batch, seq, conv_w, gn, s0, write_state):
    n_chunks = seq // DN_CHUNK
    hp = max(1, min(DN_H, DN_UNITS // (2 * n_chunks)))
    cpi = DN_UNITS // (2 * hp)
    hw = hp * HEAD_DIM

    def col_spec(col0):
        return pl.BlockSpec((seq, hw), lambda b, h: (b, col0 // hw + h))

    def conv_spec(part):
        return pl.BlockSpec((3, hw), lambda b, h: (0, part * (DN_H // hp) + h))

    vec = pl.BlockSpec((1, LANES), lambda b, h: (0, 0))
    st_spec = pl.BlockSpec((None, 2, hp, HEAD_DIM, HEAD_DIM), lambda b, h: (b, 0, h, 0, 0))
    in_specs = [col_spec(OD_DQ), col_spec(OD_DK), col_spec(OD_DV), col_spec(OD_DZ),
                pl.BlockSpec((seq, LANES), lambda b, h: (b, 0)),
                conv_spec(0), conv_spec(1), conv_spec(2), vec]
    args = [proj, proj, proj, proj, gates, conv_w, conv_w, conv_w, gn]
    if s0 is not None:
        in_specs.append(st_spec)
        args.append(s0)
    mix_spec = pl.BlockSpec((seq, hw), lambda b, h: (b, h))
    mix_shape = jax.ShapeDtypeStruct((n_batch * seq, DN_H * HEAD_DIM), BF16)
    if write_state:
        out_specs = [mix_spec, st_spec]
        out_shape = [mix_shape, jax.ShapeDtypeStruct((n_batch, 2, DN_H, HEAD_DIM, HEAD_DIM), F32)]
    else:
        out_specs, out_shape = mix_spec, mix_shape
    tok = pltpu.VMEM((hp, seq, HEAD_DIM), F32)
    both = pltpu.VMEM((hp, 2, seq, HEAD_DIM), F32)
    both16 = pltpu.VMEM((hp, 2, seq, HEAD_DIM), BF16)
    return pl.pallas_call(
        functools.partial(_delta_kernel, seq=seq, hp=hp, cpi=cpi, has_s0=s0 is not None, write_state=write_state),
        grid=(n_batch, DN_H // hp),
        in_specs=in_specs,
        out_specs=out_specs,
        out_shape=out_shape,
        scratch_shapes=[tok, tok, tok, both, both, both, both, both16, both16, both16, both16],
        compiler_params=_params("parallel", "arbitrary"),
        name="deltanet",
    )(*args)


def _rope_tables(n_tok):
    t = jnp.arange(n_tok)
    inv = 1.0 / (ROPE_THETA ** (jnp.arange(ROPE_FREQ, dtype=F32) / ROPE_FREQ))
    pos = jnp.stack([t // GRID_W, t % GRID_W], axis=1).astype(F32)
    ang = pos[:, :, None] * inv
    cos = jnp.concatenate([jnp.cos(ang), jnp.cos(ang)], axis=-1).reshape(n_tok, HEAD_DIM)
    sin = jnp.concatenate([-jnp.sin(ang), jnp.sin(ang)], axis=-1).reshape(n_tok, HEAD_DIM)
    return cos, sin


def _pad_lanes(v):
    return jnp.pad(v.reshape(1, -1), ((0, 0), (0, LANES - v.size)))


def kernel(x_prompt, x_sample, state_gla, cache_gqa_k, cache_gqa_v, cache_na_k, cache_na_v, state_delta, c, c_ctx, norm1, norm2, w_ada, b_ada, w_mlp1, w_mlp2, ev_w_in, ev_w_a2, ev_b_a2, ev_gla_norm, ev_q_norm, ev_k_norm, ev_w_out, od_w_in, od_conv, od_a_log, od_dt_bias, od_dn_norm, od_rpb, od_w_out, norm_f):
    n_ctx, seq_ctx, d = x_prompt.shape
    n_lat, seq_lat, _ = x_sample.shape
    depth = w_ada.shape[0]

    w_ev = ev_w_in[0]
    w_ev = jnp.concatenate(
        [w_ev[:, :EV_AQ], w_ev[:, EV_AQ + 2 * GLA_RANK:], w_ev[:, EV_AQ:EV_AQ + 2 * GLA_RANK],
         jnp.zeros((d, EV_COLS_PAD - w_ev.shape[1]), F32)], axis=1).astype(BF16)
    w_od = jnp.pad(od_w_in[0], ((0, 0), (0, OD_COLS_PAD - od_w_in.shape[2]))).astype(BF16)
    w_in = [w_ev, w_od]
    w_out = [ev_w_out[0].astype(BF16), od_w_out[0].astype(BF16)]
    w1 = w_mlp1.astype(BF16)
    w2 = w_mlp2.astype(BF16)
    w2pad = jnp.zeros((2, LANES, GLA_H * GLA_DK), F32)
    for dd in range(2):
        w2pad = w2pad.at[dd, dd * GLA_RANK:(dd + 1) * GLA_RANK].set(ev_w_a2[0, dd])
    b2 = ev_b_a2[0].reshape(2, 1, GLA_H * GLA_DK)
    alog_row = _pad_lanes(od_a_log[0])
    dtb_row = _pad_lanes(od_dt_bias[0])
    na_bias = _na_bias_table(od_rpb[0])
    rope = _rope_tables(seq_lat)

    cond8 = jnp.concatenate([c_ctx[None, :], c, jnp.zeros((8 - 1 - n_lat, d), F32)], axis=0)
    mod = _adaln(cond8, w_ada, b_ada).reshape(depth, 8, 6, 1, d)

    def trunk(x, n_batch, seq, row_fn, caches):
        latent = caches is not None
        tm = TM_MLP
        outs = {}
        proj = _inproj(x, norm1[0:1], mod, 0, row_fn, w_in[0], TM_INPROJ)
        if latent:
            s0 = caches["gla"].reshape(n_batch, 2, GLA_H // 2, LANES, GLA_DV)
            mix_a = _gla(proj, n_batch, seq, w2pad, b2, ev_gla_norm, s0, False)
            mix_b = _attention(proj, n_batch, seq, EV_AQ, EV_AK, EV_AV, GQA_KV, GQA_H // GQA_KV,
                               norms=(ev_q_norm, ev_k_norm), rope=rope, ctx=caches["gqa"])
        else:
            mix_a, st = _gla(proj, n_batch, seq, w2pad, b2, ev_gla_norm, None, True)
            mix_b, k_new = _attention(proj, n_batch, seq, EV_AQ, EV_AK, EV_AV, GQA_KV, GQA_H // GQA_KV,
                                      norms=(ev_q_norm, ev_k_norm), write_k=True)
            outs["st_gla"] = st.reshape(n_batch, 1, 2, GLA_H, GLA_DK, GLA_DV)
            outs["ck_gqa"] = k_new.reshape(n_batch, 1, seq, GQA_KV, HEAD_DIM)
            outs["cv_gqa"] = proj[:, EV_AV:EV_AV + GQA_KV * HEAD_DIM].reshape(n_batch, 1, seq, GQA_KV, HEAD_DIM)
        x = _outproj(mix_a, mix_b, w_out[0], x, mod, 0, row_fn, tm)
        x = _mlp(x, norm2[0:1], mod, 0, row_fn, w1, w2, norm_f[None, :], False, tm)
        proj = _inproj(x, norm1[1:2], mod, 1, row_fn, w_in[1], TM_INPROJ)
        if latent:
            mix_a = _neighbourhood(proj, n_batch, seq, caches["na"][0], caches["na"][1], na_bias)
            s0 = caches["delta"].reshape(n_batch, 2, DN_H, HEAD_DIM, HEAD_DIM)
            gates = _dn_gates(proj, n_batch, seq, alog_row, dtb_row)
            mix_b = _delta(proj, gates, n_batch, seq, od_conv[0], od_dn_norm, s0, False)
        else:
            mix_a = _attention(proj, n_batch, seq, OD_NQ, OD_NK, OD_NV, NA_H, 1)
            gates = _dn_gates(proj, n_batch, seq, alog_row, dtb_row)
            mix_b, st = _delta(proj, gates, n_batch, seq, od_conv[0], od_dn_norm, None, True)
            outs["ck_na"] = proj[:, OD_NK:OD_NK + NA_H * HEAD_DIM].reshape(n_batch, 1, seq, NA_H, HEAD_DIM)
            outs["cv_na"] = proj[:, OD_NV:OD_NV + NA_H * HEAD_DIM].reshape(n_batch, 1, seq, NA_H, HEAD_DIM)
            outs["st_dn"] = st.reshape(n_batch, 1, 2, DN_H, HEAD_DIM, HEAD_DIM)
        x = _outproj(mix_a, mix_b, w_out[1], x, mod, 1, row_fn, tm)
        y = _mlp(x, norm2[1:2], mod, 1, row_fn, w1, w2, norm_f[None, :], True, tm)
        return y.reshape(n_batch, seq, d), outs

    y_prompt, new = trunk(x_prompt.reshape(n_ctx * seq_ctx, d), n_ctx, seq_ctx, lambda tok: 0, None)
    past = cache_gqa_k.shape[2]
    caches = {
        "gla": state_gla[:, 0],
        "gqa": (cache_gqa_k[:, 0].reshape(n_lat, past, GQA_KV * HEAD_DIM),
                cache_gqa_v[:, 0].reshape(n_lat, past, GQA_KV * HEAD_DIM)),
        "na": (cache_na_k[:, 0].reshape(n_lat, past, NA_H * HEAD_DIM),
               cache_na_v[:, 0].reshape(n_lat, past, NA_H * HEAD_DIM)),
        "delta": state_delta[:, 0],
    }
    y_sample, _ = trunk(x_sample.reshape(n_lat * seq_lat, d), n_lat, seq_lat,
                        lambda tok: 1 + tok // seq_lat, caches)
    return (y_prompt, y_sample, new["st_gla"], new["ck_gqa"], new["cv_gqa"], new["ck_na"], new["cv_na"],
            new["st_dn"])
```

```python
import functools

import jax
import jax.numpy as jnp
import numpy as np
from jax import lax
from jax.experimental import pallas as pl
from jax.experimental.pallas import tpu as pltpu

F32 = jnp.float32
BF16 = jnp.bfloat16
HIGHEST = lax.Precision.HIGHEST

D_MODEL = 2048
D_FF = 4 * D_MODEL
HEAD_DIM = 128
LANES = 128
GRID_W = 64
GLA_H = 8
GLA_DK = 64
GLA_DV = 128
GLA_RANK = 16
GLA_TAU = 16.0
GQA_H = 8
GQA_KV = 2
NA_H = 8
NA_WIN_R = 8
NA_WIN_C = 16
DN_H = 8
CHUNK = 64
DN_CHUNK = 128
DN_REFINE_STEPS = 1
DN_UNITS = 16
TM_INPROJ = 1024
TN_INPROJ = 1280
TM_MLP = 512
TF_MLP = 1024
ATTN_HEADS_PER_STEP = 8
GLA_STEPS = 4
NA_ROWS_PER_STEP = 16
ROPE_THETA = 10000.0
ROPE_FREQ = HEAD_DIM // 4
EPS = 1e-6
MASK_VALUE = -1e30

EV_Q, EV_K, EV_V, EV_G = 0, 512, 1024, 2048
EV_AQ, EV_AK, EV_AV, EV_LO = 3072, 4096, 4352, 4608
EV_COLS_PAD = 5120
OD_NQ, OD_NK, OD_NV = 0, 1024, 2048
OD_DQ, OD_DK, OD_DV, OD_DZ, OD_AB = 3072, 4096, 5120, 6144, 7168
OD_COLS_PAD = 7680

VMEM_LIMIT = 56 * 1024 * 1024


def _params(*sem):
    return pltpu.CompilerParams(dimension_semantics=sem, vmem_limit_bytes=VMEM_LIMIT)


def _dot(a, b):
    return jnp.dot(a, b, preferred_element_type=F32)


def _dot_nt(a, b):
    return lax.dot_general(a, b, (((1,), (1,)), ((), ())), preferred_element_type=F32)


def _dot_tn(a, b):
    return lax.dot_general(a, b, (((0,), (0,)), ((), ())), preferred_element_type=F32)


def _dot_hi(a, b):
    return jnp.dot(a, b, precision=HIGHEST, preferred_element_type=F32)


def _sigmoid(x):
    return 1.0 / (1.0 + jnp.exp(-x))


def _silu(x):
    return x * _sigmoid(x)


def _softplus(x):
    return jnp.maximum(x, 0.0) + jnp.log(1.0 + jnp.exp(-jnp.abs(x)))


def _rms(x, g):
    return x * lax.rsqrt(jnp.mean(x * x, axis=-1, keepdims=True) + EPS) * g


def _rms_mod(x, g, sc, sh):
    return _rms(x, g) * (1.0 + sc) + sh


def _adaln_kernel(c_ref, w_ref, b_ref, o_ref):
    a = _silu(c_ref[...]).astype(BF16)
    o_ref[...] = _dot(a, w_ref[...].astype(BF16)) + b_ref[...]


def _adaln(cond8, w_ada, b_ada):
    depth, d, n = w_ada.shape
    tn = 512
    return pl.pallas_call(
        _adaln_kernel,
        grid=(depth, n // tn),
        in_specs=[
            pl.BlockSpec((8, d), lambda l, j: (0, 0)),
            pl.BlockSpec((None, d, tn), lambda l, j: (l, 0, j)),
            pl.BlockSpec((None, 1, tn), lambda l, j: (l, 0, j)),
        ],
        out_specs=pl.BlockSpec((None, 8, tn), lambda l, j: (l, 0, j)),
        out_shape=jax.ShapeDtypeStruct((depth, 8, n), F32),
        compiler_params=_params("parallel", "parallel"),
        name="adaln",
    )(cond8, w_ada, b_ada.reshape(depth, 1, n))


def _mod_spec(layer, k, row_fn, tm):
    return pl.BlockSpec((None, None, None, 1, D_MODEL), lambda i, j: (layer, row_fn(i * tm), k, 0, 0))


def _inproj_kernel(x_ref, g_ref, sc_ref, sh_ref, w_ref, o_ref, h_ref):
    @pl.when(pl.program_id(1) == 0)
    def _():
        h_ref[...] = _rms_mod(x_ref[...], g_ref[...], sc_ref[...], sh_ref[...]).astype(BF16)

    o_ref[...] = _dot(h_ref[...], w_ref[...])


def _inproj(x, g, mod, layer, row_fn, w, tm):
    m, d = x.shape
    n = w.shape[1]
    tn = TN_INPROJ
    return pl.pallas_call(
        _inproj_kernel,
        grid=(m // tm, n // tn),
        in_specs=[
            pl.BlockSpec((tm, d), lambda i, j: (i, 0)),
            pl.BlockSpec((1, d), lambda i, j: (0, 0)),
            _mod_spec(layer, 1, row_fn, tm),
            _mod_spec(layer, 0, row_fn, tm),
            pl.BlockSpec((d, tn), lambda i, j: (0, j)),
        ],
        out_specs=pl.BlockSpec((tm, tn), lambda i, j: (i, j)),
        out_shape=jax.ShapeDtypeStruct((m, n), F32),
        scratch_shapes=[pltpu.VMEM((tm, d), BF16)],
        compiler_params=_params("parallel", "arbitrary"),
        name="inproj",
    )(x, g, mod, mod, w)


def _outproj_kernel(a_ref, b_ref, wa_ref, wb_ref, x_ref, gt_ref, g_ref, sc_ref, sh_ref, o_ref, h_ref):
    acc = _dot(a_ref[...], wa_ref[...]) + _dot(b_ref[...], wb_ref[...])
    y = x_ref[...] + gt_ref[...] * acc
    o_ref[...] = y
    h_ref[...] = _rms_mod(y, g_ref[...], sc_ref[...], sh_ref[...]).astype(BF16)


def _outproj(mix_a, mix_b, w, x, g2, mod, layer, row_fn, tm):
    m, d = x.shape
    ka, kb = mix_a.shape[1], mix_b.shape[1]
    row = pl.BlockSpec((tm, d), lambda i, j: (i, 0))
    return pl.pallas_call(
        _outproj_kernel,
        grid=(m // tm, 1),
        in_specs=[
            pl.BlockSpec((tm, ka), lambda i, j: (i, 0)),
            pl.BlockSpec((tm, kb), lambda i, j: (i, 0)),
            pl.BlockSpec((ka, d), lambda i, j: (0, 0)),
            pl.BlockSpec((kb, d), lambda i, j: (1, 0)),
            row,
            _mod_spec(layer, 2, row_fn, tm),
            pl.BlockSpec((1, d), lambda i, j: (0, 0)),
            _mod_spec(layer, 4, row_fn, tm),
            _mod_spec(layer, 3, row_fn, tm),
        ],
        out_specs=[row, row],
        out_shape=[jax.ShapeDtypeStruct((m, d), F32), jax.ShapeDtypeStruct((m, d), BF16)],
        compiler_params=_params("parallel", "arbitrary"),
        name="outproj",
    )(mix_a, mix_b, w, w, x, mod, g2, mod, mod)


def _mlp_kernel(x_ref, h_ref, gt_ref, w1_ref, w2_ref, gf_ref, o_ref, acc_ref, *, final_norm):
    j = pl.program_id(1)

    @pl.when(j == 0)
    def _():
        acc_ref[...] = jnp.zeros_like(acc_ref)

    hid = jnp.maximum(_dot(h_ref[...], w1_ref[...]), 0.0)
    acc_ref[...] += _dot((hid * hid).astype(BF16), w2_ref[...])

    @pl.when(j == pl.num_programs(1) - 1)
    def _():
        y = x_ref[...] + gt_ref[...] * acc_ref[...]
        if final_norm:
            y = _rms(y, gf_ref[...])
        o_ref[...] = y


def _mlp(x, h, mod, layer, row_fn, w1, w2, gf, final_norm, tm):
    m, d = x.shape
    ff = w1.shape[2]
    tf = TF_MLP
    return pl.pallas_call(
        functools.partial(_mlp_kernel, final_norm=final_norm),
        grid=(m // tm, ff // tf),
        in_specs=[
            pl.BlockSpec((tm, d), lambda i, j: (i, 0)),
            pl.BlockSpec((tm, d), lambda i, j: (i, 0)),
            _mod_spec(layer, 5, row_fn, tm),
            pl.BlockSpec((None, d, tf), lambda i, j: (layer, 0, j)),
            pl.BlockSpec((None, tf, d), lambda i, j: (layer, j, 0)),
            pl.BlockSpec((1, d), lambda i, j: (0, 0)),
        ],
        out_specs=pl.BlockSpec((tm, d), lambda i, j: (i, 0)),
        out_shape=jax.ShapeDtypeStruct((m, d), F32),
        scratch_shapes=[pltpu.VMEM((tm, d), F32)],
        compiler_params=_params("parallel", "arbitrary"),
        name="mlp",
    )(x, h, mod, w1, w2, gf)


def _rope(x, cos, sin):
    lane = lax.broadcasted_iota(jnp.int32, x.shape, 1)
    first_half = (lane % (2 * ROPE_FREQ)) < ROPE_FREQ
    partner = jnp.where(first_half, pltpu.roll(x, LANES - ROPE_FREQ, 1), pltpu.roll(x, ROPE_FREQ, 1))
    return x * cos + partner * sin


def _attn_kernel(*refs, nkv, group, use_norm, use_rope, has_ctx, write_k):
    it = iter(refs)
    q_ref, k_ref, v_ref = next(it), next(it), next(it)
    qn_ref = kn_ref = cq_ref = sq_ref = ck_ref = sk_ref = kc_ref = vc_ref = knew_ref = None
    if use_norm:
        qn_ref, kn_ref = next(it), next(it)
    if use_rope:
        cq_ref, sq_ref, ck_ref, sk_ref = next(it), next(it), next(it), next(it)
    if has_ctx:
        kc_ref, vc_ref = next(it), next(it)
    o_ref = next(it)
    if write_k:
        knew_ref = next(it)
    kbuf, vbuf = next(it), next(it)
    scale = HEAD_DIM ** -0.5 * float(np.log2(np.e))

    kcols = [slice(kv * HEAD_DIM, (kv + 1) * HEAD_DIM) for kv in range(nkv)]

    @pl.when(pl.program_id(2) == 0)
    def _():
        for kv in range(nkv):
            k = k_ref[:, kcols[kv]]
            if use_norm:
                k = _rms(k, kn_ref[...])
            if write_k:
                knew_ref[:, kcols[kv]] = k
            if use_rope:
                k = _rope(k, ck_ref[...], sk_ref[...])
            kbuf[:, kcols[kv]] = k.astype(BF16)
        vbuf[...] = v_ref[...].astype(BF16)

    heads = range(nkv * group)
    kv_of = [h // group for h in heads]
    cols = [slice(h * HEAD_DIM, (h + 1) * HEAD_DIM) for h in heads]
    qb = []
    for h in heads:
        q = q_ref[:, cols[h]]
        if use_norm:
            q = _rms(q, qn_ref[...])
        if use_rope:
            q = _rope(q, cq_ref[...], sq_ref[...])
        qb.append((q * scale).astype(BF16))
    s = [_dot_nt(qb[h], kbuf[:, kcols[kv_of[h]]]) for h in heads]
    m = [jnp.max(s[h], axis=-1, keepdims=True) for h in heads]
    if has_ctx:
        kc16 = kc_ref[...].astype(BF16)
        s_c = [_dot_nt(qb[h], kc16[:, kcols[kv_of[h]]]) for h in heads]
        m = [jnp.maximum(m[h], jnp.max(s_c[h], axis=-1, keepdims=True)) for h in heads]
    p = [jnp.exp2(s[h] - m[h]) for h in heads]
    l = [jnp.sum(p[h], axis=-1, keepdims=True) for h in heads]
    o = [_dot(p[h].astype(BF16), vbuf[:, kcols[kv_of[h]]]) for h in heads]
    if has_ctx:
        vc16 = vc_ref[...].astype(BF16)
        p_c = [jnp.exp2(s_c[h] - m[h]) for h in heads]
        l = [l[h] + jnp.sum(p_c[h], axis=-1, keepdims=True) for h in heads]
        o = [o[h] + _dot(p_c[h].astype(BF16), vc16[:, kcols[kv_of[h]]]) for h in heads]
    for h in heads:
        o_ref[:, cols[h]] = (o[h] / l[h]).astype(o_ref.dtype)


def _attention(proj, n_batch, seq, q_col, k_col, v_col, n_kv, group, norms=None, rope=None, ctx=None,
               write_k=False):
    tq = min(seq, 256)
    nq = seq // tq
    nkv = max(1, min(n_kv, ATTN_HEADS_PER_STEP // group)) if nq == 1 else 1
    gw = nkv * group * HEAD_DIM
    kw = nkv * HEAD_DIM
    in_specs = [
        pl.BlockSpec((tq, gw), lambda b, h, i: (b * nq + i, q_col // gw + h)),
        pl.BlockSpec((seq, kw), lambda b, h, i: (b, k_col // kw + h)),
        pl.BlockSpec((seq, kw), lambda b, h, i: (b, v_col // kw + h)),
    ]
    args = [proj, proj, proj]
    vec = pl.BlockSpec((1, HEAD_DIM), lambda b, h, i: (0, 0))
    if norms is not None:
        in_specs += [vec, vec]
        args += [norms[0], norms[1]]
    if rope is not None:
        tab_q = pl.BlockSpec((tq, HEAD_DIM), lambda b, h, i: (i, 0))
        tab_k = pl.BlockSpec((seq, HEAD_DIM), lambda b, h, i: (0, 0))
        in_specs += [tab_q, tab_q, tab_k, tab_k]
        args += [rope[0], rope[1], rope[0], rope[1]]
    if ctx is not None:
        past = ctx[0].shape[1]
        c_spec = pl.BlockSpec((None, past, kw), lambda b, h, i: (b, 0, h))
        in_specs += [c_spec, c_spec]
        args += [ctx[0], ctx[1]]
    o_spec = pl.BlockSpec((tq, gw), lambda b, h, i: (b * nq + i, h))
    o_shape = jax.ShapeDtypeStruct((n_batch * seq, n_kv * group * HEAD_DIM), BF16)
    if write_k:
        out_specs = [o_spec, pl.BlockSpec((seq, kw), lambda b, h, i: (b, h))]
        out_shape = [o_shape, jax.ShapeDtypeStruct((n_batch * seq, n_kv * HEAD_DIM), F32)]
    else:
        out_specs, out_shape = o_spec, o_shape
    return pl.pallas_call(
        functools.partial(_attn_kernel, nkv=nkv, group=group, use_norm=norms is not None, use_rope=rope is not None,
                          has_ctx=ctx is not None, write_k=write_k),
        grid=(n_batch, n_kv // nkv, nq),
        in_specs=in_specs,
        out_specs=out_specs,
        out_shape=out_shape,
        scratch_shapes=[pltpu.VMEM((seq, kw), BF16), pltpu.VMEM((seq, kw), BF16)],
        compiler_params=_params("parallel", "parallel", "arbitrary"),
        name="attention",
    )(*args)


def _na_row_start(r, rows):
    return jnp.clip(r - NA_WIN_R // 2, 0, rows - NA_WIN_R)


def _na_kernel(q_ref, k_ref, v_ref, kc_ref, vc_ref, bias_ref, o_ref, kb, vb, kcb, vcb, *, rows):
    win = NA_WIN_R * GRID_W
    scale = HEAD_DIM ** -0.5
    kb[...] = k_ref[...].astype(BF16)
    vb[...] = v_ref[...].astype(BF16)
    kcb[...] = kc_ref[...].astype(BF16)
    vcb[...] = vc_ref[...].astype(BF16)

    def body(j, carry):
        rr = [j * NA_ROWS_PER_STEP + i for i in range(NA_ROWS_PER_STEP)]
        rs = [_na_row_start(r, rows) for r in rr]
        k_rows = [pl.ds(pl.multiple_of(s * GRID_W, GRID_W), win) for s in rs]
        q_rows = [pl.ds(pl.multiple_of(r * GRID_W, GRID_W), GRID_W) for r in rr]
        qb = [q_ref[q_rows[i], :].astype(BF16) for i in range(NA_ROWS_PER_STEP)]
        s_w = [_dot_nt(qb[i], kb[k_rows[i], :]) * scale for i in range(NA_ROWS_PER_STEP)]
        s_c = [_dot_nt(qb[i], kcb[...]) * scale for i in range(NA_ROWS_PER_STEP)]
        p_w, p_c, l = [], [], []
        for i in range(NA_ROWS_PER_STEP):
            delta = rr[i] - rs[i]
            bias = jnp.concatenate(
                [bias_ref[2 * jj - delta + NA_WIN_R - 1] for jj in range(NA_WIN_R // 2)], axis=-1)
            sw = s_w[i] + bias
            m = jnp.maximum(jnp.max(sw, axis=-1, keepdims=True), jnp.max(s_c[i], axis=-1, keepdims=True))
            pw = jnp.exp(sw - m)
            pc = jnp.exp(s_c[i] - m)
            l.append(jnp.sum(pw, axis=-1, keepdims=True) + jnp.sum(pc, axis=-1, keepdims=True))
            p_w.append(pw.astype(BF16))
            p_c.append(pc.astype(BF16))
        o_w = [_dot(p_w[i], vb[k_rows[i], :]) for i in range(NA_ROWS_PER_STEP)]
        o_c = [_dot(p_c[i], vcb[...]) for i in range(NA_ROWS_PER_STEP)]
        for i in range(NA_ROWS_PER_STEP):
            o_ref[q_rows[i], :] = ((o_w[i] + o_c[i]) / l[i]).astype(o_ref.dtype)
        return carry

    lax.fori_loop(0, rows // NA_ROWS_PER_STEP, body, 0)


def _na_bias_table(rpb):
    cols = jnp.arange(GRID_W)
    start = jnp.clip(cols - NA_WIN_C // 2, 0, GRID_W - NA_WIN_C)
    inside = (cols[None, :] >= start[:, None]) & (cols[None, :] < start[:, None] + NA_WIN_C)
    rel = cols[None, :] - cols[:, None] + NA_WIN_C - 1
    onehot = (rel[:, :, None] == jnp.arange(2 * NA_WIN_C - 1)).astype(F32)
    picked = jnp.einsum("hrm,ckm->hrck", rpb, onehot, precision=HIGHEST)
    tab = jnp.where(inside[None, None], picked, MASK_VALUE)
    return jnp.concatenate([tab[:, :-1], tab[:, 1:]], axis=-1)


def _neighbourhood(proj, n_batch, seq, k_ctx, v_ctx, bias):
    rows = seq // GRID_W
    past = k_ctx.shape[1]
    c_spec = pl.BlockSpec((None, past, HEAD_DIM), lambda b, h: (b, 0, h))

    def col_spec(col0):
        return pl.BlockSpec((seq, HEAD_DIM), lambda b, h: (b, col0 // HEAD_DIM + h))

    return pl.pallas_call(
        functools.partial(_na_kernel, rows=rows),
        grid=(n_batch, NA_H),
        in_specs=[
            col_spec(OD_NQ), col_spec(OD_NK), col_spec(OD_NV), c_spec, c_spec,
            pl.BlockSpec((None, 2 * NA_WIN_R - 2, GRID_W, 2 * GRID_W), lambda b, h: (h, 0, 0, 0)),
        ],
        out_specs=pl.BlockSpec((seq, HEAD_DIM), lambda b, h: (b, h)),
        out_shape=jax.ShapeDtypeStruct((n_batch * seq, NA_H * HEAD_DIM), BF16),
        scratch_shapes=[pltpu.VMEM((seq, HEAD_DIM), BF16), pltpu.VMEM((seq, HEAD_DIM), BF16),
                        pltpu.VMEM((past, HEAD_DIM), BF16), pltpu.VMEM((past, HEAD_DIM), BF16)],
        compiler_params=_params("parallel", "parallel"),
        name="neighbourhood",
    )(proj, proj, proj, k_ctx, v_ctx, bias)


def _order_masks(n=CHUNK):
    row = lax.broadcasted_iota(jnp.int32, (n, n), 0)
    col = lax.broadcasted_iota(jnp.int32, (n, n), 1)
    return row, col


def _chunk_cumsum(x, chunk, reverse):
    seq = x.shape[0]
    pos = lax.broadcasted_iota(jnp.int32, x.shape, 0) % chunk
    shift = 1
    while shift < chunk:
        if reverse:
            x = x + jnp.where(pos < chunk - shift, pltpu.roll(x, seq - shift, 0), 0.0)
        else:
            x = x + jnp.where(pos >= shift, pltpu.roll(x, shift, 0), 0.0)
        shift *= 2
    return x


def _gla_kernel(*refs, seq, has_s0, write_state):
    it = iter(refs)
    q_ref, k_ref, v_ref, gg_ref, lo_ref, w2_ref, b2_ref, gn_ref = (next(it) for _ in range(8))
    s0_ref = next(it) if has_s0 else None
    mix_ref = next(it)
    st_ref = next(it) if write_state else None
    o_s, bc_s = next(it), next(it)
    n_chunks = seq // CHUNK
    lane = lax.broadcasted_iota(jnp.int32, (1, LANES), 1)
    head_mask = [(lane < GLA_DK).astype(F32), (lane >= GLA_DK).astype(F32)]
    row, col = _order_masks()
    before = [col <= row, col >= row]

    lo = lo_ref[...].astype(BF16)
    for d in range(2):
        z = _dot(lo, w2_ref[d].astype(BF16)) + b2_ref[d]
        bc_s[d] = _chunk_cumsum(-_softplus(-z) * (1.0 / GLA_TAU), CHUNK, d == 1)

    chains = [(d, p) for d in range(2) for p in range(2)]
    cols = [slice(p * GLA_DV, (p + 1) * GLA_DV) for p in range(2)]

    def body(i, carry):
        states = list(carry)
        pre = []
        for s in range(GLA_STEPS):
            step = i * GLA_STEPS + s
            rows = [pl.ds(pl.multiple_of(n * CHUNK, CHUNK), CHUNK) for n in (step, n_chunks - 1 - step)]
            qe, ke, kd, dec = [], [], [], []
            for d in range(2):
                bc = bc_s[d, rows[d], :]
                bl = bc[CHUNK - 1:CHUNK, :] if d == 0 else bc[0:1, :]
                k = k_ref[rows[d], :]
                qe.append(q_ref[rows[d], :] * (GLA_DK ** -0.5) * jnp.exp(bc))
                ke.append((k * jnp.exp(-bc)).astype(BF16))
                kd.append(k * jnp.exp(bl - bc))
                dec.append(jnp.exp(bl))
            qep = [(qe[d] * head_mask[p]).astype(BF16) for d, p in chains]
            kdp = [(kd[d] * head_mask[p]).astype(BF16) for d, p in chains]
            vp = [v_ref[rows[d], cols[p]].astype(BF16) for d, p in chains]
            sc = [jnp.where(before[d], _dot_nt(qep[c], ke[d]), 0.0).astype(BF16) for c, (d, p) in enumerate(chains)]
            kv = [_dot_tn(vp[c], kdp[c]) for c in range(4)]
            pre.append((rows, qep, vp, sc, kv, dec))
        for rows, qep, vp, sc, kv, dec in pre:
            inter = [_dot_nt(qep[c], states[c].astype(BF16)) for c in range(4)]
            for c, (d, p) in enumerate(chains):
                o_s[d, rows[d], cols[p]] = _dot(sc[c], vp[c]) + inter[c]
                states[c] = states[c] * dec[d] + kv[c]
        return tuple(states)

    init = []
    for d in range(2):
        if has_s0:
            s0t = s0_ref[d].T
            init += [s0t * head_mask[0], s0t * head_mask[1]]
        else:
            init += [jnp.zeros((GLA_DV, LANES), F32), jnp.zeros((GLA_DV, LANES), F32)]
    fin = lax.fori_loop(0, n_chunks // GLA_STEPS, body, tuple(init))
    if write_state:
        for d in range(2):
            st_ref[d] = (fin[2 * d] + fin[2 * d + 1]).T

    for p in range(2):
        cols = slice(p * GLA_DV, (p + 1) * GLA_DV)
        y = _rms(o_s[0, :, cols] + o_s[1, :, cols], gn_ref[...]) * _silu(gg_ref[:, cols])
        mix_ref[:, cols] = y.astype(mix_ref.dtype)


def _gla(proj, n_batch, seq, w2pad, b2, gn, s0, write_state):
    pairs = GLA_H // 2
    pw = 2 * GLA_DV
    st_spec = pl.BlockSpec((None, 2, None, LANES, GLA_DV), lambda b, p: (b, 0, p, 0, 0))
    in_specs = [
        pl.BlockSpec((seq, LANES), lambda b, p: (b, EV_Q // LANES + p)),
        pl.BlockSpec((seq, LANES), lambda b, p: (b, EV_K // LANES + p)),
        pl.BlockSpec((seq, pw), lambda b, p: (b, EV_V // pw + p)),
        pl.BlockSpec((seq, pw), lambda b, p: (b, EV_G // pw + p)),
        pl.BlockSpec((seq, LANES), lambda b, p: (b, EV_LO // LANES)),
        pl.BlockSpec((2, LANES, LANES), lambda b, p: (0, 0, p)),
        pl.BlockSpec((2, 1, LANES), lambda b, p: (0, 0, p)),
        pl.BlockSpec((1, GLA_DV), lambda b, p: (0, 0)),
    ]
    args = [proj, proj, proj, proj, proj, w2pad, b2, gn]
    if s0 is not None:
        in_specs.append(st_spec)
        args.append(s0)
    mix_spec = pl.BlockSpec((seq, pw), lambda b, p: (b, p))
    mix_shape = jax.ShapeDtypeStruct((n_batch * seq, GLA_H * GLA_DV), BF16)
    if write_state:
        out_specs = [mix_spec, st_spec]
        out_shape = [mix_shape, jax.ShapeDtypeStruct((n_batch, 2, pairs, LANES, GLA_DV), F32)]
    else:
        out_specs, out_shape = mix_spec, mix_shape
    return pl.pallas_call(
        functools.partial(_gla_kernel, seq=seq, has_s0=s0 is not None, write_state=write_state),
        grid=(n_batch, pairs),
        in_specs=in_specs,
        out_specs=out_specs,
        out_shape=out_shape,
        scratch_shapes=[pltpu.VMEM((2, seq, pw), F32), pltpu.VMEM((2, seq, LANES), F32)],
        compiler_params=_params("parallel", "parallel"),
        name="gla",
    )(*args)


def _conv_silu(x, w):
    seq = x.shape[0]
    t = lax.broadcasted_iota(jnp.int32, x.shape, 0)
    prev = jnp.where(t == 0, 0.0, pltpu.roll(x, 1, 0))
    nxt = jnp.where(t == seq - 1, 0.0, pltpu.roll(x, seq - 1, 0))
    return _silu(prev * w[0:1, :] + x * w[1:2, :] + nxt * w[2:3, :])


def _hi_lo(x):
    hi = x.astype(BF16)
    return hi, (x - hi.astype(F32)).astype(BF16)


def _split_lanes(a):
    hi, lo = _hi_lo(a)
    return jnp.concatenate([hi, hi, lo], axis=1)


def _split_rows(b):
    hi, lo = _hi_lo(b)
    return jnp.concatenate([hi, lo, hi], axis=0)


def _l2norm(x):
    return x * lax.rsqrt(jnp.sum(x * x, axis=-1, keepdims=True) + EPS)


def _delta_kernel(*refs, seq, hp, cpi, has_s0, write_state):
    it = iter(refs)
    q_ref, k_ref, v_ref, z_ref, gates_ref, wq_ref, wk_ref, wv_ref, gn_ref = (next(it) for _ in range(9))
    s0_ref = next(it) if has_s0 else None
    mix_ref = next(it)
    st_ref = next(it) if write_state else None
    qs, ks, vs, o_s, gc_s, b_s, u0_s, kc_s, at_s, qe_s, kd_s = (next(it) for _ in range(11))
    n_chunks = seq // DN_CHUNK
    lane = lax.broadcasted_iota(jnp.int32, (1, LANES), 1)
    row, col = _order_masks(DN_CHUNK)
    eye = (row == col).astype(F32)
    before = [col <= row, col >= row]
    strict = [col < row, col > row]
    couples = [((row >> (lvl + 1)) == (col >> (lvl + 1))) & ((row >> lvl) != (col >> lvl))
               for lvl in range(DN_CHUNK.bit_length() - 1)]

    gates = gates_ref[...]
    for hd in range(hp):
        cols = slice(hd * HEAD_DIM, (hd + 1) * HEAD_DIM)
        h = pl.program_id(1) * hp + hd
        qs[hd] = _l2norm(_conv_silu(q_ref[:, cols], wq_ref[:, cols])) * (HEAD_DIM ** -0.5)
        ks[hd] = _l2norm(_conv_silu(k_ref[:, cols], wk_ref[:, cols]))
        vs[hd] = _conv_silu(v_ref[:, cols], wv_ref[:, cols])
        for d in range(2):
            sel_g = lane == d * DN_H + h
            sel_b = lane == 2 * DN_H + d * DN_H + h
            gc_s[hd, d] = jnp.broadcast_to(
                jnp.sum(jnp.where(sel_g, gates, 0.0), axis=-1, keepdims=True), (seq, LANES))
            b_s[hd, d] = jnp.broadcast_to(
                jnp.sum(jnp.where(sel_b, gates, 0.0), axis=-1, keepdims=True), (seq, LANES))

    def chunk_rows(n):
        return pl.ds(pl.multiple_of(n * DN_CHUNK, DN_CHUNK), DN_CHUNK)

    def last_row(d, n):
        return pl.ds(n * DN_CHUNK + (DN_CHUNK - 1 if d == 0 else 0), 1)

    def wy_stages(units):
        nu = range(len(units))
        rows = [chunk_rows(n) for _, _, n in units]
        gc = [gc_s[hd, d, rows[i], :] for i, (hd, d, _) in enumerate(units)]
        bb = [b_s[hd, d, rows[i], :] for i, (hd, d, _) in enumerate(units)]
        k = [ks[hd, rows[i], :] for i, (hd, _, _) in enumerate(units)]
        k16 = [k[i].astype(BF16) for i in nu]
        kbeta = [k[i] * bb[i] for i in nu]
        kk = [_dot_nt(kbeta[i].astype(BF16), k16[i]) for i in nu]
        qk = [_dot_nt(qs[hd, rows[i], :].astype(BF16), k16[i]) for i, (hd, _, _) in enumerate(units)]
        yield
        mm = []
        for i, (hd, d, _) in enumerate(units):
            decay = jnp.where(before[d], jnp.exp(jnp.where(before[d], gc[i] - gc[i].T, 0.0)), 0.0)
            mm.append(jnp.where(strict[d], kk[i] * decay, 0.0))
            at_s[hd, d, rows[i], :] = (qk[i] * decay).astype(BF16)
        inv = [eye - jnp.where(couples[0], mm[i], 0.0) for i in nu]
        for lvl in range(1, len(couples)):
            inv16 = [inv[i].astype(BF16) for i in nu]
            c_inv = [_dot(jnp.where(couples[lvl], mm[i], 0.0).astype(BF16), inv16[i]).astype(BF16) for i in nu]
            yield
            inv = [inv[i] - _dot(inv16[i], c_inv[i]) for i in nu]
            yield
        inv16 = [inv[i].astype(BF16) for i in nu]
        egc = [jnp.exp(gc[i]) for i in nu]
        rhs = [jnp.concatenate([vs[hd, rows[i], :] * bb[i], kbeta[i] * egc[i]], axis=-1)
               for i, (hd, _, _) in enumerate(units)]
        sol = [_dot(inv16[i], rhs[i].astype(BF16)) for i in nu]
        yield
        mm_split = [_split_lanes(mm[i]) for i in nu]
        for _ in range(DN_REFINE_STEPS):
            res = [rhs[i] - sol[i] - _dot(mm_split[i], _split_rows(sol[i])) for i in nu]
            yield
            sol = [sol[i] + _dot(inv16[i], res[i].astype(BF16)) for i in nu]
            yield
        for i, (hd, d, _) in enumerate(units):
            u0_s[hd, d, rows[i], :] = sol[i][:, :HEAD_DIM]
            kc_s[hd, d, rows[i], :] = sol[i][:, HEAD_DIM:].astype(BF16)
            g_last = gc[i][DN_CHUNK - 1:DN_CHUNK, :] if d == 0 else gc[i][0:1, :]
            qe_s[hd, d, rows[i], :] = (qs[hd, rows[i], :] * egc[i]).astype(BF16)
            kd_s[hd, d, rows[i], :] = (k[i] * jnp.exp(g_last - gc[i])).astype(BF16)

    chains = [(hd, d) for hd in range(hp) for d in range(2)]

    def chunk_of(d, step):
        return step if d == 0 else n_chunks - 1 - step

    def scan_stages(state, steps):
        for step in steps:
            ns = [chunk_of(d, step) for _, d in chains]
            rows = [chunk_rows(n) for n in ns]
            s16 = [st.astype(BF16) for st in state]
            ks_ = [_dot(kc_s[hd, d, rows[c], :], s16[c]) for c, (hd, d) in enumerate(chains)]
            qs_ = [_dot(qe_s[hd, d, rows[c], :], s16[c]) for c, (hd, d) in enumerate(chains)]
            yield
            u16 = [(u0_s[hd, d, rows[c], :] - ks_[c]).astype(BF16) for c, (hd, d) in enumerate(chains)]
            au = [_dot(at_s[hd, d, rows[c], :], u16[c]) for c, (hd, d) in enumerate(chains)]
            ku = [_dot_tn(kd_s[hd, d, rows[c], :], u16[c]) for c, (hd, d) in enumerate(chains)]
            yield
            for c, (hd, d) in enumerate(chains):
                o_s[hd, d, rows[c], :] = qs_[c] + au[c]
                state[c] = state[c] * jnp.exp(gc_s[hd, d, last_row(d, ns[c]), :]) + ku[c]

    def run(*stages):
        live = list(stages)
        while live:
            for g in list(live):
                try:
                    next(g)
                except StopIteration:
                    live.remove(g)

    def group_units(j):
        return [(hd, d, chunk_of(d, j * cpi + c)) for c in range(cpi) for hd in range(hp) for d in range(2)]

    def group_steps(j):
        return [j * cpi + c for c in range(cpi)]

    n_groups = n_chunks // cpi
    if has_s0:
        init = tuple(s0_ref[d, hd] for hd, d in chains)
    else:
        init = tuple(jnp.zeros((HEAD_DIM, HEAD_DIM), F32) for _ in chains)
    run(wy_stages(group_units(0)))

    def body(j, carry):
        state = list(carry)
        run(wy_stages(group_units(j)), scan_stages(state, group_steps(j - 1)))
        return tuple(state)

    fin = list(lax.fori_loop(1, n_groups, body, init))
    run(scan_stages(fin, group_steps(n_groups - 1)))
    if write_state:
        for c, (hd, d) in enumerate(chains):
            st_ref[d, hd] = fin[c]

    for hd in range(hp):
        cols = slice(hd * HEAD_DIM, (hd + 1) * HEAD_DIM)
        y = _rms(o_s[hd, 0] + o_s[hd, 1], gn_ref[...]) * _silu(z_ref[:, cols])
        mix_ref[:, cols] = y.astype(mix_ref.dtype)


def _dn_gates_kernel(ab_ref, alog_ref, dtb_ref, o_ref):
    ab = ab_ref[...]
    lane = lax.broadcasted_iota(jnp.int32, (1, LANES), 1)
    g = -jnp.exp(alog_ref[...]) * _softplus(ab + dtb_ref[...])
    fwd = _chunk_cumsum(g, DN_CHUNK, False)
    bwd = _chunk_cumsum(g, DN_CHUNK, True)
    o_ref[...] = jnp.where(lane < DN_H, fwd, jnp.where(lane < 2 * DN_H, bwd, _sigmoid(ab)))


def _dn_gates(proj, n_batch, seq, alog_row, dtb_row):
    vec = pl.BlockSpec((1, LANES), lambda b: (0, 0))
    return pl.pallas_call(
        _dn_gates_kernel,
        grid=(n_batch,),
        in_specs=[pl.BlockSpec((seq, LANES), lambda b: (b, OD_AB // LANES)), vec, vec],
        out_specs=pl.BlockSpec((seq, LANES), lambda b: (b, 0)),
        out_shape=jax.ShapeDtypeStruct((n_batch * seq, LANES), F32),
        compiler_params=_params("parallel"),
        name="dn_gates",
    )(proj, alog_row, dtb_row)


def _delta(proj, gates, n_batch, seq, conv_w, gn, s0, write_state):
    n_chunks = seq // DN_CHUNK
    hp = max(1, min(DN_H, DN_UNITS // (2 * n_chunks)))
    cpi = DN_UNITS // (2 * hp)
    hw = hp * HEAD_DIM

    def col_spec(col0):
        return pl.BlockSpec((seq, hw), lambda b, h: (b, col0 // hw + h))

    def conv_spec(part):
        return pl.BlockSpec((3, hw), lambda b, h: (0, part * (DN_H // hp) + h))

    vec = pl.BlockSpec((1, LANES), lambda b, h: (0, 0))
    st_spec = pl.BlockSpec((None, 2, hp, HEAD_DIM, HEAD_DIM), lambda b, h: (b, 0, h, 0, 0))
    in_specs = [col_spec(OD_DQ), col_spec(OD_DK), col_spec(OD_DV), col_spec(OD_DZ),
                pl.BlockSpec((seq, LANES), lambda b, h: (b, 0)),
                conv_spec(0), conv_spec(1), conv_spec(2), vec]
    args = [proj, proj, proj, proj, gates, conv_w, conv_w, conv_w, gn]
    if s0 is not None:
        in_specs.append(st_spec)
        args.append(s0)
    mix_spec = pl.BlockSpec((seq, hw), lambda b, h: (b, h))
    mix_shape = jax.ShapeDtypeStruct((n_batch * seq, DN_H * HEAD_DIM), BF16)
    if write_state:
        out_specs = [mix_spec, st_spec]
        out_shape = [mix_shape, jax.ShapeDtypeStruct((n_batch, 2, DN_H, HEAD_DIM, HEAD_DIM), F32)]
    else:
        out_specs, out_shape = mix_spec, mix_shape
    tok = pltpu.VMEM((hp, seq, HEAD_DIM), F32)
    both = pltpu.VMEM((hp, 2, seq, HEAD_DIM), F32)
    both16 = pltpu.VMEM((hp, 2, seq, HEAD_DIM), BF16)
    return pl.pallas_call(
        functools.partial(_delta_kernel, seq=seq, hp=hp, cpi=cpi, has_s0=s0 is not None, write_state=write_state),
        grid=(n_batch, DN_H // hp),
        in_specs=in_specs,
        out_specs=out_specs,
        out_shape=out_shape,
        scratch_shapes=[tok, tok, tok, both, both, both, both, both16, both16, both16, both16],
        compiler_params=_params("parallel", "arbitrary"),
        name="deltanet",
    )(*args)


def _rope_tables(n_tok):
    t = jnp.arange(n_tok)
    inv = 1.0 / (ROPE_THETA ** (jnp.arange(ROPE_FREQ, dtype=F32) / ROPE_FREQ))
    pos = jnp.stack([t // GRID_W, t % GRID_W], axis=1).astype(F32)
    ang = pos[:, :, None] * inv
    cos = jnp.concatenate([jnp.cos(ang), jnp.cos(ang)], axis=-1).reshape(n_tok, HEAD_DIM)
    sin = jnp.concatenate([-jnp.sin(ang), jnp.sin(ang)], axis=-1).reshape(n_tok, HEAD_DIM)
    return cos, sin


def _pad_lanes(v):
    return jnp.pad(v.reshape(1, -1), ((0, 0), (0, LANES - v.size)))


def kernel(x_prompt, x_sample, state_gla, cache_gqa_k, cache_gqa_v, cache_na_k, cache_na_v, state_delta, c, c_ctx, norm1, norm2, w_ada, b_ada, w_mlp1, w_mlp2, ev_w_in, ev_w_a2, ev_b_a2, ev_gla_norm, ev_q_norm, ev_k_norm, ev_w_out, od_w_in, od_conv, od_a_log, od_dt_bias, od_dn_norm, od_rpb, od_w_out, norm_f):
    n_ctx, seq_ctx, d = x_prompt.shape
    n_lat, seq_lat, _ = x_sample.shape
    depth = w_ada.shape[0]

    w_ev = ev_w_in[0]
    w_ev = jnp.concatenate(
        [w_ev[:, :EV_AQ], w_ev[:, EV_AQ + 2 * GLA_RANK:], w_ev[:, EV_AQ:EV_AQ + 2 * GLA_RANK],
         jnp.zeros((d, EV_COLS_PAD - w_ev.shape[1]), F32)], axis=1).astype(BF16)
    w_od = jnp.pad(od_w_in[0], ((0, 0), (0, OD_COLS_PAD - od_w_in.shape[2]))).astype(BF16)
    w_in = [w_ev, w_od]
    w_out = [ev_w_out[0].astype(BF16), od_w_out[0].astype(BF16)]
    w1 = w_mlp1.astype(BF16)
    w2 = w_mlp2.astype(BF16)
    w2pad = jnp.zeros((2, LANES, GLA_H * GLA_DK), F32)
    for dd in range(2):
        w2pad = w2pad.at[dd, dd * GLA_RANK:(dd + 1) * GLA_RANK].set(ev_w_a2[0, dd])
    b2 = ev_b_a2[0].reshape(2, 1, GLA_H * GLA_DK)
    alog_row = _pad_lanes(od_a_log[0])
    dtb_row = _pad_lanes(od_dt_bias[0])
    na_bias = _na_bias_table(od_rpb[0])
    rope = _rope_tables(seq_lat)

    cond8 = jnp.concatenate([c_ctx[None, :], c, jnp.zeros((8 - 1 - n_lat, d), F32)], axis=0)
    mod = _adaln(cond8, w_ada, b_ada).reshape(depth, 8, 6, 1, d)

    def trunk(x, n_batch, seq, row_fn, caches):
        latent = caches is not None
        tm = TM_MLP
        outs = {}
        proj = _inproj(x, norm1[0:1], mod, 0, row_fn, w_in[0], TM_INPROJ)
        if latent:
            s0 = caches["gla"].reshape(n_batch, 2, GLA_H // 2, LANES, GLA_DV)
            mix_a = _gla(proj, n_batch, seq, w2pad, b2, ev_gla_norm, s0, False)
            mix_b = _attention(proj, n_batch, seq, EV_AQ, EV_AK, EV_AV, GQA_KV, GQA_H // GQA_KV,
                               norms=(ev_q_norm, ev_k_norm), rope=rope, ctx=caches["gqa"])
        else:
            mix_a, st = _gla(proj, n_batch, seq, w2pad, b2, ev_gla_norm, None, True)
            mix_b, k_new = _attention(proj, n_batch, seq, EV_AQ, EV_AK, EV_AV, GQA_KV, GQA_H // GQA_KV,
                                      norms=(ev_q_norm, ev_k_norm), write_k=True)
            outs["st_gla"] = st.reshape(n_batch, 1, 2, GLA_H, GLA_DK, GLA_DV)
            outs["ck_gqa"] = k_new.reshape(n_batch, 1, seq, GQA_KV, HEAD_DIM)
            outs["cv_gqa"] = proj[:, EV_AV:EV_AV + GQA_KV * HEAD_DIM].reshape(n_batch, 1, seq, GQA_KV, HEAD_DIM)
        x, h = _outproj(mix_a, mix_b, w_out[0], x, norm2[0:1], mod, 0, row_fn, tm)
        x = _mlp(x, h, mod, 0, row_fn, w1, w2, norm_f[None, :], False, tm)
        proj = _inproj(x, norm1[1:2], mod, 1, row_fn, w_in[1], TM_INPROJ)
        if latent:
            mix_a = _neighbourhood(proj, n_batch, seq, caches["na"][0], caches["na"][1], na_bias)
            s0 = caches["delta"].reshape(n_batch, 2, DN_H, HEAD_DIM, HEAD_DIM)
            gates = _dn_gates(proj, n_batch, seq, alog_row, dtb_row)
            mix_b = _delta(proj, gates, n_batch, seq, od_conv[0], od_dn_norm, s0, False)
        else:
            mix_a = _attention(proj, n_batch, seq, OD_NQ, OD_NK, OD_NV, NA_H, 1)
            gates = _dn_gates(proj, n_batch, seq, alog_row, dtb_row)
            mix_b, st = _delta(proj, gates, n_batch, seq, od_conv[0], od_dn_norm, None, True)
            outs["ck_na"] = proj[:, OD_NK:OD_NK + NA_H * HEAD_DIM].reshape(n_batch, 1, seq, NA_H, HEAD_DIM)
            outs["cv_na"] = proj[:, OD_NV:OD_NV + NA_H * HEAD_DIM].reshape(n_batch, 1, seq, NA_H, HEAD_DIM)
            outs["st_dn"] = st.reshape(n_batch, 1, 2, DN_H, HEAD_DIM, HEAD_DIM)
        x, h = _outproj(mix_a, mix_b, w_out[1], x, norm2[1:2], mod, 1, row_fn, tm)
        y = _mlp(x, h, mod, 1, row_fn, w1, w2, norm_f[None, :], True, tm)
        return y.reshape(n_batch, seq, d), outs

    y_prompt, new = trunk(x_prompt.reshape(n_ctx * seq_ctx, d), n_ctx, seq_ctx, lambda tok: 0, None)
    past = cache_gqa_k.shape[2]
    caches = {
        "gla": state_gla[:, 0],
        "gqa": (cache_gqa_k[:, 0].reshape(n_lat, past, GQA_KV * HEAD_DIM),
                cache_gqa_v[:, 0].reshape(n_lat, past, GQA_KV * HEAD_DIM)),
        "na": (cache_na_k[:, 0].reshape(n_lat, past, NA_H * HEAD_DIM),
               cache_na_v[:, 0].reshape(n_lat, past, NA_H * HEAD_DIM)),
        "delta": state_delta[:, 0],
    }
    y_sample, _ = trunk(x_sample.reshape(n_lat * seq_lat, d), n_lat, seq_lat,
                        lambda tok: 1 + tok // seq_lat, caches)
    return (y_prompt, y_sample, new["st_gla"], new["ck_gqa"], new["cv_gqa"], new["ck_na"], new["cv_na"],
            new["st_dn"])
```

```python
import functools

import jax
import jax.numpy as jnp
import numpy as np
from jax import lax
from jax.experimental import pallas as pl
from jax.experimental.pallas import tpu as pltpu

F32 = jnp.float32
BF16 = jnp.bfloat16
HIGHEST = lax.Precision.HIGHEST

D_MODEL = 2048
D_FF = 4 * D_MODEL
HEAD_DIM = 128
LANES = 128
GRID_W = 64
GLA_H = 8
GLA_DK = 64
GLA_DV = 128
GLA_RANK = 16
GLA_TAU = 16.0
GQA_H = 8
GQA_KV = 2
NA_H = 8
NA_WIN_R = 8
NA_WIN_C = 16
DN_H = 8
CHUNK = 64
DN_CHUNK = 128
DN_REFINE_STEPS = 1
DN_UNITS = 16
DN_TOKENS_PER_STEP = 2048
TN_ADALN = 2048
TM_INPROJ = 1024
TN_INPROJ = 1280
TM_MLP = 512
TF_MLP = 1024
ATTN_HEADS_PER_STEP = 8
GLA_STEPS = 8
NA_ROWS_PER_STEP = 16
ROPE_THETA = 10000.0
ROPE_FREQ = HEAD_DIM // 4
EPS = 1e-6
MASK_VALUE = -1e30

EV_Q, EV_K, EV_V, EV_G = 0, 512, 1024, 2048
EV_AQ, EV_AK, EV_AV, EV_LO = 3072, 4096, 4352, 4608
EV_COLS_PAD = 5120
OD_NQ, OD_NK, OD_NV = 0, 1024, 2048
OD_DQ, OD_DK, OD_DV, OD_DZ, OD_AB = 3072, 4096, 5120, 6144, 7168
OD_COLS_PAD = 7680

VMEM_LIMIT = 56 * 1024 * 1024


def _params(*sem):
    return pltpu.CompilerParams(dimension_semantics=sem, vmem_limit_bytes=VMEM_LIMIT)


def _dot(a, b):
    return jnp.dot(a, b, preferred_element_type=F32)


def _dot_nt(a, b):
    return lax.dot_general(a, b, (((1,), (1,)), ((), ())), preferred_element_type=F32)


def _dot_tn(a, b):
    return lax.dot_general(a, b, (((0,), (0,)), ((), ())), preferred_element_type=F32)


def _sigmoid(x):
    return 1.0 / (1.0 + jnp.exp(-x))


def _silu(x):
    return x * _sigmoid(x)


def _softplus(x):
    return jnp.maximum(x, 0.0) + jnp.log(1.0 + jnp.exp(-jnp.abs(x)))


def _rms(x, g):
    return x * lax.rsqrt(jnp.mean(x * x, axis=-1, keepdims=True) + EPS) * g


def _rms_mod(x, g, sc, sh):
    return _rms(x, g) * (1.0 + sc) + sh


def _adaln_kernel(c_ref, w_ref, b_ref, o_ref):
    a = _silu(c_ref[...]).astype(BF16)
    o_ref[...] = _dot(a, w_ref[...].astype(BF16)) + b_ref[...]


def _adaln(cond8, w_ada, b_ada):
    depth, d, n = w_ada.shape
    tn = TN_ADALN
    return pl.pallas_call(
        _adaln_kernel,
        grid=(depth, n // tn),
        in_specs=[
            pl.BlockSpec((8, d), lambda l, j: (0, 0)),
            pl.BlockSpec((None, d, tn), lambda l, j: (l, 0, j)),
            pl.BlockSpec((None, 1, tn), lambda l, j: (l, 0, j)),
        ],
        out_specs=pl.BlockSpec((None, 8, tn), lambda l, j: (l, 0, j)),
        out_shape=jax.ShapeDtypeStruct((depth, 8, n), F32),
        compiler_params=_params("parallel", "parallel"),
        name="adaln",
    )(cond8, w_ada, b_ada.reshape(depth, 1, n))


def _mod_spec(layer, k, row_fn, tm):
    return pl.BlockSpec((None, None, None, 1, D_MODEL), lambda i, j: (layer, row_fn(i * tm), k, 0, 0))


def _inproj_kernel(x_ref, g_ref, sc_ref, sh_ref, w_ref, o_ref, h_ref):
    @pl.when(pl.program_id(1) == 0)
    def _():
        h_ref[...] = _rms_mod(x_ref[...], g_ref[...], sc_ref[...], sh_ref[...]).astype(BF16)

    o_ref[...] = _dot(h_ref[...], w_ref[...])


def _inproj(x, g, mod, layer, row_fn, w, tm):
    m, d = x.shape
    n = w.shape[1]
    tn = TN_INPROJ
    return pl.pallas_call(
        _inproj_kernel,
        grid=(m // tm, n // tn),
        in_specs=[
            pl.BlockSpec((tm, d), lambda i, j: (i, 0)),
            pl.BlockSpec((1, d), lambda i, j: (0, 0)),
            _mod_spec(layer, 1, row_fn, tm),
            _mod_spec(layer, 0, row_fn, tm),
            pl.BlockSpec((d, tn), lambda i, j: (0, j)),
        ],
        out_specs=pl.BlockSpec((tm, tn), lambda i, j: (i, j)),
        out_shape=jax.ShapeDtypeStruct((m, n), F32),
        scratch_shapes=[pltpu.VMEM((tm, d), BF16)],
        compiler_params=_params("parallel", "arbitrary"),
        name="inproj",
    )(x, g, mod, mod, w)


def _outproj_kernel(a_ref, b_ref, wa_ref, wb_ref, x_ref, gt_ref, g_ref, sc_ref, sh_ref, o_ref, h_ref):
    acc = _dot(a_ref[...], wa_ref[...]) + _dot(b_ref[...], wb_ref[...])
    y = x_ref[...] + gt_ref[...] * acc
    o_ref[...] = y
    h_ref[...] = _rms_mod(y, g_ref[...], sc_ref[...], sh_ref[...]).astype(BF16)


def _outproj(mix_a, mix_b, w, x, g2, mod, layer, row_fn, tm):
    m, d = x.shape
    ka, kb = mix_a.shape[1], mix_b.shape[1]
    row = pl.BlockSpec((tm, d), lambda i, j: (i, 0))
    return pl.pallas_call(
        _outproj_kernel,
        grid=(m // tm, 1),
        in_specs=[
            pl.BlockSpec((tm, ka), lambda i, j: (i, 0)),
            pl.BlockSpec((tm, kb), lambda i, j: (i, 0)),
            pl.BlockSpec((ka, d), lambda i, j: (0, 0)),
            pl.BlockSpec((kb, d), lambda i, j: (1, 0)),
            row,
            _mod_spec(layer, 2, row_fn, tm),
            pl.BlockSpec((1, d), lambda i, j: (0, 0)),
            _mod_spec(layer, 4, row_fn, tm),
            _mod_spec(layer, 3, row_fn, tm),
        ],
        out_specs=[row, row],
        out_shape=[jax.ShapeDtypeStruct((m, d), F32), jax.ShapeDtypeStruct((m, d), BF16)],
        compiler_params=_params("parallel", "arbitrary"),
        name="outproj",
    )(mix_a, mix_b, w, w, x, mod, g2, mod, mod)


def _mlp_kernel(x_ref, h_ref, gt_ref, w1_ref, w2_ref, gf_ref, o_ref, acc_ref, *, final_norm):
    j = pl.program_id(1)

    @pl.when(j == 0)
    def _():
        acc_ref[...] = jnp.zeros_like(acc_ref)

    hid = jnp.maximum(_dot(h_ref[...], w1_ref[...]), 0.0)
    acc_ref[...] += _dot((hid * hid).astype(BF16), w2_ref[...])

    @pl.when(j == pl.num_programs(1) - 1)
    def _():
        y = x_ref[...] + gt_ref[...] * acc_ref[...]
        if final_norm:
            y = _rms(y, gf_ref[...])
        o_ref[...] = y


def _mlp(x, h, mod, layer, row_fn, w1, w2, gf, final_norm, tm):
    m, d = x.shape
    ff = w1.shape[2]
    tf = TF_MLP
    return pl.pallas_call(
        functools.partial(_mlp_kernel, final_norm=final_norm),
        grid=(m // tm, ff // tf),
        in_specs=[
            pl.BlockSpec((tm, d), lambda i, j: (i, 0)),
            pl.BlockSpec((tm, d), lambda i, j: (i, 0)),
            _mod_spec(layer, 5, row_fn, tm),
            pl.BlockSpec((None, d, tf), lambda i, j: (layer, 0, j)),
            pl.BlockSpec((None, tf, d), lambda i, j: (layer, j, 0)),
            pl.BlockSpec((1, d), lambda i, j: (0, 0)),
        ],
        out_specs=pl.BlockSpec((tm, d), lambda i, j: (i, 0)),
        out_shape=jax.ShapeDtypeStruct((m, d), F32),
        scratch_shapes=[pltpu.VMEM((tm, d), F32)],
        compiler_params=_params("parallel", "arbitrary"),
        name="mlp",
    )(x, h, mod, w1, w2, gf)


def _rope(x, cos, sin):
    lane = lax.broadcasted_iota(jnp.int32, x.shape, 1)
    first_half = (lane % (2 * ROPE_FREQ)) < ROPE_FREQ
    partner = jnp.where(first_half, pltpu.roll(x, LANES - ROPE_FREQ, 1), pltpu.roll(x, ROPE_FREQ, 1))
    return x * cos + partner * sin


def _attn_kernel(*refs, nkv, group, use_norm, use_rope, has_ctx, write_k):
    it = iter(refs)
    q_ref, k_ref, v_ref = next(it), next(it), next(it)
    qn_ref = kn_ref = cq_ref = sq_ref = ck_ref = sk_ref = kc_ref = vc_ref = knew_ref = None
    if use_norm:
        qn_ref, kn_ref = next(it), next(it)
    if use_rope:
        cq_ref, sq_ref, ck_ref, sk_ref = next(it), next(it), next(it), next(it)
    if has_ctx:
        kc_ref, vc_ref = next(it), next(it)
    o_ref = next(it)
    if write_k:
        knew_ref = next(it)
    kbuf, vbuf = next(it), next(it)
    scale = HEAD_DIM ** -0.5 * float(np.log2(np.e))

    kcols = [slice(kv * HEAD_DIM, (kv + 1) * HEAD_DIM) for kv in range(nkv)]

    @pl.when(pl.program_id(2) == 0)
    def _():
        for kv in range(nkv):
            k = k_ref[:, kcols[kv]]
            if use_norm:
                k = _rms(k, kn_ref[...])
            if write_k:
                knew_ref[:, kcols[kv]] = k
            if use_rope:
                k = _rope(k, ck_ref[...], sk_ref[...])
            kbuf[:, kcols[kv]] = k.astype(BF16)
        vbuf[...] = v_ref[...].astype(BF16)

    heads = range(nkv * group)
    kv_of = [h // group for h in heads]
    cols = [slice(h * HEAD_DIM, (h + 1) * HEAD_DIM) for h in heads]
    qb = []
    for h in heads:
        q = q_ref[:, cols[h]]
        if use_norm:
            q = _rms(q, qn_ref[...])
        if use_rope:
            q = _rope(q, cq_ref[...], sq_ref[...])
        qb.append((q * scale).astype(BF16))
    s = [_dot_nt(qb[h], kbuf[:, kcols[kv_of[h]]]) for h in heads]
    m = [jnp.max(s[h], axis=-1, keepdims=True) for h in heads]
    if has_ctx:
        kc16 = kc_ref[...].astype(BF16)
        s_c = [_dot_nt(qb[h], kc16[:, kcols[kv_of[h]]]) for h in heads]
        m = [jnp.maximum(m[h], jnp.max(s_c[h], axis=-1, keepdims=True)) for h in heads]
    p = [jnp.exp2(s[h] - m[h]) for h in heads]
    l = [jnp.sum(p[h], axis=-1, keepdims=True) for h in heads]
    o = [_dot(p[h].astype(BF16), vbuf[:, kcols[kv_of[h]]]) for h in heads]
    if has_ctx:
        vc16 = vc_ref[...].astype(BF16)
        p_c = [jnp.exp2(s_c[h] - m[h]) for h in heads]
        l = [l[h] + jnp.sum(p_c[h], axis=-1, keepdims=True) for h in heads]
        o = [o[h] + _dot(p_c[h].astype(BF16), vc16[:, kcols[kv_of[h]]]) for h in heads]
    for h in heads:
        o_ref[:, cols[h]] = (o[h] / l[h]).astype(o_ref.dtype)


def _attention(proj, n_batch, seq, q_col, k_col, v_col, n_kv, group, norms=None, rope=None, ctx=None,
               write_k=False):
    tq = min(seq, 256)
    nq = seq // tq
    nkv = max(1, min(n_kv, ATTN_HEADS_PER_STEP // group)) if nq == 1 else 1
    gw = nkv * group * HEAD_DIM
    kw = nkv * HEAD_DIM
    in_specs = [
        pl.BlockSpec((tq, gw), lambda b, h, i: (b * nq + i, q_col // gw + h)),
        pl.BlockSpec((seq, kw), lambda b, h, i: (b, k_col // kw + h)),
        pl.BlockSpec((seq, kw), lambda b, h, i: (b, v_col // kw + h)),
    ]
    args = [proj, proj, proj]
    vec = pl.BlockSpec((1, HEAD_DIM), lambda b, h, i: (0, 0))
    if norms is not None:
        in_specs += [vec, vec]
        args += [norms[0], norms[1]]
    if rope is not None:
        tab_q = pl.BlockSpec((tq, HEAD_DIM), lambda b, h, i: (i, 0))
        tab_k = pl.BlockSpec((seq, HEAD_DIM), lambda b, h, i: (0, 0))
        in_specs += [tab_q, tab_q, tab_k, tab_k]
        args += [rope[0], rope[1], rope[0], rope[1]]
    if ctx is not None:
        past = ctx[0].shape[1]
        c_spec = pl.BlockSpec((None, past, kw), lambda b, h, i: (b, 0, h))
        in_specs += [c_spec, c_spec]
        args += [ctx[0], ctx[1]]
    o_spec = pl.BlockSpec((tq, gw), lambda b, h, i: (b * nq + i, h))
    o_shape = jax.ShapeDtypeStruct((n_batch * seq, n_kv * group * HEAD_DIM), BF16)
    if write_k:
        out_specs = [o_spec, pl.BlockSpec((seq, kw), lambda b, h, i: (b, h))]
        out_shape = [o_shape, jax.ShapeDtypeStruct((n_batch * seq, n_kv * HEAD_DIM), F32)]
    else:
        out_specs, out_shape = o_spec, o_shape
    return pl.pallas_call(
        functools.partial(_attn_kernel, nkv=nkv, group=group, use_norm=norms is not None, use_rope=rope is not None,
                          has_ctx=ctx is not None, write_k=write_k),
        grid=(n_batch, n_kv // nkv, nq),
        in_specs=in_specs,
        out_specs=out_specs,
        out_shape=out_shape,
        scratch_shapes=[pltpu.VMEM((seq, kw), BF16), pltpu.VMEM((seq, kw), BF16)],
        compiler_params=_params("parallel", "parallel", "arbitrary"),
        name="attention",
    )(*args)


def _na_row_start(r, rows):
    return jnp.clip(r - NA_WIN_R // 2, 0, rows - NA_WIN_R)


def _na_kernel(q_ref, k_ref, v_ref, kc_ref, vc_ref, bias_ref, o_ref, kb, vb, kcb, vcb, *, rows):
    win = NA_WIN_R * GRID_W
    scale = HEAD_DIM ** -0.5
    kb[...] = k_ref[...].astype(BF16)
    vb[...] = v_ref[...].astype(BF16)
    kcb[...] = kc_ref[...].astype(BF16)
    vcb[...] = vc_ref[...].astype(BF16)

    def body(j, carry):
        rr = [j * NA_ROWS_PER_STEP + i for i in range(NA_ROWS_PER_STEP)]
        rs = [_na_row_start(r, rows) for r in rr]
        k_rows = [pl.ds(pl.multiple_of(s * GRID_W, GRID_W), win) for s in rs]
        q_rows = [pl.ds(pl.multiple_of(r * GRID_W, GRID_W), GRID_W) for r in rr]
        qb = [q_ref[q_rows[i], :].astype(BF16) for i in range(NA_ROWS_PER_STEP)]
        s_w = [_dot_nt(qb[i], kb[k_rows[i], :]) * scale for i in range(NA_ROWS_PER_STEP)]
        s_c = [_dot_nt(qb[i], kcb[...]) * scale for i in range(NA_ROWS_PER_STEP)]
        p_w, p_c, l = [], [], []
        for i in range(NA_ROWS_PER_STEP):
            delta = rr[i] - rs[i]
            bias = jnp.concatenate(
                [bias_ref[2 * jj - delta + NA_WIN_R - 1] for jj in range(NA_WIN_R // 2)], axis=-1)
            sw = s_w[i] + bias
            m = jnp.maximum(jnp.max(sw, axis=-1, keepdims=True), jnp.max(s_c[i], axis=-1, keepdims=True))
            pw = jnp.exp(sw - m)
            pc = jnp.exp(s_c[i] - m)
            l.append(jnp.sum(pw, axis=-1, keepdims=True) + jnp.sum(pc, axis=-1, keepdims=True))
            p_w.append(pw.astype(BF16))
            p_c.append(pc.astype(BF16))
        o_w = [_dot(p_w[i], vb[k_rows[i], :]) for i in range(NA_ROWS_PER_STEP)]
        o_c = [_dot(p_c[i], vcb[...]) for i in range(NA_ROWS_PER_STEP)]
        for i in range(NA_ROWS_PER_STEP):
            o_ref[q_rows[i], :] = ((o_w[i] + o_c[i]) / l[i]).astype(o_ref.dtype)
        return carry

    lax.fori_loop(0, rows // NA_ROWS_PER_STEP, body, 0)


def _na_bias_table(rpb):
    cols = jnp.arange(GRID_W)
    start = jnp.clip(cols - NA_WIN_C // 2, 0, GRID_W - NA_WIN_C)
    inside = (cols[None, :] >= start[:, None]) & (cols[None, :] < start[:, None] + NA_WIN_C)
    rel = cols[None, :] - cols[:, None] + NA_WIN_C - 1
    onehot = (rel[:, :, None] == jnp.arange(2 * NA_WIN_C - 1)).astype(F32)
    picked = jnp.einsum("hrm,ckm->hrck", rpb, onehot, precision=HIGHEST)
    tab = jnp.where(inside[None, None], picked, MASK_VALUE)
    return jnp.concatenate([tab[:, :-1], tab[:, 1:]], axis=-1)


def _neighbourhood(proj, n_batch, seq, k_ctx, v_ctx, bias):
    rows = seq // GRID_W
    past = k_ctx.shape[1]
    c_spec = pl.BlockSpec((None, past, HEAD_DIM), lambda b, h: (b, 0, h))

    def col_spec(col0):
        return pl.BlockSpec((seq, HEAD_DIM), lambda b, h: (b, col0 // HEAD_DIM + h))

    return pl.pallas_call(
        functools.partial(_na_kernel, rows=rows),
        grid=(n_batch, NA_H),
        in_specs=[
            col_spec(OD_NQ), col_spec(OD_NK), col_spec(OD_NV), c_spec, c_spec,
            pl.BlockSpec((None, 2 * NA_WIN_R - 2, GRID_W, 2 * GRID_W), lambda b, h: (h, 0, 0, 0)),
        ],
        out_specs=pl.BlockSpec((seq, HEAD_DIM), lambda b, h: (b, h)),
        out_shape=jax.ShapeDtypeStruct((n_batch * seq, NA_H * HEAD_DIM), BF16),
        scratch_shapes=[pltpu.VMEM((seq, HEAD_DIM), BF16), pltpu.VMEM((seq, HEAD_DIM), BF16),
                        pltpu.VMEM((past, HEAD_DIM), BF16), pltpu.VMEM((past, HEAD_DIM), BF16)],
        compiler_params=_params("parallel", "parallel"),
        name="neighbourhood",
    )(proj, proj, proj, k_ctx, v_ctx, bias)


def _order_masks(n=CHUNK):
    row = lax.broadcasted_iota(jnp.int32, (n, n), 0)
    col = lax.broadcasted_iota(jnp.int32, (n, n), 1)
    return row, col


def _chunk_cumsum(x, chunk, reverse):
    seq = x.shape[0]
    pos = lax.broadcasted_iota(jnp.int32, x.shape, 0) % chunk
    shift = 1
    while shift < chunk:
        if reverse:
            x = x + jnp.where(pos < chunk - shift, pltpu.roll(x, seq - shift, 0), 0.0)
        else:
            x = x + jnp.where(pos >= shift, pltpu.roll(x, shift, 0), 0.0)
        shift *= 2
    return x


def _gla_kernel(*refs, seq, has_s0, write_state):
    it = iter(refs)
    q_ref, k_ref, v_ref, gg_ref, lo_ref, w2_ref, b2_ref, gn_ref = (next(it) for _ in range(8))
    s0_ref = next(it) if has_s0 else None
    mix_ref = next(it)
    st_ref = next(it) if write_state else None
    o_s, bc_s = next(it), next(it)
    n_chunks = seq // CHUNK
    lane = lax.broadcasted_iota(jnp.int32, (1, LANES), 1)
    head_mask = [(lane < GLA_DK).astype(F32), (lane >= GLA_DK).astype(F32)]
    row, col = _order_masks()
    before = [col <= row, col >= row]

    lo = lo_ref[...].astype(BF16)
    for d in range(2):
        z = _dot(lo, w2_ref[d].astype(BF16)) + b2_ref[d]
        bc_s[d] = _chunk_cumsum(-_softplus(-z) * (1.0 / GLA_TAU), CHUNK, d == 1)

    chains = [(d, p) for d in range(2) for p in range(2)]
    cols = [slice(p * GLA_DV, (p + 1) * GLA_DV) for p in range(2)]
    steps_per_iter = min(GLA_STEPS, n_chunks)

    def body(i, carry):
        states = list(carry)
        pre = []
        for s in range(steps_per_iter):
            step = i * steps_per_iter + s
            rows = [pl.ds(pl.multiple_of(n * CHUNK, CHUNK), CHUNK) for n in (step, n_chunks - 1 - step)]
            qe, ke, kd, dec = [], [], [], []
            for d in range(2):
                bc = bc_s[d, rows[d], :]
                bl = bc[CHUNK - 1:CHUNK, :] if d == 0 else bc[0:1, :]
                k = k_ref[rows[d], :]
                qe.append(q_ref[rows[d], :] * (GLA_DK ** -0.5) * jnp.exp(bc))
                ke.append((k * jnp.exp(-bc)).astype(BF16))
                kd.append(k * jnp.exp(bl - bc))
                dec.append(jnp.exp(bl))
            qep = [(qe[d] * head_mask[p]).astype(BF16) for d, p in chains]
            kdp = [(kd[d] * head_mask[p]).astype(BF16) for d, p in chains]
            vp = [v_ref[rows[d], cols[p]].astype(BF16) for d, p in chains]
            sc = [jnp.where(before[d], _dot_nt(qep[c], ke[d]), 0.0).astype(BF16) for c, (d, p) in enumerate(chains)]
            kv = [_dot_tn(vp[c], kdp[c]) for c in range(4)]
            pre.append((rows, qep, vp, sc, kv, dec))
        for rows, qep, vp, sc, kv, dec in pre:
            inter = [_dot_nt(qep[c], states[c].astype(BF16)) for c in range(4)]
            for c, (d, p) in enumerate(chains):
                o_s[d, rows[d], cols[p]] = _dot(sc[c], vp[c]) + inter[c]
                states[c] = states[c] * dec[d] + kv[c]
        return tuple(states)

    init = []
    for d in range(2):
        if has_s0:
            s0t = s0_ref[d].T
            init += [s0t * head_mask[0], s0t * head_mask[1]]
        else:
            init += [jnp.zeros((GLA_DV, LANES), F32), jnp.zeros((GLA_DV, LANES), F32)]
    fin = lax.fori_loop(0, n_chunks // steps_per_iter, body, tuple(init))
    if write_state:
        for d in range(2):
            st_ref[d] = (fin[2 * d] + fin[2 * d + 1]).T

    for p in range(2):
        cols = slice(p * GLA_DV, (p + 1) * GLA_DV)
        y = _rms(o_s[0, :, cols] + o_s[1, :, cols], gn_ref[...]) * _silu(gg_ref[:, cols])
        mix_ref[:, cols] = y.astype(mix_ref.dtype)


def _gla(proj, n_batch, seq, w2pad, b2, gn, s0, write_state):
    pairs = GLA_H // 2
    pw = 2 * GLA_DV
    st_spec = pl.BlockSpec((None, 2, None, LANES, GLA_DV), lambda b, p: (b, 0, p, 0, 0))
    in_specs = [
        pl.BlockSpec((seq, LANES), lambda b, p: (b, EV_Q // LANES + p)),
        pl.BlockSpec((seq, LANES), lambda b, p: (b, EV_K // LANES + p)),
        pl.BlockSpec((seq, pw), lambda b, p: (b, EV_V // pw + p)),
        pl.BlockSpec((seq, pw), lambda b, p: (b, EV_G // pw + p)),
        pl.BlockSpec((seq, LANES), lambda b, p: (b, EV_LO // LANES)),
        pl.BlockSpec((2, LANES, LANES), lambda b, p: (0, 0, p)),
        pl.BlockSpec((2, 1, LANES), lambda b, p: (0, 0, p)),
        pl.BlockSpec((1, GLA_DV), lambda b, p: (0, 0)),
    ]
    args = [proj, proj, proj, proj, proj, w2pad, b2, gn]
    if s0 is not None:
        in_specs.append(st_spec)
        args.append(s0)
    mix_spec = pl.BlockSpec((seq, pw), lambda b, p: (b, p))
    mix_shape = jax.ShapeDtypeStruct((n_batch * seq, GLA_H * GLA_DV), BF16)
    if write_state:
        out_specs = [mix_spec, st_spec]
        out_shape = [mix_shape, jax.ShapeDtypeStruct((n_batch, 2, pairs, LANES, GLA_DV), F32)]
    else:
        out_specs, out_shape = mix_spec, mix_shape
    return pl.pallas_call(
        functools.partial(_gla_kernel, seq=seq, has_s0=s0 is not None, write_state=write_state),
        grid=(n_batch, pairs),
        in_specs=in_specs,
        out_specs=out_specs,
        out_shape=out_shape,
        scratch_shapes=[pltpu.VMEM((2, seq, pw), F32), pltpu.VMEM((2, seq, LANES), F32)],
        compiler_params=_params("parallel", "parallel"),
        name="gla",
    )(*args)


def _conv_silu(x, w):
    seq = x.shape[0]
    t = lax.broadcasted_iota(jnp.int32, x.shape, 0)
    prev = jnp.where(t == 0, 0.0, pltpu.roll(x, 1, 0))
    nxt = jnp.where(t == seq - 1, 0.0, pltpu.roll(x, seq - 1, 0))
    return _silu(prev * w[0:1, :] + x * w[1:2, :] + nxt * w[2:3, :])


def _hi_lo(x):
    hi = x.astype(BF16)
    return hi, (x - hi.astype(F32)).astype(BF16)


def _split_lanes(a):
    hi, lo = _hi_lo(a)
    return jnp.concatenate([hi, hi, lo], axis=1)


def _split_rows(b):
    hi, lo = _hi_lo(b)
    return jnp.concatenate([hi, lo, hi], axis=0)


def _l2norm(x):
    return x * lax.rsqrt(jnp.sum(x * x, axis=-1, keepdims=True) + EPS)


def _delta_kernel(*refs, seq, hp, cpi, has_s0, write_state):
    it = iter(refs)
    q_ref, k_ref, v_ref, z_ref, gates_ref, wq_ref, wk_ref, wv_ref, gn_ref = (next(it) for _ in range(9))
    s0_ref = next(it) if has_s0 else None
    mix_ref = next(it)
    st_ref = next(it) if write_state else None
    qs, ks, vs, o_s, gc_s, b_s, u0_s, kc_s, at_s, qe_s, kd_s = (next(it) for _ in range(11))
    n_chunks = seq // DN_CHUNK
    lane = lax.broadcasted_iota(jnp.int32, (1, LANES), 1)
    row, col = _order_masks(DN_CHUNK)
    eye = (row == col).astype(F32)
    before = [col <= row, col >= row]
    strict = [col < row, col > row]
    couples = [((row >> (lvl + 1)) == (col >> (lvl + 1))) & ((row >> lvl) != (col >> lvl))
               for lvl in range(DN_CHUNK.bit_length() - 1)]

    gates = gates_ref[...]
    for hd in range(hp):
        cols = slice(hd * HEAD_DIM, (hd + 1) * HEAD_DIM)
        h = pl.program_id(1) * hp + hd
        qs[hd] = _l2norm(_conv_silu(q_ref[:, cols], wq_ref[:, cols])) * (HEAD_DIM ** -0.5)
        ks[hd] = _l2norm(_conv_silu(k_ref[:, cols], wk_ref[:, cols]))
        vs[hd] = _conv_silu(v_ref[:, cols], wv_ref[:, cols])
        for d in range(2):
            sel_g = lane == d * DN_H + h
            sel_b = lane == 2 * DN_H + d * DN_H + h
            gc_s[hd, d] = jnp.broadcast_to(
                jnp.sum(jnp.where(sel_g, gates, 0.0), axis=-1, keepdims=True), (seq, LANES))
            b_s[hd, d] = jnp.broadcast_to(
                jnp.sum(jnp.where(sel_b, gates, 0.0), axis=-1, keepdims=True), (seq, LANES))

    def chunk_rows(n):
        return pl.ds(pl.multiple_of(n * DN_CHUNK, DN_CHUNK), DN_CHUNK)

    def last_row(d, n):
        return pl.ds(n * DN_CHUNK + (DN_CHUNK - 1 if d == 0 else 0), 1)

    def wy_stages(units):
        nu = range(len(units))
        rows = [chunk_rows(n) for _, _, n in units]
        gc = [gc_s[hd, d, rows[i], :] for i, (hd, d, _) in enumerate(units)]
        bb = [b_s[hd, d, rows[i], :] for i, (hd, d, _) in enumerate(units)]
        k = [ks[hd, rows[i], :] for i, (hd, _, _) in enumerate(units)]
        k16 = [k[i].astype(BF16) for i in nu]
        kbeta = [k[i] * bb[i] for i in nu]
        kk = [_dot_nt(kbeta[i].astype(BF16), k16[i]) for i in nu]
        qk = [_dot_nt(qs[hd, rows[i], :].astype(BF16), k16[i]) for i, (hd, _, _) in enumerate(units)]
        yield
        mm = []
        for i, (hd, d, _) in enumerate(units):
            decay = jnp.where(before[d], jnp.exp(jnp.where(before[d], gc[i] - gc[i].T, 0.0)), 0.0)
            mm.append(jnp.where(strict[d], kk[i] * decay, 0.0))
            at_s[hd, d, rows[i], :] = (qk[i] * decay).astype(BF16)
        inv = [eye - jnp.where(couples[0], mm[i], 0.0) for i in nu]
        for lvl in range(1, len(couples)):
            inv16 = [inv[i].astype(BF16) for i in nu]
            c_inv = [_dot(jnp.where(couples[lvl], mm[i], 0.0).astype(BF16), inv16[i]).astype(BF16) for i in nu]
            yield
            inv = [inv[i] - _dot(inv16[i], c_inv[i]) for i in nu]
            yield
        inv16 = [inv[i].astype(BF16) for i in nu]
        egc = [jnp.exp(gc[i]) for i in nu]
        rhs = [jnp.concatenate([vs[hd, rows[i], :] * bb[i], kbeta[i] * egc[i]], axis=-1)
               for i, (hd, _, _) in enumerate(units)]
        sol = [_dot(inv16[i], rhs[i].astype(BF16)) for i in nu]
        yield
        mm_split = [_split_lanes(mm[i]) for i in nu]
        for _ in range(DN_REFINE_STEPS):
            res = [rhs[i] - sol[i] - _dot(mm_split[i], _split_rows(sol[i])) for i in nu]
            yield
            sol = [sol[i] + _dot(inv16[i], res[i].astype(BF16)) for i in nu]
            yield
        for i, (hd, d, _) in enumerate(units):
            u0_s[hd, d, rows[i], :] = sol[i][:, :HEAD_DIM]
            kc_s[hd, d, rows[i], :] = sol[i][:, HEAD_DIM:].astype(BF16)
            g_last = gc[i][DN_CHUNK - 1:DN_CHUNK, :] if d == 0 else gc[i][0:1, :]
            qe_s[hd, d, rows[i], :] = (qs[hd, rows[i], :] * egc[i]).astype(BF16)
            kd_s[hd, d, rows[i], :] = (k[i] * jnp.exp(g_last - gc[i])).astype(BF16)

    chains = [(hd, d) for hd in range(hp) for d in range(2)]

    def chunk_of(d, step):
        return step if d == 0 else n_chunks - 1 - step

    def scan_stages(state, steps):
        for step in steps:
            ns = [chunk_of(d, step) for _, d in chains]
            rows = [chunk_rows(n) for n in ns]
            s16 = [st.astype(BF16) for st in state]
            ks_ = [_dot(kc_s[hd, d, rows[c], :], s16[c]) for c, (hd, d) in enumerate(chains)]
            qs_ = [_dot(qe_s[hd, d, rows[c], :], s16[c]) for c, (hd, d) in enumerate(chains)]
            yield
            u16 = [(u0_s[hd, d, rows[c], :] - ks_[c]).astype(BF16) for c, (hd, d) in enumerate(chains)]
            au = [_dot(at_s[hd, d, rows[c], :], u16[c]) for c, (hd, d) in enumerate(chains)]
            ku = [_dot_tn(kd_s[hd, d, rows[c], :], u16[c]) for c, (hd, d) in enumerate(chains)]
            yield
            for c, (hd, d) in enumerate(chains):
                o_s[hd, d, rows[c], :] = qs_[c] + au[c]
                state[c] = state[c] * jnp.exp(gc_s[hd, d, last_row(d, ns[c]), :]) + ku[c]

    def run(*stages):
        live = list(stages)
        while live:
            for g in list(live):
                try:
                    next(g)
                except StopIteration:
                    live.remove(g)

    def group_units(j):
        return [(hd, d, chunk_of(d, j * cpi + c)) for c in range(cpi) for hd in range(hp) for d in range(2)]

    def group_steps(j):
        return [j * cpi + c for c in range(cpi)]

    n_groups = n_chunks // cpi
    if has_s0:
        init = tuple(s0_ref[d, hd] for hd, d in chains)
    else:
        init = tuple(jnp.zeros((HEAD_DIM, HEAD_DIM), F32) for _ in chains)
    run(wy_stages(group_units(0)))

    def body(j, carry):
        state = list(carry)
        run(wy_stages(group_units(j)), scan_stages(state, group_steps(j - 1)))
        return tuple(state)

    fin = list(lax.fori_loop(1, n_groups, body, init))
    run(scan_stages(fin, group_steps(n_groups - 1)))
    if write_state:
        for c, (hd, d) in enumerate(chains):
            st_ref[d, hd] = fin[c]

    for hd in range(hp):
        cols = slice(hd * HEAD_DIM, (hd + 1) * HEAD_DIM)
        y = _rms(o_s[hd, 0] + o_s[hd, 1], gn_ref[...]) * _silu(z_ref[:, cols])
        mix_ref[:, cols] = y.astype(mix_ref.dtype)


def _dn_gates_kernel(ab_ref, alog_ref, dtb_ref, o_ref):
    ab = ab_ref[...]
    lane = lax.broadcasted_iota(jnp.int32, (1, LANES), 1)
    g = -jnp.exp(alog_ref[...]) * _softplus(ab + dtb_ref[...])
    fwd = _chunk_cumsum(g, DN_CHUNK, False)
    bwd = _chunk_cumsum(g, DN_CHUNK, True)
    o_ref[...] = jnp.where(lane < DN_H, fwd, jnp.where(lane < 2 * DN_H, bwd, _sigmoid(ab)))


def _dn_gates(proj, n_batch, seq, alog_row, dtb_row):
    vec = pl.BlockSpec((1, LANES), lambda b: (0, 0))
    return pl.pallas_call(
        _dn_gates_kernel,
        grid=(n_batch,),
        in_specs=[pl.BlockSpec((seq, LANES), lambda b: (b, OD_AB // LANES)), vec, vec],
        out_specs=pl.BlockSpec((seq, LANES), lambda b: (b, 0)),
        out_shape=jax.ShapeDtypeStruct((n_batch * seq, LANES), F32),
        compiler_params=_params("parallel"),
        name="dn_gates",
    )(proj, alog_row, dtb_row)


def _delta(proj, gates, n_batch, seq, conv_w, gn, s0, write_state):
    n_chunks = seq // DN_CHUNK
    cpi = min(n_chunks, DN_UNITS // 2)
    hp = max(1, min(DN_H, DN_UNITS // (2 * cpi), DN_TOKENS_PER_STEP // seq))
    hw = hp * HEAD_DIM

    def col_spec(col0):
        return pl.BlockSpec((seq, hw), lambda b, h: (b, col0 // hw + h))

    def conv_spec(part):
        return pl.BlockSpec((3, hw), lambda b, h: (0, part * (DN_H // hp) + h))

    vec = pl.BlockSpec((1, LANES), lambda b, h: (0, 0))
    st_spec = pl.BlockSpec((None, 2, hp, HEAD_DIM, HEAD_DIM), lambda b, h: (b, 0, h, 0, 0))
    in_specs = [col_spec(OD_DQ), col_spec(OD_DK), col_spec(OD_DV), col_spec(OD_DZ),
                pl.BlockSpec((seq, LANES), lambda b, h: (b, 0)),
                conv_spec(0), conv_spec(1), conv_spec(2), vec]
    args = [proj, proj, proj, proj, gates, conv_w, conv_w, conv_w, gn]
    if s0 is not None:
        in_specs.append(st_spec)
        args.append(s0)
    mix_spec = pl.BlockSpec((seq, hw), lambda b, h: (b, h))
    mix_shape = jax.ShapeDtypeStruct((n_batch * seq, DN_H * HEAD_DIM), BF16)
    if write_state:
        out_specs = [mix_spec, st_spec]
        out_shape = [mix_shape, jax.ShapeDtypeStruct((n_batch, 2, DN_H, HEAD_DIM, HEAD_DIM), F32)]
    else:
        out_specs, out_shape = mix_spec, mix_shape
    tok = pltpu.VMEM((hp, seq, HEAD_DIM), F32)
    both = pltpu.VMEM((hp, 2, seq, HEAD_DIM), F32)
    both16 = pltpu.VMEM((hp, 2, seq, HEAD_DIM), BF16)
    return pl.pallas_call(
        functools.partial(_delta_kernel, seq=seq, hp=hp, cpi=cpi, has_s0=s0 is not None, write_state=write_state),
        grid=(n_batch, DN_H // hp),
        in_specs=in_specs,
        out_specs=out_specs,
        out_shape=out_shape,
        scratch_shapes=[tok, tok, tok, both, both, both, both, both16, both16, both16, both16],
        compiler_params=_params("parallel", "arbitrary"),
        name="deltanet",
    )(*args)


def _rope_tables(n_tok):
    t = jnp.arange(n_tok)
    inv = 1.0 / (ROPE_THETA ** (jnp.arange(ROPE_FREQ, dtype=F32) / ROPE_FREQ))
    pos = jnp.stack([t // GRID_W, t % GRID_W], axis=1).astype(F32)
    ang = pos[:, :, None] * inv
    cos = jnp.concatenate([jnp.cos(ang), jnp.cos(ang)], axis=-1).reshape(n_tok, HEAD_DIM)
    sin = jnp.concatenate([-jnp.sin(ang), jnp.sin(ang)], axis=-1).reshape(n_tok, HEAD_DIM)
    return cos, sin


def _pad_lanes(v):
    return jnp.pad(v.reshape(1, -1), ((0, 0), (0, LANES - v.size)))


def kernel(x_prompt, x_sample, state_gla, cache_gqa_k, cache_gqa_v, cache_na_k, cache_na_v, state_delta, c, c_ctx, norm1, norm2, w_ada, b_ada, w_mlp1, w_mlp2, ev_w_in, ev_w_a2, ev_b_a2, ev_gla_norm, ev_q_norm, ev_k_norm, ev_w_out, od_w_in, od_conv, od_a_log, od_dt_bias, od_dn_norm, od_rpb, od_w_out, norm_f):
    n_ctx, seq_ctx, d = x_prompt.shape
    n_lat, seq_lat, _ = x_sample.shape
    depth = w_ada.shape[0]

    w_ev = ev_w_in[0]
    w_ev = jnp.concatenate(
        [w_ev[:, :EV_AQ], w_ev[:, EV_AQ + 2 * GLA_RANK:], w_ev[:, EV_AQ:EV_AQ + 2 * GLA_RANK],
         jnp.zeros((d, EV_COLS_PAD - w_ev.shape[1]), F32)], axis=1).astype(BF16)
    w_od = jnp.pad(od_w_in[0], ((0, 0), (0, OD_COLS_PAD - od_w_in.shape[2]))).astype(BF16)
    w_in = [w_ev, w_od]
    w_out = [ev_w_out[0].astype(BF16), od_w_out[0].astype(BF16)]
    w1 = w_mlp1.astype(BF16)
    w2 = w_mlp2.astype(BF16)
    w2pad = jnp.zeros((2, LANES, GLA_H * GLA_DK), F32)
    for dd in range(2):
        w2pad = w2pad.at[dd, dd * GLA_RANK:(dd + 1) * GLA_RANK].set(ev_w_a2[0, dd])
    b2 = ev_b_a2[0].reshape(2, 1, GLA_H * GLA_DK)
    alog_row = _pad_lanes(od_a_log[0])
    dtb_row = _pad_lanes(od_dt_bias[0])
    na_bias = _na_bias_table(od_rpb[0])
    rope = _rope_tables(seq_lat)

    cond8 = jnp.concatenate([c_ctx[None, :], c, jnp.zeros((8 - 1 - n_lat, d), F32)], axis=0)
    mod = _adaln(cond8, w_ada, b_ada).reshape(depth, 8, 6, 1, d)

    def trunk(x, n_batch, seq, row_fn, caches):
        latent = caches is not None
        tm = TM_MLP
        outs = {}
        proj = _inproj(x, norm1[0:1], mod, 0, row_fn, w_in[0], TM_INPROJ)
        if latent:
            s0 = caches["gla"].reshape(n_batch, 2, GLA_H // 2, LANES, GLA_DV)
            mix_a = _gla(proj, n_batch, seq, w2pad, b2, ev_gla_norm, s0, False)
            mix_b = _attention(proj, n_batch, seq, EV_AQ, EV_AK, EV_AV, GQA_KV, GQA_H // GQA_KV,
                               norms=(ev_q_norm, ev_k_norm), rope=rope, ctx=caches["gqa"])
        else:
            mix_a, st = _gla(proj, n_batch, seq, w2pad, b2, ev_gla_norm, None, True)
            mix_b, k_new = _attention(proj, n_batch, seq, EV_AQ, EV_AK, EV_AV, GQA_KV, GQA_H // GQA_KV,
                                      norms=(ev_q_norm, ev_k_norm), write_k=True)
            outs["st_gla"] = st.reshape(n_batch, 1, 2, GLA_H, GLA_DK, GLA_DV)
            outs["ck_gqa"] = k_new.reshape(n_batch, 1, seq, GQA_KV, HEAD_DIM)
            outs["cv_gqa"] = proj[:, EV_AV:EV_AV + GQA_KV * HEAD_DIM].reshape(n_batch, 1, seq, GQA_KV, HEAD_DIM)
        x, h = _outproj(mix_a, mix_b, w_out[0], x, norm2[0:1], mod, 0, row_fn, tm)
        x = _mlp(x, h, mod, 0, row_fn, w1, w2, norm_f[None, :], False, tm)
        proj = _inproj(x, norm1[1:2], mod, 1, row_fn, w_in[1], TM_INPROJ)
        if latent:
            mix_a = _neighbourhood(proj, n_batch, seq, caches["na"][0], caches["na"][1], na_bias)
            s0 = caches["delta"].reshape(n_batch, 2, DN_H, HEAD_DIM, HEAD_DIM)
            gates = _dn_gates(proj, n_batch, seq, alog_row, dtb_row)
            mix_b = _delta(proj, gates, n_batch, seq, od_conv[0], od_dn_norm, s0, False)
        else:
            mix_a = _attention(proj, n_batch, seq, OD_NQ, OD_NK, OD_NV, NA_H, 1)
            gates = _dn_gates(proj, n_batch, seq, alog_row, dtb_row)
            mix_b, st = _delta(proj, gates, n_batch, seq, od_conv[0], od_dn_norm, None, True)
            outs["ck_na"] = proj[:, OD_NK:OD_NK + NA_H * HEAD_DIM].reshape(n_batch, 1, seq, NA_H, HEAD_DIM)
            outs["cv_na"] = proj[:, OD_NV:OD_NV + NA_H * HEAD_DIM].reshape(n_batch, 1, seq, NA_H, HEAD_DIM)
            outs["st_dn"] = st.reshape(n_batch, 1, 2, DN_H, HEAD_DIM, HEAD_DIM)
        x, h = _outproj(mix_a, mix_b, w_out[1], x, norm2[1:2], mod, 1, row_fn, tm)
        y = _mlp(x, h, mod, 1, row_fn, w1, w2, norm_f[None, :], True, tm)
        return y.reshape(n_batch, seq, d), outs

    y_prompt, new = trunk(x_prompt.reshape(n_ctx * seq_ctx, d), n_ctx, seq_ctx, lambda tok: 0, None)
    past = cache_gqa_k.shape[2]
    caches = {
        "gla": state_gla[:, 0],
        "gqa": (cache_gqa_k[:, 0].reshape(n_lat, past, GQA_KV * HEAD_DIM),
                cache_gqa_v[:, 0].reshape(n_lat, past, GQA_KV * HEAD_DIM)),
        "na": (cache_na_k[:, 0].reshape(n_lat, past, NA_H * HEAD_DIM),
               cache_na_v[:, 0].reshape(n_lat, past, NA_H * HEAD_DIM)),
        "delta": state_delta[:, 0],
    }
    y_sample, _ = trunk(x_sample.reshape(n_lat * seq_lat, d), n_lat, seq_lat,
                        lambda tok: 1 + tok // seq_lat, caches)
    return (y_prompt, y_sample, new["st_gla"], new["ck_gqa"], new["cv_gqa"], new["ck_na"], new["cv_na"],
            new["st_dn"])
```

```python
import functools

import jax
import jax.numpy as jnp
import numpy as np
from jax import lax
from jax.experimental import pallas as pl
from jax.experimental.pallas import tpu as pltpu

F32 = jnp.float32
BF16 = jnp.bfloat16
HIGHEST = lax.Precision.HIGHEST

D_MODEL = 2048
D_FF = 4 * D_MODEL
HEAD_DIM = 128
LANES = 128
GRID_W = 64
GLA_H = 8
GLA_DK = 64
GLA_DV = 128
GLA_RANK = 16
GLA_TAU = 16.0
GQA_H = 8
GQA_KV = 2
NA_H = 8
NA_WIN_R = 8
NA_WIN_C = 16
DN_H = 8
CHUNK = 64
DN_CHUNK = 128
DN_REFINE_STEPS = 1
DN_UNITS = 16
DN_TOKENS_PER_STEP = 2048
TN_ADALN = 2048
TM_INPROJ = 1024
TN_INPROJ = 1280
TM_MLP = 512
TF_MLP = 1024
ATTN_HEADS_PER_STEP = 8
GLA_STEPS = 8
NA_ROWS_PER_STEP = 16
ROPE_THETA = 10000.0
ROPE_FREQ = HEAD_DIM // 4
EPS = 1e-6
MASK_VALUE = -1e30

EV_Q, EV_K, EV_V, EV_G = 0, 512, 1024, 2048
EV_AQ, EV_AK, EV_AV, EV_LO = 3072, 4096, 4352, 4608
EV_COLS_PAD = 5120
OD_NQ, OD_NK, OD_NV = 0, 1024, 2048
OD_DQ, OD_DK, OD_DV, OD_DZ, OD_AB = 3072, 4096, 5120, 6144, 7168
OD_COLS_PAD = 7680

VMEM_LIMIT = 56 * 1024 * 1024


def _params(*sem):
    return pltpu.CompilerParams(dimension_semantics=sem, vmem_limit_bytes=VMEM_LIMIT)


def _dot(a, b):
    return jnp.dot(a, b, preferred_element_type=F32)


def _dot_nt(a, b):
    return lax.dot_general(a, b, (((1,), (1,)), ((), ())), preferred_element_type=F32)


def _dot_tn(a, b):
    return lax.dot_general(a, b, (((0,), (0,)), ((), ())), preferred_element_type=F32)


def _sigmoid(x):
    return 1.0 / (1.0 + jnp.exp(-x))


def _silu(x):
    return x * _sigmoid(x)


def _softplus(x):
    return jnp.maximum(x, 0.0) + jnp.log(1.0 + jnp.exp(-jnp.abs(x)))


def _rms(x, g):
    return x * lax.rsqrt(jnp.mean(x * x, axis=-1, keepdims=True) + EPS) * g


def _rms_mod(x, g, sc, sh):
    return _rms(x, g) * (1.0 + sc) + sh


def _adaln_kernel(c_ref, w_ref, b_ref, o_ref):
    a = _silu(c_ref[...]).astype(BF16)
    o_ref[...] = _dot(a, w_ref[...].astype(BF16)) + b_ref[...]


def _adaln(cond8, w_ada, b_ada):
    depth, d, n = w_ada.shape
    tn = TN_ADALN
    return pl.pallas_call(
        _adaln_kernel,
        grid=(depth, n // tn),
        in_specs=[
            pl.BlockSpec((8, d), lambda l, j: (0, 0)),
            pl.BlockSpec((None, d, tn), lambda l, j: (l, 0, j)),
            pl.BlockSpec((None, 1, tn), lambda l, j: (l, 0, j)),
        ],
        out_specs=pl.BlockSpec((None, 8, tn), lambda l, j: (l, 0, j)),
        out_shape=jax.ShapeDtypeStruct((depth, 8, n), F32),
        compiler_params=_params("parallel", "parallel"),
        name="adaln",
    )(cond8, w_ada, b_ada.reshape(depth, 1, n))


def _mod_spec(layer, k, row_fn, tm):
    return pl.BlockSpec((None, None, None, 1, D_MODEL), lambda i, j: (layer, row_fn(i * tm), k, 0, 0))


def _inproj_kernel(x_ref, g_ref, sc_ref, sh_ref, w_ref, o_ref, h_ref):
    @pl.when(pl.program_id(1) == 0)
    def _():
        h_ref[...] = _rms_mod(x_ref[...], g_ref[...], sc_ref[...], sh_ref[...]).astype(BF16)

    o_ref[...] = _dot(h_ref[...], w_ref[...])


def _inproj(x, g, mod, layer, row_fn, w, tm):
    m, d = x.shape
    n = w.shape[1]
    tn = TN_INPROJ
    return pl.pallas_call(
        _inproj_kernel,
        grid=(m // tm, n // tn),
        in_specs=[
            pl.BlockSpec((tm, d), lambda i, j: (i, 0)),
            pl.BlockSpec((1, d), lambda i, j: (0, 0)),
            _mod_spec(layer, 1, row_fn, tm),
            _mod_spec(layer, 0, row_fn, tm),
            pl.BlockSpec((d, tn), lambda i, j: (0, j)),
        ],
        out_specs=pl.BlockSpec((tm, tn), lambda i, j: (i, j)),
        out_shape=jax.ShapeDtypeStruct((m, n), F32),
        scratch_shapes=[pltpu.VMEM((tm, d), BF16)],
        compiler_params=_params("parallel", "arbitrary"),
        name="inproj",
    )(x, g, mod, mod, w)


def _outproj_kernel(a_ref, b_ref, wa_ref, wb_ref, x_ref, gt_ref, g_ref, sc_ref, sh_ref, o_ref, h_ref):
    acc = _dot(a_ref[...], wa_ref[...]) + _dot(b_ref[...], wb_ref[...])
    y = x_ref[...] + gt_ref[...] * acc
    o_ref[...] = y
    h_ref[...] = _rms_mod(y, g_ref[...], sc_ref[...], sh_ref[...]).astype(BF16)


def _outproj(mix_a, mix_b, w, x, g2, mod, layer, row_fn, tm):
    m, d = x.shape
    ka, kb = mix_a.shape[1], mix_b.shape[1]
    row = pl.BlockSpec((tm, d), lambda i, j: (i, 0))
    return pl.pallas_call(
        _outproj_kernel,
        grid=(m // tm, 1),
        in_specs=[
            pl.BlockSpec((tm, ka), lambda i, j: (i, 0)),
            pl.BlockSpec((tm, kb), lambda i, j: (i, 0)),
            pl.BlockSpec((ka, d), lambda i, j: (0, 0)),
            pl.BlockSpec((kb, d), lambda i, j: (1, 0)),
            row,
            _mod_spec(layer, 2, row_fn, tm),
            pl.BlockSpec((1, d), lambda i, j: (0, 0)),
            _mod_spec(layer, 4, row_fn, tm),
            _mod_spec(layer, 3, row_fn, tm),
        ],
        out_specs=[row, row],
        out_shape=[jax.ShapeDtypeStruct((m, d), F32), jax.ShapeDtypeStruct((m, d), BF16)],
        compiler_params=_params("parallel", "arbitrary"),
        name="outproj",
    )(mix_a, mix_b, w, w, x, mod, g2, mod, mod)


def _mlp_kernel(x_ref, h_ref, gt_ref, w1_ref, w2_ref, gf_ref, o_ref, acc_ref, *, final_norm):
    j = pl.program_id(1)

    @pl.when(j == 0)
    def _():
        acc_ref[...] = jnp.zeros_like(acc_ref)

    hid = jnp.maximum(_dot(h_ref[...], w1_ref[...]), 0.0)
    acc_ref[...] += _dot((hid * hid).astype(BF16), w2_ref[...])

    @pl.when(j == pl.num_programs(1) - 1)
    def _():
        y = x_ref[...] + gt_ref[...] * acc_ref[...]
        if final_norm:
            y = _rms(y, gf_ref[...])
        o_ref[...] = y


def _mlp(x, h, mod, layer, row_fn, w1, w2, gf, final_norm, tm):
    m, d = x.shape
    ff = w1.shape[2]
    tf = TF_MLP
    return pl.pallas_call(
        functools.partial(_mlp_kernel, final_norm=final_norm),
        grid=(m // tm, ff // tf),
        in_specs=[
            pl.BlockSpec((tm, d), lambda i, j: (i, 0)),
            pl.BlockSpec((tm, d), lambda i, j: (i, 0)),
            _mod_spec(layer, 5, row_fn, tm),
            pl.BlockSpec((None, d, tf), lambda i, j: (layer, 0, j)),
            pl.BlockSpec((None, tf, d), lambda i, j: (layer, j, 0)),
            pl.BlockSpec((1, d), lambda i, j: (0, 0)),
        ],
        out_specs=pl.BlockSpec((tm, d), lambda i, j: (i, 0)),
        out_shape=jax.ShapeDtypeStruct((m, d), F32),
        scratch_shapes=[pltpu.VMEM((tm, d), F32)],
        compiler_params=_params("parallel", "arbitrary"),
        name="mlp",
    )(x, h, mod, w1, w2, gf)


def _rope(x, cos, sin):
    lane = lax.broadcasted_iota(jnp.int32, x.shape, 1)
    first_half = (lane % (2 * ROPE_FREQ)) < ROPE_FREQ
    partner = jnp.where(first_half, pltpu.roll(x, LANES - ROPE_FREQ, 1), pltpu.roll(x, ROPE_FREQ, 1))
    return x * cos + partner * sin


def _attn_kernel(*refs, nkv, group, use_norm, use_rope, has_ctx, write_k):
    it = iter(refs)
    q_ref, k_ref, v_ref = next(it), next(it), next(it)
    qn_ref = kn_ref = cq_ref = sq_ref = ck_ref = sk_ref = kc_ref = vc_ref = knew_ref = None
    if use_norm:
        qn_ref, kn_ref = next(it), next(it)
    if use_rope:
        cq_ref, sq_ref, ck_ref, sk_ref = next(it), next(it), next(it), next(it)
    if has_ctx:
        kc_ref, vc_ref = next(it), next(it)
    o_ref = next(it)
    if write_k:
        knew_ref = next(it)
    kbuf, vbuf = next(it), next(it)
    scale = HEAD_DIM ** -0.5 * float(np.log2(np.e))

    kcols = [slice(kv * HEAD_DIM, (kv + 1) * HEAD_DIM) for kv in range(nkv)]

    @pl.when(pl.program_id(2) == 0)
    def _():
        for kv in range(nkv):
            k = k_ref[:, kcols[kv]]
            if use_norm:
                k = _rms(k, kn_ref[...])
            if write_k:
                knew_ref[:, kcols[kv]] = k
            if use_rope:
                k = _rope(k, ck_ref[...], sk_ref[...])
            kbuf[:, kcols[kv]] = k.astype(BF16)
        vbuf[...] = v_ref[...].astype(BF16)

    heads = range(nkv * group)
    kv_of = [h // group for h in heads]
    cols = [slice(h * HEAD_DIM, (h + 1) * HEAD_DIM) for h in heads]
    kc16 = kc_ref[...].astype(BF16) if has_ctx else None
    vc16 = vc_ref[...].astype(BF16) if has_ctx else None

    def scores(h):
        q = q_ref[:, cols[h]]
        if use_norm:
            q = _rms(q, qn_ref[...])
        if use_rope:
            q = _rope(q, cq_ref[...], sq_ref[...])
        qb = (q * scale).astype(BF16)
        s = _dot_nt(qb, kbuf[:, kcols[kv_of[h]]])
        s_c = _dot_nt(qb, kc16[:, kcols[kv_of[h]]]) if has_ctx else None
        return s, s_c

    def finish(h, s, s_c):
        m = jnp.max(s, axis=-1, keepdims=True)
        if has_ctx:
            m = jnp.maximum(m, jnp.max(s_c, axis=-1, keepdims=True))
        p = jnp.exp2(s - m)
        l = jnp.sum(p, axis=-1, keepdims=True)
        o = _dot(p.astype(BF16), vbuf[:, kcols[kv_of[h]]])
        if has_ctx:
            p_c = jnp.exp2(s_c - m)
            l = l + jnp.sum(p_c, axis=-1, keepdims=True)
            o = o + _dot(p_c.astype(BF16), vc16[:, kcols[kv_of[h]]])
        o_ref[:, cols[h]] = (o / l).astype(o_ref.dtype)

    pending = scores(0)
    for h in heads:
        following = scores(h + 1) if h + 1 < len(heads) else None
        finish(h, *pending)
        pending = following


def _attention(proj, n_batch, seq, q_col, k_col, v_col, n_kv, group, norms=None, rope=None, ctx=None,
               write_k=False):
    tq = min(seq, 256)
    nq = seq // tq
    nkv = max(1, min(n_kv, ATTN_HEADS_PER_STEP // group)) if nq == 1 else 1
    gw = nkv * group * HEAD_DIM
    kw = nkv * HEAD_DIM
    in_specs = [
        pl.BlockSpec((tq, gw), lambda b, h, i: (b * nq + i, q_col // gw + h)),
        pl.BlockSpec((seq, kw), lambda b, h, i: (b, k_col // kw + h)),
        pl.BlockSpec((seq, kw), lambda b, h, i: (b, v_col // kw + h)),
    ]
    args = [proj, proj, proj]
    vec = pl.BlockSpec((1, HEAD_DIM), lambda b, h, i: (0, 0))
    if norms is not None:
        in_specs += [vec, vec]
        args += [norms[0], norms[1]]
    if rope is not None:
        tab_q = pl.BlockSpec((tq, HEAD_DIM), lambda b, h, i: (i, 0))
        tab_k = pl.BlockSpec((seq, HEAD_DIM), lambda b, h, i: (0, 0))
        in_specs += [tab_q, tab_q, tab_k, tab_k]
        args += [rope[0], rope[1], rope[0], rope[1]]
    if ctx is not None:
        past = ctx[0].shape[1]
        c_spec = pl.BlockSpec((None, past, kw), lambda b, h, i: (b, 0, h))
        in_specs += [c_spec, c_spec]
        args += [ctx[0], ctx[1]]
    o_spec = pl.BlockSpec((tq, gw), lambda b, h, i: (b * nq + i, h))
    o_shape = jax.ShapeDtypeStruct((n_batch * seq, n_kv * group * HEAD_DIM), BF16)
    if write_k:
        out_specs = [o_spec, pl.BlockSpec((seq, kw), lambda b, h, i: (b, h))]
        out_shape = [o_shape, jax.ShapeDtypeStruct((n_batch * seq, n_kv * HEAD_DIM), F32)]
    else:
        out_specs, out_shape = o_spec, o_shape
    return pl.pallas_call(
        functools.partial(_attn_kernel, nkv=nkv, group=group, use_norm=norms is not None, use_rope=rope is not None,
                          has_ctx=ctx is not None, write_k=write_k),
        grid=(n_batch, n_kv // nkv, nq),
        in_specs=in_specs,
        out_specs=out_specs,
        out_shape=out_shape,
        scratch_shapes=[pltpu.VMEM((seq, kw), BF16), pltpu.VMEM((seq, kw), BF16)],
        compiler_params=_params("parallel", "parallel", "arbitrary"),
        name="attention",
    )(*args)


def _na_row_start(r, rows):
    return jnp.clip(r - NA_WIN_R // 2, 0, rows - NA_WIN_R)


def _na_kernel(q_ref, k_ref, v_ref, kc_ref, vc_ref, bias_ref, o_ref, kb, vb, kcb, vcb, *, rows):
    win = NA_WIN_R * GRID_W
    scale = HEAD_DIM ** -0.5
    kb[...] = k_ref[...].astype(BF16)
    vb[...] = v_ref[...].astype(BF16)
    kcb[...] = kc_ref[...].astype(BF16)
    vcb[...] = vc_ref[...].astype(BF16)

    def body(j, carry):
        rr = [j * NA_ROWS_PER_STEP + i for i in range(NA_ROWS_PER_STEP)]
        rs = [_na_row_start(r, rows) for r in rr]
        k_rows = [pl.ds(pl.multiple_of(s * GRID_W, GRID_W), win) for s in rs]
        q_rows = [pl.ds(pl.multiple_of(r * GRID_W, GRID_W), GRID_W) for r in rr]
        qb = [q_ref[q_rows[i], :].astype(BF16) for i in range(NA_ROWS_PER_STEP)]
        s_w = [_dot_nt(qb[i], kb[k_rows[i], :]) * scale for i in range(NA_ROWS_PER_STEP)]
        s_c = [_dot_nt(qb[i], kcb[...]) * scale for i in range(NA_ROWS_PER_STEP)]
        p_w, p_c, l = [], [], []
        for i in range(NA_ROWS_PER_STEP):
            delta = rr[i] - rs[i]
            bias = jnp.concatenate(
                [bias_ref[2 * jj - delta + NA_WIN_R - 1] for jj in range(NA_WIN_R // 2)], axis=-1)
            sw = s_w[i] + bias
            m = jnp.maximum(jnp.max(sw, axis=-1, keepdims=True), jnp.max(s_c[i], axis=-1, keepdims=True))
            pw = jnp.exp(sw - m)
            pc = jnp.exp(s_c[i] - m)
            l.append(jnp.sum(pw, axis=-1, keepdims=True) + jnp.sum(pc, axis=-1, keepdims=True))
            p_w.append(pw.astype(BF16))
            p_c.append(pc.astype(BF16))
        o_w = [_dot(p_w[i], vb[k_rows[i], :]) for i in range(NA_ROWS_PER_STEP)]
        o_c = [_dot(p_c[i], vcb[...]) for i in range(NA_ROWS_PER_STEP)]
        for i in range(NA_ROWS_PER_STEP):
            o_ref[q_rows[i], :] = ((o_w[i] + o_c[i]) / l[i]).astype(o_ref.dtype)
        return carry

    lax.fori_loop(0, rows // NA_ROWS_PER_STEP, body, 0)


def _na_bias_table(rpb):
    cols = jnp.arange(GRID_W)
    start = jnp.clip(cols - NA_WIN_C // 2, 0, GRID_W - NA_WIN_C)
    inside = (cols[None, :] >= start[:, None]) & (cols[None, :] < start[:, None] + NA_WIN_C)
    rel = cols[None, :] - cols[:, None] + NA_WIN_C - 1
    onehot = (rel[:, :, None] == jnp.arange(2 * NA_WIN_C - 1)).astype(F32)
    picked = jnp.einsum("hrm,ckm->hrck", rpb, onehot, precision=HIGHEST)
    tab = jnp.where(inside[None, None], picked, MASK_VALUE)
    return jnp.concatenate([tab[:, :-1], tab[:, 1:]], axis=-1)


def _neighbourhood(proj, n_batch, seq, k_ctx, v_ctx, bias):
    rows = seq // GRID_W
    past = k_ctx.shape[1]
    c_spec = pl.BlockSpec((None, past, HEAD_DIM), lambda b, h: (b, 0, h))

    def col_spec(col0):
        return pl.BlockSpec((seq, HEAD_DIM), lambda b, h: (b, col0 // HEAD_DIM + h))

    return pl.pallas_call(
        functools.partial(_na_kernel, rows=rows),
        grid=(n_batch, NA_H),
        in_specs=[
            col_spec(OD_NQ), col_spec(OD_NK), col_spec(OD_NV), c_spec, c_spec,
            pl.BlockSpec((None, 2 * NA_WIN_R - 2, GRID_W, 2 * GRID_W), lambda b, h: (h, 0, 0, 0)),
        ],
        out_specs=pl.BlockSpec((seq, HEAD_DIM), lambda b, h: (b, h)),
        out_shape=jax.ShapeDtypeStruct((n_batch * seq, NA_H * HEAD_DIM), BF16),
        scratch_shapes=[pltpu.VMEM((seq, HEAD_DIM), BF16), pltpu.VMEM((seq, HEAD_DIM), BF16),
                        pltpu.VMEM((past, HEAD_DIM), BF16), pltpu.VMEM((past, HEAD_DIM), BF16)],
        compiler_params=_params("parallel", "parallel"),
        name="neighbourhood",
    )(proj, proj, proj, k_ctx, v_ctx, bias)


def _order_masks(n=CHUNK):
    row = lax.broadcasted_iota(jnp.int32, (n, n), 0)
    col = lax.broadcasted_iota(jnp.int32, (n, n), 1)
    return row, col


def _chunk_cumsum(x, chunk, reverse):
    seq = x.shape[0]
    pos = lax.broadcasted_iota(jnp.int32, x.shape, 0) % chunk
    shift = 1
    while shift < chunk:
        if reverse:
            x = x + jnp.where(pos < chunk - shift, pltpu.roll(x, seq - shift, 0), 0.0)
        else:
            x = x + jnp.where(pos >= shift, pltpu.roll(x, shift, 0), 0.0)
        shift *= 2
    return x


def _gla_kernel(*refs, seq, has_s0, write_state):
    it = iter(refs)
    q_ref, k_ref, v_ref, gg_ref, lo_ref, w2_ref, b2_ref, gn_ref = (next(it) for _ in range(8))
    s0_ref = next(it) if has_s0 else None
    mix_ref = next(it)
    st_ref = next(it) if write_state else None
    o_s, bc_s = next(it), next(it)
    n_chunks = seq // CHUNK
    lane = lax.broadcasted_iota(jnp.int32, (1, LANES), 1)
    head_mask = [(lane < GLA_DK).astype(F32), (lane >= GLA_DK).astype(F32)]
    row, col = _order_masks()
    before = [col <= row, col >= row]

    lo = lo_ref[...].astype(BF16)
    for d in range(2):
        z = _dot(lo, w2_ref[d].astype(BF16)) + b2_ref[d]
        bc_s[d] = _chunk_cumsum(-_softplus(-z) * (1.0 / GLA_TAU), CHUNK, d == 1)

    chains = [(d, p) for d in range(2) for p in range(2)]
    cols = [slice(p * GLA_DV, (p + 1) * GLA_DV) for p in range(2)]
    steps_per_iter = min(GLA_STEPS, n_chunks)

    def body(i, carry):
        states = list(carry)
        pre = []
        for s in range(steps_per_iter):
            step = i * steps_per_iter + s
            rows = [pl.ds(pl.multiple_of(n * CHUNK, CHUNK), CHUNK) for n in (step, n_chunks - 1 - step)]
            qe, ke, kd, dec = [], [], [], []
            for d in range(2):
                bc = bc_s[d, rows[d], :]
                bl = bc[CHUNK - 1:CHUNK, :] if d == 0 else bc[0:1, :]
                k = k_ref[rows[d], :]
                qe.append(q_ref[rows[d], :] * (GLA_DK ** -0.5) * jnp.exp(bc))
                ke.append((k * jnp.exp(-bc)).astype(BF16))
                kd.append(k * jnp.exp(bl - bc))
                dec.append(jnp.exp(bl))
            qep = [(qe[d] * head_mask[p]).astype(BF16) for d, p in chains]
            kdp = [(kd[d] * head_mask[p]).astype(BF16) for d, p in chains]
            vp = [v_ref[rows[d], cols[p]].astype(BF16) for d, p in chains]
            sc = [jnp.where(before[d], _dot_nt(qep[c], ke[d]), 0.0).astype(BF16) for c, (d, p) in enumerate(chains)]
            kv = [_dot_tn(vp[c], kdp[c]) for c in range(4)]
            pre.append((rows, qep, vp, sc, kv, dec))
        for rows, qep, vp, sc, kv, dec in pre:
            inter = [_dot_nt(qep[c], states[c].astype(BF16)) for c in range(4)]
            for c, (d, p) in enumerate(chains):
                o_s[d, rows[d], cols[p]] = _dot(sc[c], vp[c]) + inter[c]
                states[c] = states[c] * dec[d] + kv[c]
        return tuple(states)

    init = []
    for d in range(2):
        if has_s0:
            s0t = s0_ref[d].T
            init += [s0t * head_mask[0], s0t * head_mask[1]]
        else:
            init += [jnp.zeros((GLA_DV, LANES), F32), jnp.zeros((GLA_DV, LANES), F32)]
    fin = lax.fori_loop(0, n_chunks // steps_per_iter, body, tuple(init))
    if write_state:
        for d in range(2):
            st_ref[d] = (fin[2 * d] + fin[2 * d + 1]).T

    for p in range(2):
        cols = slice(p * GLA_DV, (p + 1) * GLA_DV)
        y = _rms(o_s[0, :, cols] + o_s[1, :, cols], gn_ref[...]) * _silu(gg_ref[:, cols])
        mix_ref[:, cols] = y.astype(mix_ref.dtype)


def _gla(proj, n_batch, seq, w2pad, b2, gn, s0, write_state):
    pairs = GLA_H // 2
    pw = 2 * GLA_DV
    st_spec = pl.BlockSpec((None, 2, None, LANES, GLA_DV), lambda b, p: (b, 0, p, 0, 0))
    in_specs = [
        pl.BlockSpec((seq, LANES), lambda b, p: (b, EV_Q // LANES + p)),
        pl.BlockSpec((seq, LANES), lambda b, p: (b, EV_K // LANES + p)),
        pl.BlockSpec((seq, pw), lambda b, p: (b, EV_V // pw + p)),
        pl.BlockSpec((seq, pw), lambda b, p: (b, EV_G // pw + p)),
        pl.BlockSpec((seq, LANES), lambda b, p: (b, EV_LO // LANES)),
        pl.BlockSpec((2, LANES, LANES), lambda b, p: (0, 0, p)),
        pl.BlockSpec((2, 1, LANES), lambda b, p: (0, 0, p)),
        pl.BlockSpec((1, GLA_DV), lambda b, p: (0, 0)),
    ]
    args = [proj, proj, proj, proj, proj, w2pad, b2, gn]
    if s0 is not None:
        in_specs.append(st_spec)
        args.append(s0)
    mix_spec = pl.BlockSpec((seq, pw), lambda b, p: (b, p))
    mix_shape = jax.ShapeDtypeStruct((n_batch * seq, GLA_H * GLA_DV), BF16)
    if write_state:
        out_specs = [mix_spec, st_spec]
        out_shape = [mix_shape, jax.ShapeDtypeStruct((n_batch, 2, pairs, LANES, GLA_DV), F32)]
    else:
        out_specs, out_shape = mix_spec, mix_shape
    return pl.pallas_call(
        functools.partial(_gla_kernel, seq=seq, has_s0=s0 is not None, write_state=write_state),
        grid=(n_batch, pairs),
        in_specs=in_specs,
        out_specs=out_specs,
        out_shape=out_shape,
        scratch_shapes=[pltpu.VMEM((2, seq, pw), F32), pltpu.VMEM((2, seq, LANES), F32)],
        compiler_params=_params("parallel", "parallel"),
        name="gla",
    )(*args)


def _conv_silu(x, w):
    seq = x.shape[0]
    t = lax.broadcasted_iota(jnp.int32, x.shape, 0)
    prev = jnp.where(t == 0, 0.0, pltpu.roll(x, 1, 0))
    nxt = jnp.where(t == seq - 1, 0.0, pltpu.roll(x, seq - 1, 0))
    return _silu(prev * w[0:1, :] + x * w[1:2, :] + nxt * w[2:3, :])


def _hi_lo(x):
    hi = x.astype(BF16)
    return hi, (x - hi.astype(F32)).astype(BF16)


def _split_lanes(a):
    hi, lo = _hi_lo(a)
    return jnp.concatenate([hi, hi, lo], axis=1)


def _split_rows(b):
    hi, lo = _hi_lo(b)
    return jnp.concatenate([hi, lo, hi], axis=0)


def _l2norm(x):
    return x * lax.rsqrt(jnp.sum(x * x, axis=-1, keepdims=True) + EPS)


def _delta_kernel(*refs, seq, hp, cpi, has_s0, write_state):
    it = iter(refs)
    q_ref, k_ref, v_ref, z_ref, gates_ref, wq_ref, wk_ref, wv_ref, gn_ref = (next(it) for _ in range(9))
    s0_ref = next(it) if has_s0 else None
    mix_ref = next(it)
    st_ref = next(it) if write_state else None
    qs, ks, vs, o_s, gc_s, b_s, u0_s, kc_s, at_s, qe_s, kd_s = (next(it) for _ in range(11))
    n_chunks = seq // DN_CHUNK
    lane = lax.broadcasted_iota(jnp.int32, (1, LANES), 1)
    row, col = _order_masks(DN_CHUNK)
    eye = (row == col).astype(F32)
    before = [col <= row, col >= row]
    strict = [col < row, col > row]
    couples = [((row >> (lvl + 1)) == (col >> (lvl + 1))) & ((row >> lvl) != (col >> lvl))
               for lvl in range(DN_CHUNK.bit_length() - 1)]

    gates = gates_ref[...]
    for hd in range(hp):
        cols = slice(hd * HEAD_DIM, (hd + 1) * HEAD_DIM)
        h = pl.program_id(1) * hp + hd
        qs[hd] = _l2norm(_conv_silu(q_ref[:, cols], wq_ref[:, cols])) * (HEAD_DIM ** -0.5)
        ks[hd] = _l2norm(_conv_silu(k_ref[:, cols], wk_ref[:, cols]))
        vs[hd] = _conv_silu(v_ref[:, cols], wv_ref[:, cols])
        for d in range(2):
            sel_g = lane == d * DN_H + h
            sel_b = lane == 2 * DN_H + d * DN_H + h
            gc_s[hd, d] = jnp.broadcast_to(
                jnp.sum(jnp.where(sel_g, gates, 0.0), axis=-1, keepdims=True), (seq, LANES))
            b_s[hd, d] = jnp.broadcast_to(
                jnp.sum(jnp.where(sel_b, gates, 0.0), axis=-1, keepdims=True), (seq, LANES))

    def chunk_rows(n):
        return pl.ds(pl.multiple_of(n * DN_CHUNK, DN_CHUNK), DN_CHUNK)

    def last_row(d, n):
        return pl.ds(n * DN_CHUNK + (DN_CHUNK - 1 if d == 0 else 0), 1)

    def wy_stages(units):
        nu = range(len(units))
        rows = [chunk_rows(n) for _, _, n in units]
        gc = [gc_s[hd, d, rows[i], :] for i, (hd, d, _) in enumerate(units)]
        bb = [b_s[hd, d, rows[i], :] for i, (hd, d, _) in enumerate(units)]
        k = [ks[hd, rows[i], :] for i, (hd, _, _) in enumerate(units)]
        k16 = [k[i].astype(BF16) for i in nu]
        kbeta = [k[i] * bb[i] for i in nu]
        kk = [_dot_nt(kbeta[i].astype(BF16), k16[i]) for i in nu]
        qk = [_dot_nt(qs[hd, rows[i], :].astype(BF16), k16[i]) for i, (hd, _, _) in enumerate(units)]
        yield
        mm = []
        for i, (hd, d, _) in enumerate(units):
            decay = jnp.where(before[d], jnp.exp(jnp.where(before[d], gc[i] - gc[i].T, 0.0)), 0.0)
            mm.append(jnp.where(strict[d], kk[i] * decay, 0.0))
            at_s[hd, d, rows[i], :] = (qk[i] * decay).astype(BF16)
        inv = [eye - jnp.where(couples[0], mm[i], 0.0) for i in nu]
        for lvl in range(1, len(couples)):
            inv16 = [inv[i].astype(BF16) for i in nu]
            c_inv = [_dot(jnp.where(couples[lvl], mm[i], 0.0).astype(BF16), inv16[i]).astype(BF16) for i in nu]
            yield
            inv = [inv[i] - _dot(inv16[i], c_inv[i]) for i in nu]
            yield
        inv16 = [inv[i].astype(BF16) for i in nu]
        egc = [jnp.exp(gc[i]) for i in nu]
        rhs = [jnp.concatenate([vs[hd, rows[i], :] * bb[i], kbeta[i] * egc[i]], axis=-1)
               for i, (hd, _, _) in enumerate(units)]
        sol = [_dot(inv16[i], rhs[i].astype(BF16)) for i in nu]
        yield
        mm_split = [_split_lanes(mm[i]) for i in nu]
        for _ in range(DN_REFINE_STEPS):
            res = [rhs[i] - sol[i] - _dot(mm_split[i], _split_rows(sol[i])) for i in nu]
            yield
            sol = [sol[i] + _dot(inv16[i], res[i].astype(BF16)) for i in nu]
            yield
        for i, (hd, d, _) in enumerate(units):
            u0_s[hd, d, rows[i], :] = sol[i][:, :HEAD_DIM]
            kc_s[hd, d, rows[i], :] = sol[i][:, HEAD_DIM:].astype(BF16)
            g_last = gc[i][DN_CHUNK - 1:DN_CHUNK, :] if d == 0 else gc[i][0:1, :]
            qe_s[hd, d, rows[i], :] = (qs[hd, rows[i], :] * egc[i]).astype(BF16)
            kd_s[hd, d, rows[i], :] = (k[i] * jnp.exp(g_last - gc[i])).astype(BF16)

    chains = [(hd, d) for hd in range(hp) for d in range(2)]

    def chunk_of(d, step):
        return step if d == 0 else n_chunks - 1 - step

    def scan_stages(state, steps):
        for step in steps:
            ns = [chunk_of(d, step) for _, d in chains]
            rows = [chunk_rows(n) for n in ns]
            s16 = [st.astype(BF16) for st in state]
            ks_ = [_dot(kc_s[hd, d, rows[c], :], s16[c]) for c, (hd, d) in enumerate(chains)]
            qs_ = [_dot(qe_s[hd, d, rows[c], :], s16[c]) for c, (hd, d) in enumerate(chains)]
            yield
            u16 = [(u0_s[hd, d, rows[c], :] - ks_[c]).astype(BF16) for c, (hd, d) in enumerate(chains)]
            au = [_dot(at_s[hd, d, rows[c], :], u16[c]) for c, (hd, d) in enumerate(chains)]
            ku = [_dot_tn(kd_s[hd, d, rows[c], :], u16[c]) for c, (hd, d) in enumerate(chains)]
            yield
            for c, (hd, d) in enumerate(chains):
                o_s[hd, d, rows[c], :] = qs_[c] + au[c]
                state[c] = state[c] * jnp.exp(gc_s[hd, d, last_row(d, ns[c]), :]) + ku[c]

    def run(*stages):
        live = list(stages)
        while live:
            for g in list(live):
                try:
                    next(g)
                except StopIteration:
                    live.remove(g)

    def group_units(j):
        return [(hd, d, chunk_of(d, j * cpi + c)) for c in range(cpi) for hd in range(hp) for d in range(2)]

    def group_steps(j):
        return [j * cpi + c for c in range(cpi)]

    n_groups = n_chunks // cpi
    if has_s0:
        init = tuple(s0_ref[d, hd] for hd, d in chains)
    else:
        init = tuple(jnp.zeros((HEAD_DIM, HEAD_DIM), F32) for _ in chains)
    run(wy_stages(group_units(0)))

    def body(j, carry):
        state = list(carry)
        run(wy_stages(group_units(j)), scan_stages(state, group_steps(j - 1)))
        return tuple(state)

    fin = list(lax.fori_loop(1, n_groups, body, init))
    run(scan_stages(fin, group_steps(n_groups - 1)))
    if write_state:
        for c, (hd, d) in enumerate(chains):
            st_ref[d, hd] = fin[c]

    for hd in range(hp):
        cols = slice(hd * HEAD_DIM, (hd + 1) * HEAD_DIM)
        y = _rms(o_s[hd, 0] + o_s[hd, 1], gn_ref[...]) * _silu(z_ref[:, cols])
        mix_ref[:, cols] = y.astype(mix_ref.dtype)


def _dn_gates_kernel(ab_ref, alog_ref, dtb_ref, o_ref):
    ab = ab_ref[...]
    lane = lax.broadcasted_iota(jnp.int32, (1, LANES), 1)
    g = -jnp.exp(alog_ref[...]) * _softplus(ab + dtb_ref[...])
    fwd = _chunk_cumsum(g, DN_CHUNK, False)
    bwd = _chunk_cumsum(g, DN_CHUNK, True)
    o_ref[...] = jnp.where(lane < DN_H, fwd, jnp.where(lane < 2 * DN_H, bwd, _sigmoid(ab)))


def _dn_gates(proj, n_batch, seq, alog_row, dtb_row):
    vec = pl.BlockSpec((1, LANES), lambda b: (0, 0))
    return pl.pallas_call(
        _dn_gates_kernel,
        grid=(n_batch,),
        in_specs=[pl.BlockSpec((seq, LANES), lambda b: (b, OD_AB // LANES)), vec, vec],
        out_specs=pl.BlockSpec((seq, LANES), lambda b: (b, 0)),
        out_shape=jax.ShapeDtypeStruct((n_batch * seq, LANES), F32),
        compiler_params=_params("parallel"),
        name="dn_gates",
    )(proj, alog_row, dtb_row)


def _delta(proj, gates, n_batch, seq, conv_w, gn, s0, write_state):
    n_chunks = seq // DN_CHUNK
    cpi = min(n_chunks, DN_UNITS // 2)
    hp = max(1, min(DN_H, DN_UNITS // (2 * cpi), DN_TOKENS_PER_STEP // seq))
    hw = hp * HEAD_DIM

    def col_spec(col0):
        return pl.BlockSpec((seq, hw), lambda b, h: (b, col0 // hw + h))

    def conv_spec(part):
        return pl.BlockSpec((3, hw), lambda b, h: (0, part * (DN_H // hp) + h))

    vec = pl.BlockSpec((1, LANES), lambda b, h: (0, 0))
    st_spec = pl.BlockSpec((None, 2, hp, HEAD_DIM, HEAD_DIM), lambda b, h: (b, 0, h, 0, 0))
    in_specs = [col_spec(OD_DQ), col_spec(OD_DK), col_spec(OD_DV), col_spec(OD_DZ),
                pl.BlockSpec((seq, LANES), lambda b, h: (b, 0)),
                conv_spec(0), conv_spec(1), conv_spec(2), vec]
    args = [proj, proj, proj, proj, gates, conv_w, conv_w, conv_w, gn]
    if s0 is not None:
        in_specs.append(st_spec)
        args.append(s0)
    mix_spec = pl.BlockSpec((seq, hw), lambda b, h: (b, h))
    mix_shape = jax.ShapeDtypeStruct((n_batch * seq, DN_H * HEAD_DIM), BF16)
    if write_state:
        out_specs = [mix_spec, st_spec]
        out_shape = [mix_shape, jax.ShapeDtypeStruct((n_batch, 2, DN_H, HEAD_DIM, HEAD_DIM), F32)]
    else:
        out_specs, out_shape = mix_spec, mix_shape
    tok = pltpu.VMEM((hp, seq, HEAD_DIM), F32)
    both = pltpu.VMEM((hp, 2, seq, HEAD_DIM), F32)
    both16 = pltpu.VMEM((hp, 2, seq, HEAD_DIM), BF16)
    return pl.pallas_call(
        functools.partial(_delta_kernel, seq=seq, hp=hp, cpi=cpi, has_s0=s0 is not None, write_state=write_state),
        grid=(n_batch, DN_H // hp),
        in_specs=in_specs,
        out_specs=out_specs,
        out_shape=out_shape,
        scratch_shapes=[tok, tok, tok, both, both, both, both, both16, both16, both16, both16],
        compiler_params=_params("parallel", "arbitrary"),
        name="deltanet",
    )(*args)


def _rope_tables(n_tok):
    t = jnp.arange(n_tok)
    inv = 1.0 / (ROPE_THETA ** (jnp.arange(ROPE_FREQ, dtype=F32) / ROPE_FREQ))
    pos = jnp.stack([t // GRID_W, t % GRID_W], axis=1).astype(F32)
    ang = pos[:, :, None] * inv
    cos = jnp.concatenate([jnp.cos(ang), jnp.cos(ang)], axis=-1).reshape(n_tok, HEAD_DIM)
    sin = jnp.concatenate([-jnp.sin(ang), jnp.sin(ang)], axis=-1).reshape(n_tok, HEAD_DIM)
    return cos, sin


def _pad_lanes(v):
    return jnp.pad(v.reshape(1, -1), ((0, 0), (0, LANES - v.size)))


def kernel(x_prompt, x_sample, state_gla, cache_gqa_k, cache_gqa_v, cache_na_k, cache_na_v, state_delta, c, c_ctx, norm1, norm2, w_ada, b_ada, w_mlp1, w_mlp2, ev_w_in, ev_w_a2, ev_b_a2, ev_gla_norm, ev_q_norm, ev_k_norm, ev_w_out, od_w_in, od_conv, od_a_log, od_dt_bias, od_dn_norm, od_rpb, od_w_out, norm_f):
    n_ctx, seq_ctx, d = x_prompt.shape
    n_lat, seq_lat, _ = x_sample.shape
    depth = w_ada.shape[0]

    w_ev = ev_w_in[0]
    w_ev = jnp.concatenate(
        [w_ev[:, :EV_AQ], w_ev[:, EV_AQ + 2 * GLA_RANK:], w_ev[:, EV_AQ:EV_AQ + 2 * GLA_RANK],
         jnp.zeros((d, EV_COLS_PAD - w_ev.shape[1]), F32)], axis=1).astype(BF16)
    w_od = jnp.pad(od_w_in[0], ((0, 0), (0, OD_COLS_PAD - od_w_in.shape[2]))).astype(BF16)
    w_in = [w_ev, w_od]
    w_out = [ev_w_out[0].astype(BF16), od_w_out[0].astype(BF16)]
    w1 = w_mlp1.astype(BF16)
    w2 = w_mlp2.astype(BF16)
    w2pad = jnp.zeros((2, LANES, GLA_H * GLA_DK), F32)
    for dd in range(2):
        w2pad = w2pad.at[dd, dd * GLA_RANK:(dd + 1) * GLA_RANK].set(ev_w_a2[0, dd])
    b2 = ev_b_a2[0].reshape(2, 1, GLA_H * GLA_DK)
    alog_row = _pad_lanes(od_a_log[0])
    dtb_row = _pad_lanes(od_dt_bias[0])
    na_bias = _na_bias_table(od_rpb[0])
    rope = _rope_tables(seq_lat)

    cond8 = jnp.concatenate([c_ctx[None, :], c, jnp.zeros((8 - 1 - n_lat, d), F32)], axis=0)
    mod = _adaln(cond8, w_ada, b_ada).reshape(depth, 8, 6, 1, d)

    def trunk(x, n_batch, seq, row_fn, caches):
        latent = caches is not None
        tm = TM_MLP
        outs = {}
        proj = _inproj(x, norm1[0:1], mod, 0, row_fn, w_in[0], TM_INPROJ)
        if latent:
            s0 = caches["gla"].reshape(n_batch, 2, GLA_H // 2, LANES, GLA_DV)
            mix_a = _gla(proj, n_batch, seq, w2pad, b2, ev_gla_norm, s0, False)
            mix_b = _attention(proj, n_batch, seq, EV_AQ, EV_AK, EV_AV, GQA_KV, GQA_H // GQA_KV,
                               norms=(ev_q_norm, ev_k_norm), rope=rope, ctx=caches["gqa"])
        else:
            mix_a, st = _gla(proj, n_batch, seq, w2pad, b2, ev_gla_norm, None, True)
            mix_b, k_new = _attention(proj, n_batch, seq, EV_AQ, EV_AK, EV_AV, GQA_KV, GQA_H // GQA_KV,
                                      norms=(ev_q_norm, ev_k_norm), write_k=True)
            outs["st_gla"] = st.reshape(n_batch, 1, 2, GLA_H, GLA_DK, GLA_DV)
            outs["ck_gqa"] = k_new.reshape(n_batch, 1, seq, GQA_KV, HEAD_DIM)
            outs["cv_gqa"] = proj[:, EV_AV:EV_AV + GQA_KV * HEAD_DIM].reshape(n_batch, 1, seq, GQA_KV, HEAD_DIM)
        x, h = _outproj(mix_a, mix_b, w_out[0], x, norm2[0:1], mod, 0, row_fn, tm)
        x = _mlp(x, h, mod, 0, row_fn, w1, w2, norm_f[None, :], False, tm)
        proj = _inproj(x, norm1[1:2], mod, 1, row_fn, w_in[1], TM_INPROJ)
        if latent:
            mix_a = _neighbourhood(proj, n_batch, seq, caches["na"][0], caches["na"][1], na_bias)
            s0 = caches["delta"].reshape(n_batch, 2, DN_H, HEAD_DIM, HEAD_DIM)
            gates = _dn_gates(proj, n_batch, seq, alog_row, dtb_row)
            mix_b = _delta(proj, gates, n_batch, seq, od_conv[0], od_dn_norm, s0, False)
        else:
            mix_a = _attention(proj, n_batch, seq, OD_NQ, OD_NK, OD_NV, NA_H, 1)
            gates = _dn_gates(proj, n_batch, seq, alog_row, dtb_row)
            mix_b, st = _delta(proj, gates, n_batch, seq, od_conv[0], od_dn_norm, None, True)
            outs["ck_na"] = proj[:, OD_NK:OD_NK + NA_H * HEAD_DIM].reshape(n_batch, 1, seq, NA_H, HEAD_DIM)
            outs["cv_na"] = proj[:, OD_NV:OD_NV + NA_H * HEAD_DIM].reshape(n_batch, 1, seq, NA_H, HEAD_DIM)
            outs["st_dn"] = st.reshape(n_batch, 1, 2, DN_H, HEAD_DIM, HEAD_DIM)
        x, h = _outproj(mix_a, mix_b, w_out[1], x, norm2[1:2], mod, 1, row_fn, tm)
        y = _mlp(x, h, mod, 1, row_fn, w1, w2, norm_f[None, :], True, tm)
        return y.reshape(n_batch, seq, d), outs

    y_prompt, new = trunk(x_prompt.reshape(n_ctx * seq_ctx, d), n_ctx, seq_ctx, lambda tok: 0, None)
    past = cache_gqa_k.shape[2]
    caches = {
        "gla": state_gla[:, 0],
        "gqa": (cache_gqa_k[:, 0].reshape(n_lat, past, GQA_KV * HEAD_DIM),
                cache_gqa_v[:, 0].reshape(n_lat, past, GQA_KV * HEAD_DIM)),
        "na": (cache_na_k[:, 0].reshape(n_lat, past, NA_H * HEAD_DIM),
               cache_na_v[:, 0].reshape(n_lat, past, NA_H * HEAD_DIM)),
        "delta": state_delta[:, 0],
    }
    y_sample, _ = trunk(x_sample.reshape(n_lat * seq_lat, d), n_lat, seq_lat,
                        lambda tok: 1 + tok // seq_lat, caches)
    return (y_prompt, y_sample, new["st_gla"], new["ck_gqa"], new["cv_gqa"], new["ck_na"], new["cv_na"],
            new["st_dn"])
```
